```python
import jax, jax.numpy as jnp
from jax import lax
import numpy as np

D_MODEL = 2048
BATCH = 2
SEQ = 8192
DEPTH = 1

MEM_LEN = 256
POOL_WINDOWS = (2, 4, 8, 16)
POOL_GROUP_DIM = D_MODEL // 16
POOL_WIDTH = len(POOL_WINDOWS) * POOL_GROUP_DIM
LRU_WIDTH = D_MODEL // 2
LRU_BLOCKS = 8
LRU_BLOCK_DIM = LRU_WIDTH // LRU_BLOCKS
CONV_WIDTH = 4
LRU_C = 8.0
MEM_HEADS = 4
MEM_HEAD_DIM = D_MODEL // 16
MEM_WIDTH = MEM_HEADS * MEM_HEAD_DIM
MIX_WIDTH = POOL_WIDTH + LRU_WIDTH + MEM_WIDTH
IN_WIDTH = POOL_WIDTH + 2 * LRU_WIDTH + MEM_WIDTH
N_GROUPS = 4
EXPERTS_PER_GROUP = 8
N_EXPERTS = N_GROUPS * EXPERTS_PER_GROUP
TOP_K = 2
EXPERT_FF = D_MODEL // 2
EXPERT_BLOCK = 128
ALPHA = (2 * DEPTH) ** 0.25
BETA = (8 * DEPTH) ** -0.25
EPS = 1e-5

kernel_name = 'hybrid_pool_rglru_memattn_hmoe_deepnorm'


def layer_norm(x, g, b):
    xf = x.astype(jnp.float32)
    mu = jnp.mean(xf, -1, keepdims=True)
    var = jnp.mean(jnp.square(xf - mu), -1, keepdims=True)
    return ((xf - mu) * lax.rsqrt(var + EPS) * g + b).astype(x.dtype)


def rms_norm(x, g):
    xf = x.astype(jnp.float32)
    return (xf * lax.rsqrt(jnp.mean(jnp.square(xf), -1, keepdims=True) + EPS) * g).astype(x.dtype)


def pool_mixer(u, w_pool, pool_scale):
    B, S, _ = u.shape
    uf = u.astype(jnp.float32)
    cs = jnp.cumsum(uf, axis=1)
    pos = jnp.arange(1, S + 1, dtype=jnp.float32)[None, :, None]
    parts = []
    for g, w in enumerate(POOL_WINDOWS):
        sl = slice(g * POOL_GROUP_DIM, (g + 1) * POOL_GROUP_DIM)
        c = cs[..., sl]
        prev = jnp.pad(c[:, :-w], ((0, 0), (w, 0), (0, 0)))
        parts.append((c - prev) / jnp.minimum(pos, float(w)) - uf[..., sl])
    d = jnp.stack(parts, axis=2).astype(u.dtype)
    y = jnp.einsum('bsgc,gcd->bsgd', d, w_pool).reshape(B, S, POOL_WIDTH)
    return y * pool_scale


def _linear_scan_combine(left, right):
    a_l, b_l = left
    a_r, b_r = right
    return a_l * a_r, a_r * b_l + b_r


def rglru_mixer(u, gate, conv_w, conv_b, w_a, b_a, w_x, b_x, lam):
    B, S, C = u.shape
    uc = lax.conv_general_dilated(u, conv_w[:, None, :], window_strides=(1,), padding=[(CONV_WIDTH - 1, 0)],
                                  dimension_numbers=('NWC', 'WIO', 'NWC'), feature_group_count=C) + conv_b
    ub = uc.reshape(B, S, LRU_BLOCKS, LRU_BLOCK_DIM)
    r = jax.nn.sigmoid(jnp.einsum('bshc,hcd->bshd', ub, w_a).reshape(B, S, C) + b_a).astype(jnp.float32)
    i = jax.nn.sigmoid(jnp.einsum('bshc,hcd->bshd', ub, w_x).reshape(B, S, C) + b_x).astype(jnp.float32)
    log_a = -LRU_C * r * jax.nn.softplus(-lam.astype(jnp.float32))
    a = jnp.exp(log_a)
    mult = jnp.sqrt(jnp.maximum(-jnp.expm1(2.0 * log_a), 0.0))
    bterm = mult * i * uc.astype(jnp.float32)
    _, h = lax.associative_scan(_linear_scan_combine, (a, bterm), axis=1)
    return jax.nn.gelu(gate) * h.astype(gate.dtype)


def memory_attention(q, mem, w_mem_kv):
    B, S, _ = q.shape
    M = mem.shape[1]
    kv = jnp.einsum('bmd,de->bme', mem, w_mem_kv)
    k = kv[..., :MEM_WIDTH].reshape(B, M, MEM_HEADS, MEM_HEAD_DIM)
    v = kv[..., MEM_WIDTH:].reshape(B, M, MEM_HEADS, MEM_HEAD_DIM)
    qh = q.reshape(B, S, MEM_HEADS, MEM_HEAD_DIM)
    s = jnp.einsum('bshd,bmhd->bhsm', qh, k).astype(jnp.float32) * (MEM_HEAD_DIM ** -0.5)
    p = jax.nn.softmax(s, axis=-1).astype(v.dtype)
    return jnp.einsum('bhsm,bmhd->bshd', p, v).reshape(B, S, MEM_WIDTH)


def mixer_sublayer(x, mem, w_in, w_pool, pool_scale, conv_w, conv_b, w_a, b_a, w_x, b_x, lam,
                   w_mem_kv, mix_norm_g, w_out):
    h = jnp.einsum('bsd,de->bse', x, w_in)
    o1 = POOL_WIDTH
    o2 = o1 + LRU_WIDTH
    o3 = o2 + LRU_WIDTH
    y_pool = pool_mixer(h[..., :o1], w_pool, pool_scale)
    y_lru = rglru_mixer(h[..., o1:o2], h[..., o2:o3], conv_w, conv_b, w_a, b_a, w_x, b_x, lam)
    y_mem = memory_attention(h[..., o3:], mem, w_mem_kv)
    m1 = POOL_WIDTH
    m2 = m1 + LRU_WIDTH
    mix = jnp.concatenate([rms_norm(y_pool, mix_norm_g[:m1]),
                           rms_norm(y_lru, mix_norm_g[m1:m2]),
                           rms_norm(y_mem, mix_norm_g[m2:])], axis=-1)
    return jnp.einsum('bse,ed->bsd', mix, w_out)


def hier_moe(x, w_group, b_group, w_fine, b_fine, w_gate, w_up, w_down):
    B, S, D = x.shape
    N = B * S
    xt = x.reshape(N, D)
    g_prob = jax.nn.softmax(jnp.dot(xt, w_group).astype(jnp.float32) + b_group, axis=-1)
    g_p, g_idx = lax.top_k(g_prob, 1)
    f_logits = jnp.dot(xt, w_fine).astype(jnp.float32).reshape(N, N_GROUPS, EXPERTS_PER_GROUP) + b_fine
    f_sel = jnp.take_along_axis(f_logits, g_idx[:, :, None], axis=1)[:, 0]
    top_l, top_j = lax.top_k(f_sel, TOP_K)
    top_w = jax.nn.softmax(top_l, axis=-1) * g_p
    e_idx = g_idx * EXPERTS_PER_GROUP + top_j
    A = N * TOP_K
    e_flat = e_idx.reshape(A)
    w_flat = top_w.reshape(A)
    tok_flat = jnp.repeat(jnp.arange(N, dtype=jnp.int32), TOP_K)
    order = jnp.argsort(e_flat, stable=True)
    e_sorted = e_flat[order]
    counts = jnp.bincount(e_flat, length=N_EXPERTS)
    starts = jnp.cumsum(counts) - counts
    padded = (counts + EXPERT_BLOCK - 1) // EXPERT_BLOCK * EXPERT_BLOCK
    pad_ends = jnp.cumsum(padded)
    pad_starts = pad_ends - padded
    dest = pad_starts[e_sorted] + jnp.arange(A, dtype=jnp.int32) - starts[e_sorted]
    n_blocks = (A + N_EXPERTS * EXPERT_BLOCK + EXPERT_BLOCK - 1) // EXPERT_BLOCK
    P = n_blocks * EXPERT_BLOCK
    buf_tok = jnp.full((P,), N, jnp.int32).at[dest].set(tok_flat[order])
    buf_w = jnp.zeros((P,), jnp.float32).at[dest].set(w_flat[order])
    block_start = jnp.arange(n_blocks, dtype=jnp.int32) * EXPERT_BLOCK
    block_exp = jnp.minimum(jnp.searchsorted(pad_ends, block_start, side='right'), N_EXPERTS - 1)
    x_pad = jnp.concatenate([xt, jnp.zeros((1, D), xt.dtype)], axis=0)
    xb = x_pad[buf_tok].reshape(n_blocks, EXPERT_BLOCK, D)

    def expert_block(args):
        xblk, e = args
        hdn = jax.nn.silu(xblk @ w_gate[e]) * (xblk @ w_up[e])
        return hdn @ w_down[e]

    yb = lax.map(expert_block, (xb, block_exp)).reshape(P, D)
    yb = yb * buf_w[:, None].astype(yb.dtype)
    out = jnp.zeros((N + 1, D), yb.dtype).at[buf_tok].add(yb)[:N]
    return out.reshape(B, S, D)


def setup_inputs(seed: int = 0) -> dict:
    key = jax.random.key(seed)
    ks = jax.random.split(key, 28)
    nrm = lambda k, shape, scale: jax.random.normal(k, shape, jnp.float32) * scale
    L, D = DEPTH, D_MODEL
    u = jax.random.uniform(ks[10], (L, LRU_WIDTH), jnp.float32, minval=0.9, maxval=0.999)
    s = u ** (1.0 / LRU_C)
    lam = jnp.log(s) - jnp.log1p(-s)
    w_mem_kv = jnp.concatenate([nrm(ks[11], (L, D, MEM_WIDTH), D ** -0.5),
                                nrm(ks[12], (L, D, MEM_WIDTH), BETA * D ** -0.5)], axis=-1)
    return {
        'x': nrm(ks[0], (BATCH, SEQ, D), 1.0),
        'mem': nrm(ks[1], (BATCH, MEM_LEN, D), 1.0),
        'w_in': nrm(ks[2], (L, D, IN_WIDTH), D ** -0.5),
        'w_pool': nrm(ks[3], (L, len(POOL_WINDOWS), POOL_GROUP_DIM, POOL_GROUP_DIM), POOL_GROUP_DIM ** -0.5),
        'pool_scale': 1.0 + nrm(ks[4], (L, POOL_WIDTH), 0.1),
        'conv_w': nrm(ks[5], (L, CONV_WIDTH, LRU_WIDTH), CONV_WIDTH ** -0.5),
        'conv_b': nrm(ks[6], (L, LRU_WIDTH), 0.02),
        'w_a': nrm(ks[7], (L, LRU_BLOCKS, LRU_BLOCK_DIM, LRU_BLOCK_DIM), LRU_BLOCK_DIM ** -0.5),
        'b_a': nrm(ks[8], (L, LRU_WIDTH), 0.02),
        'w_x': nrm(ks[9], (L, LRU_BLOCKS, LRU_BLOCK_DIM, LRU_BLOCK_DIM), LRU_BLOCK_DIM ** -0.5),
        'b_x': nrm(ks[13], (L, LRU_WIDTH), 0.02),
        'lam': lam,
        'w_mem_kv': w_mem_kv,
        'mix_norm_g': 1.0 + nrm(ks[14], (L, MIX_WIDTH), 0.02),
        'w_out': nrm(ks[15], (L, MIX_WIDTH, D), BETA * MIX_WIDTH ** -0.5),
        'ln1_g': 1.0 + nrm(ks[16], (L, D), 0.02),
        'ln1_b': nrm(ks[17], (L, D), 0.02),
        'w_group': nrm(ks[18], (L, D, N_GROUPS), D ** -0.5),
        'b_group': nrm(ks[19], (L, N_GROUPS), 0.01),
        'w_fine': nrm(ks[20], (L, D, N_EXPERTS), D ** -0.5),
        'b_fine': nrm(ks[21], (L, N_GROUPS, EXPERTS_PER_GROUP), 0.01),
        'w_gate': nrm(ks[22], (L, N_EXPERTS, D, EXPERT_FF), D ** -0.5),
        'w_up': nrm(ks[23], (L, N_EXPERTS, D, EXPERT_FF), D ** -0.5),
        'w_down': nrm(ks[24], (L, N_EXPERTS, EXPERT_FF, D), BETA * EXPERT_FF ** -0.5),
        'ln2_g': 1.0 + nrm(ks[25], (L, D), 0.02),
        'ln2_b': nrm(ks[26], (L, D), 0.02),
    }


def reference(x, mem, w_in, w_pool, pool_scale, conv_w, conv_b, w_a, b_a, w_x, b_x, lam, w_mem_kv,
              mix_norm_g, w_out, ln1_g, ln1_b, w_group, b_group, w_fine, b_fine, w_gate, w_up, w_down,
              ln2_g, ln2_b):
    for l in range(DEPTH):
        y = mixer_sublayer(x, mem, w_in[l], w_pool[l], pool_scale[l], conv_w[l], conv_b[l], w_a[l], b_a[l],
                           w_x[l], b_x[l], lam[l], w_mem_kv[l], mix_norm_g[l], w_out[l])
        x = layer_norm(ALPHA * x + y, ln1_g[l], ln1_b[l])
        y = hier_moe(x, w_group[l], b_group[l], w_fine[l], b_fine[l], w_gate[l], w_up[l], w_down[l])
        x = layer_norm(ALPHA * x + y, ln2_g[l], ln2_b[l])
    return x
```

```python
import functools

import jax
import jax.numpy as jnp
from jax import lax
from jax.experimental import pallas as pl
from jax.experimental.pallas import tpu as pltpu

F32 = jnp.float32
BF16 = jnp.bfloat16

POOL_WINDOWS = (2, 4, 8, 16)
LRU_C = 8.0
CONV_WIDTH = 4
N_GROUPS = 4
EXPERTS_PER_GROUP = 8
N_EXPERTS = N_GROUPS * EXPERTS_PER_GROUP
EPS = 1e-5

LANES = 128
HALO = 16
SEQ_TILE = 256
EXPERT_ROWS = 128
COMBINE_TILE = 512
ROUTE_LO = 64
NEG = -1e30
VMEM_LIMIT = 56 * 1024 * 1024


def _sigmoid(v):
    return 1.0 / (1.0 + jnp.exp(-v))


def _gelu_tanh(v):
    return 0.5 * v * (1.0 + jnp.tanh(0.7978845608028654 * (v + 0.044715 * (v * v * v))))


def _shift_rows(v, k):
    return pltpu.roll(v, k, 0)


def _kv_kernel(mem_ref, w_ref, kv_ref):
    kv_ref[0] = jnp.dot(mem_ref[0].astype(BF16), w_ref[...], preferred_element_type=F32).astype(BF16)


def _kv_project(mem, w_kv):
    B, M, D = mem.shape
    E = w_kv.shape[1]
    return pl.pallas_call(
        _kv_kernel,
        grid=(B,),
        in_specs=[pl.BlockSpec((1, M, D), lambda b: (b, 0, 0)),
                  pl.BlockSpec((D, E), lambda b: (0, 0))],
        out_specs=pl.BlockSpec((1, M, E), lambda b: (b, 0, 0)),
        out_shape=jax.ShapeDtypeStruct((B, M, E), BF16),
        compiler_params=pltpu.CompilerParams(dimension_semantics=("arbitrary",),
                                             vmem_limit_bytes=VMEM_LIMIT),
        name="kv_project",
    )(mem, w_kv)


def _mixer_kernel(dims, alpha,
                  x_ref, kv_ref, w_in_ref, w_pool_ref, pool_scale_ref, conv_w_ref, conv_b_ref,
                  w_ax_ref, b_ax_ref, nsp_ref, norm_g_ref, w_out_ref, ln_g_ref, ln_b_ref,
                  w_route_ref, b_route_ref,
                  x1_ref, route_ref, counts_ref,
                  h_ref, mix_ref, halo_ref, state_ref, cnt_ref):
    T, pool_w, lru_w, mem_w, n_heads, head_dim, gdim, bdim = dims
    b = pl.program_id(0)
    s = pl.program_id(1)
    o_lru = pool_w
    o_gate = o_lru + lru_w
    o_q = o_gate + lru_w

    @pl.when(s == 0)
    def _():
        halo_ref[...] = jnp.zeros_like(halo_ref)
        state_ref[...] = jnp.zeros_like(state_ref)

    @pl.when((b == 0) & (s == 0))
    def _():
        cnt_ref[...] = jnp.zeros_like(cnt_ref)

    x = x_ref[0]
    h_ref[...] = jnp.dot(x.astype(BF16), w_in_ref[...], preferred_element_type=F32)

    row = lax.broadcasted_iota(jnp.int32, (T, 1), 0)
    pos = (s * T + row + 1).astype(F32)

    ext = jnp.concatenate([halo_ref[:, :pool_w], h_ref[:, :pool_w]], axis=0)
    ssq = jnp.zeros((T, 1), F32)
    win = ext
    span = 1
    for g, w in enumerate(POOL_WINDOWS):
        while span < w:
            win = win + _shift_rows(win, span)
            span *= 2
        sw = win[HALO:, :gdim]
        if g + 1 < len(POOL_WINDOWS):
            win = win[:, gdim:]
        u = h_ref[:, g * gdim:(g + 1) * gdim]
        d = sw / jnp.minimum(pos, float(w)) - u
        y = jnp.dot(d.astype(BF16), w_pool_ref[g], preferred_element_type=F32)
        y = y * pool_scale_ref[:, g * gdim:(g + 1) * gdim]
        ssq = ssq + jnp.sum(y * y, axis=-1, keepdims=True)
        mix_ref[:, g * gdim:(g + 1) * gdim] = y
    inv = lax.rsqrt(ssq / pool_w + EPS)
    mix_ref[:, :pool_w] = mix_ref[:, :pool_w] * inv * norm_g_ref[:, :pool_w]

    ssq = jnp.zeros((T, 1), F32)
    for hh in range(lru_w // bdim):
        c0 = hh * bdim
        ext = jnp.concatenate([halo_ref[:, o_lru + c0:o_lru + c0 + bdim],
                               h_ref[:, o_lru + c0:o_lru + c0 + bdim]], axis=0)
        cw = conv_w_ref[:, c0:c0 + bdim]
        uc = ext * cw[CONV_WIDTH - 1:CONV_WIDTH]
        for k in range(1, CONV_WIDTH):
            uc = uc + _shift_rows(ext, k) * cw[CONV_WIDTH - 1 - k:CONV_WIDTH - k]
        uc = uc[HALO:] + conv_b_ref[:, c0:c0 + bdim]
        ga = jnp.dot(uc.astype(BF16), w_ax_ref[hh], preferred_element_type=F32)
        r = _sigmoid(ga[:, :bdim] + b_ax_ref[:, c0:c0 + bdim])
        i = _sigmoid(ga[:, bdim:] + b_ax_ref[:, lru_w + c0:lru_w + c0 + bdim])
        log_a = -LRU_C * r * nsp_ref[:, c0:c0 + bdim]
        a = jnp.exp(log_a)
        mult = jnp.sqrt(jnp.maximum(1.0 - a * a, 0.0))
        bt = mult * i * uc
        k = 1
        while k < T:
            if k < 8:
                keep = row >= k
                a_prev = jnp.where(keep, _shift_rows(a, k), 1.0)
                b_prev = jnp.where(keep, _shift_rows(bt, k), 0.0)
                bt = bt + a * b_prev
                a = a * a_prev
            else:
                bt = jnp.concatenate([bt[:k], bt[k:] + a[k:] * bt[:T - k]], axis=0)
                a = jnp.concatenate([a[:k], a[k:] * a[:T - k]], axis=0)
            k *= 2
        hs = a * state_ref[:, c0:c0 + bdim] + bt
        state_ref[:, c0:c0 + bdim] = hs[T - 1:T]
        y = _gelu_tanh(h_ref[:, o_gate + c0:o_gate + c0 + bdim]) * hs
        ssq = ssq + jnp.sum(y * y, axis=-1, keepdims=True)
        mix_ref[:, o_lru + c0:o_lru + c0 + bdim] = y
    inv = lax.rsqrt(ssq / lru_w + EPS)
    mix_ref[:, o_lru:o_gate] = mix_ref[:, o_lru:o_gate] * inv * norm_g_ref[:, o_lru:o_gate]

    halo_ref[...] = h_ref[T - HALO:, :o_gate]

    ssq = jnp.zeros((T, 1), F32)
    m0 = o_gate
    for hh in range(n_heads):
        c0 = hh * head_dim
        q = h_ref[:, o_q + c0:o_q + c0 + head_dim].astype(BF16)
        kh = kv_ref[0, :, c0:c0 + head_dim]
        vh = kv_ref[0, :, mem_w + c0:mem_w + c0 + head_dim]
        sc = lax.dot_general(q, kh, (((1,), (1,)), ((), ())), preferred_element_type=F32) * (head_dim ** -0.5)
        p = jnp.exp(sc - jnp.max(sc, axis=-1, keepdims=True))
        p = p / jnp.sum(p, axis=-1, keepdims=True)
        y = jnp.dot(p.astype(BF16), vh, preferred_element_type=F32)
        ssq = ssq + jnp.sum(y * y, axis=-1, keepdims=True)
        mix_ref[:, m0 + c0:m0 + c0 + head_dim] = y
    inv = lax.rsqrt(ssq / mem_w + EPS)
    mix_ref[:, m0:] = mix_ref[:, m0:] * inv * norm_g_ref[:, m0:]

    y = jnp.dot(mix_ref[...].astype(BF16), w_out_ref[...], preferred_element_type=F32)
    z = alpha * x + y
    mu = jnp.mean(z, axis=-1, keepdims=True)
    zc = z - mu
    var = jnp.mean(zc * zc, axis=-1, keepdims=True)
    x1 = zc * lax.rsqrt(var + EPS) * ln_g_ref[...] + ln_b_ref[...]
    x1_ref[0] = x1

    xh = x1.astype(BF16)
    xl = (x1 - xh.astype(F32)).astype(BF16)
    lg = jnp.dot(jnp.concatenate([xh, xl], axis=0), w_route_ref[...], preferred_element_type=F32)
    top = lg[:T]
    logits = top + lg[T:] + pltpu.roll(top, LANES - ROUTE_LO, 1) + b_route_ref[...]
    lane_i = lax.broadcasted_iota(jnp.int32, (T, LANES), 1)
    lane = lane_i.astype(F32)
    lane_grp = ((lane_i - N_GROUPS) >> 3).astype(F32)

    is_g = lane_i < N_GROUPS
    gl = jnp.where(is_g, logits, NEG)
    gmax = jnp.max(gl, axis=-1, keepdims=True)
    gsum = jnp.sum(jnp.where(is_g, jnp.exp(gl - gmax), 0.0), axis=-1, keepdims=True)
    g_p = 1.0 / gsum
    g_idx = jnp.min(jnp.where(gl == gmax, lane, float(LANES)), axis=-1, keepdims=True)

    in_grp = (lane_i >= N_GROUPS) & (lane_i < N_GROUPS + N_EXPERTS) & (lane_grp == g_idx)
    fl = jnp.where(in_grp, logits, NEG)
    l1 = jnp.max(fl, axis=-1, keepdims=True)
    i1 = jnp.min(jnp.where(fl == l1, lane, float(LANES)), axis=-1, keepdims=True)
    fl2 = jnp.where(lane == i1, NEG, fl)
    l2 = jnp.max(fl2, axis=-1, keepdims=True)
    i2 = jnp.min(jnp.where(fl2 == l2, lane, float(LANES)), axis=-1, keepdims=True)
    e21 = jnp.exp(l2 - l1)
    w1 = g_p / (1.0 + e21)
    w2 = g_p * e21 / (1.0 + e21)
    e1 = i1 - N_GROUPS
    e2 = i2 - N_GROUPS

    oh1 = (lane == e1).astype(F32)
    oh2 = (lane == e2).astype(F32)
    both = oh1 + oh2
    tri = (lax.broadcasted_iota(jnp.int32, (T, T), 0) > lax.broadcasted_iota(jnp.int32, (T, T), 1))
    pfx = jnp.dot(tri.astype(F32).astype(BF16), both.astype(BF16), preferred_element_type=F32)
    base = pfx + cnt_ref[...]
    r1 = jnp.sum(oh1 * base, axis=-1, keepdims=True)
    r2 = jnp.sum(oh2 * base, axis=-1, keepdims=True)
    cnt_ref[...] = cnt_ref[...] + jnp.sum(both, axis=0, keepdims=True)
    counts_ref[...] = jnp.broadcast_to(cnt_ref[...], counts_ref.shape)

    route = jnp.where(lane_i == 0, e1,
            jnp.where(lane_i == 1, e2,
            jnp.where(lane_i == 2, r1,
            jnp.where(lane_i == 3, r2,
            jnp.where(lane_i == 4, w1,
            jnp.where(lane_i == 5, w2, 0.0))))))
    route_ref[0] = route


def _mixer(x, kv, w_in, w_pool, pool_scale, conv_w, conv_b, w_ax, b_ax, nsp, norm_g, w_out, ln_g, ln_b,
           w_route, b_route, alpha):
    B, S, D = x.shape
    T = SEQ_TILE
    in_w = w_in.shape[1]
    pool_w = pool_scale.shape[1]
    lru_w = conv_b.shape[1]
    mem_w = kv.shape[2] // 2
    gdim = w_pool.shape[1]
    bdim = w_ax.shape[1]
    n_heads = 4
    head_dim = mem_w // n_heads
    mix_w = w_out.shape[0]
    dims = (T, pool_w, lru_w, mem_w, n_heads, head_dim, gdim, bdim)
    assert S % T == 0 and T % 8 == 0

    const2 = lambda b, s: (0, 0)
    const3 = lambda b, s: (0, 0, 0)
    single = pl.Buffered(1)

    def full(a):
        return pl.BlockSpec(a.shape, const2 if a.ndim == 2 else const3, pipeline_mode=single)

    in_specs = [
        pl.BlockSpec((1, T, D), lambda b, s: (b, s, 0)),
        pl.BlockSpec((1,) + kv.shape[1:], lambda b, s: (b, 0, 0)),
        full(w_in), full(w_pool), full(pool_scale), full(conv_w), full(conv_b), full(w_ax), full(b_ax),
        full(nsp), full(norm_g), full(w_out), full(ln_g), full(ln_b), full(w_route), full(b_route),
    ]
    out_specs = [
        pl.BlockSpec((1, T, D), lambda b, s: (b, s, 0)),
        pl.BlockSpec((1, T, LANES), lambda b, s: (b, s, 0)),
        pl.BlockSpec((8, LANES), const2),
    ]
    out_shape = [
        jax.ShapeDtypeStruct((B, S, D), F32),
        jax.ShapeDtypeStruct((B, S, LANES), F32),
        jax.ShapeDtypeStruct((8, LANES), F32),
    ]
    scratch = [
        pltpu.VMEM((T, in_w), F32),
        pltpu.VMEM((T, mix_w), F32),
        pltpu.VMEM((HALO, pool_w + lru_w), F32),
        pltpu.VMEM((1, lru_w), F32),
        pltpu.VMEM((1, LANES), F32),
    ]
    return pl.pallas_call(
        functools.partial(_mixer_kernel, dims, alpha),
        grid=(B, S // T),
        in_specs=in_specs,
        out_specs=out_specs,
        out_shape=out_shape,
        scratch_shapes=scratch,
        compiler_params=pltpu.CompilerParams(dimension_semantics=("arbitrary", "arbitrary"),
                                             vmem_limit_bytes=VMEM_LIMIT),
        name="mixer_router",
    )(x, kv, w_in, w_pool, pool_scale, conv_w, conv_b, w_ax, b_ax, nsp, norm_g, w_out, ln_g, ln_b,
      w_route, b_route)


def _ffn_kernel(bexp_ref, idx_cur_ref, idx_nxt_ref, x1_hbm, wg_ref, wu_ref, wd_ref, ys_hbm,
                xbuf, obuf, gsem, ssem):
    del bexp_ref
    R = EXPERT_ROWS
    r = pl.program_id(0)
    nb = pl.num_programs(0)
    cur = r % 2

    def gather_copy(tok, i, sl):
        return pltpu.make_async_copy(x1_hbm.at[pl.ds(tok, 1)], xbuf.at[sl, pl.ds(i, 1)], gsem.at[sl])

    def scatter_copy(dst, i, sl):
        return pltpu.make_async_copy(obuf.at[sl, pl.ds(i, 1)], ys_hbm.at[pl.ds(dst, 1)], ssem.at[sl])

    def start_gathers(idx_ref, sl):
        def body(i, c):
            gather_copy(idx_ref[0, 0, i], i, sl).start()
            return c
        lax.fori_loop(0, R, body, 0)

    def wait_gathers(sl):
        def body(i, c):
            gather_copy(0, i, sl).wait()
            return c
        lax.fori_loop(0, R, body, 0)

    def wait_scatters(sl):
        def body(i, c):
            scatter_copy(0, i, sl).wait()
            return c
        lax.fori_loop(0, R, body, 0)

    @pl.when(r == 0)
    def _():
        start_gathers(idx_cur_ref, 0)

    @pl.when(r + 1 < nb)
    def _():
        start_gathers(idx_nxt_ref, 1 - cur)

    wait_gathers(cur)

    @pl.when(r >= 2)
    def _():
        wait_scatters(cur)

    xb = xbuf[cur].astype(BF16)
    g = jnp.dot(xb, wg_ref[0], preferred_element_type=F32)
    u = jnp.dot(xb, wu_ref[0], preferred_element_type=F32)
    hdn = (g * _sigmoid(g)) * u
    obuf[cur] = jnp.dot(hdn.astype(BF16), wd_ref[0], preferred_element_type=F32)

    def body(i, c):
        scatter_copy(idx_cur_ref[0, 0, R + i], i, cur).start()
        return c
    lax.fori_loop(0, R, body, 0)

    @pl.when(r == nb - 1)
    def _():
        wait_scatters(cur)

        @pl.when(nb >= 2)
        def _():
            wait_scatters(1 - cur)


def _expert_ffn(x1, idx, block_exp, w_gate, w_up, w_down, n_rows_out):
    N, D = x1.shape
    nb = idx.shape[0]
    R = EXPERT_ROWS
    FF = w_gate.shape[2]
    grid_spec = pltpu.PrefetchScalarGridSpec(
        num_scalar_prefetch=1,
        grid=(nb,),
        in_specs=[
            pl.BlockSpec((1, 1, 2 * R), lambda r, be: (r, 0, 0), memory_space=pltpu.SMEM),
            pl.BlockSpec((1, 1, 2 * R), lambda r, be: (jnp.minimum(r + 1, nb - 1), 0, 0),
                         memory_space=pltpu.SMEM),
            pl.BlockSpec(memory_space=pl.ANY),
            pl.BlockSpec((1, D, FF), lambda r, be: (be[r], 0, 0)),
            pl.BlockSpec((1, D, FF), lambda r, be: (be[r], 0, 0)),
            pl.BlockSpec((1, FF, D), lambda r, be: (be[r], 0, 0)),
        ],
        out_specs=pl.BlockSpec(memory_space=pl.ANY),
        scratch_shapes=[
            pltpu.VMEM((2, R, D), F32),
            pltpu.VMEM((2, R, D), F32),
            pltpu.SemaphoreType.DMA((2,)),
            pltpu.SemaphoreType.DMA((2,)),
        ],
    )
    return pl.pallas_call(
        _ffn_kernel,
        grid_spec=grid_spec,
        out_shape=jax.ShapeDtypeStruct((n_rows_out, D), F32),
        compiler_params=pltpu.CompilerParams(dimension_semantics=("arbitrary",),
                                             vmem_limit_bytes=VMEM_LIMIT),
        name="expert_ffn",
    )(block_exp, idx, idx, x1, w_gate, w_up, w_down)


def _combine_kernel(alpha, x1_ref, y0_ref, y1_ref, route_ref, g_ref, b_ref, out_ref):
    rt = route_ref[...]
    z = alpha * x1_ref[...] + rt[:, 4:5] * y0_ref[...] + rt[:, 5:6] * y1_ref[...]
    mu = jnp.mean(z, axis=-1, keepdims=True)
    zc = z - mu
    var = jnp.mean(zc * zc, axis=-1, keepdims=True)
    out_ref[...] = zc * lax.rsqrt(var + EPS) * g_ref[...] + b_ref[...]


def _combine(x1, ys, route, ln_g, ln_b, alpha):
    N, D = x1.shape
    T = COMBINE_TILE
    assert N % T == 0
    off = N // T
    return pl.pallas_call(
        functools.partial(_combine_kernel, alpha),
        grid=(N // T,),
        in_specs=[
            pl.BlockSpec((T, D), lambda i: (i, 0)),
            pl.BlockSpec((T, D), lambda i: (i, 0)),
            pl.BlockSpec((T, D), lambda i: (i + off, 0)),
            pl.BlockSpec((T, LANES), lambda i: (i, 0)),
            pl.BlockSpec((1, D), lambda i: (0, 0)),
            pl.BlockSpec((1, D), lambda i: (0, 0)),
        ],
        out_specs=pl.BlockSpec((T, D), lambda i: (i, 0)),
        out_shape=jax.ShapeDtypeStruct((N, D), F32),
        compiler_params=pltpu.CompilerParams(dimension_semantics=("arbitrary",),
                                             vmem_limit_bytes=VMEM_LIMIT),
        name="combine_ln",
    )(x1, ys, ys, route, ln_g, ln_b)


def _route_weights(w_group, w_fine):
    w = jnp.concatenate([w_group, w_fine], axis=1)
    wh = w.astype(BF16)
    wl = (w - wh.astype(F32)).astype(BF16)
    n = w.shape[1]
    out = jnp.zeros((w.shape[0], LANES), BF16)
    out = out.at[:, :n].set(wh)
    return out.at[:, ROUTE_LO:ROUTE_LO + n].set(wl)


def _layer(x, mem, w_in, w_pool, pool_scale, conv_w, conv_b, w_a, b_a, w_x, b_x, lam, w_mem_kv, mix_norm_g,
           w_out, ln1_g, ln1_b, w_group, b_group, w_fine, b_fine, w_gate, w_up, w_down, ln2_g, ln2_b, alpha):
    B, S, D = x.shape
    N = B * S
    R = EXPERT_ROWS
    row = lambda v: v.reshape(1, -1).astype(F32)

    kv = _kv_project(mem, w_mem_kv.astype(BF16))
    w_ax = jnp.concatenate([w_a, w_x], axis=2).astype(BF16)
    b_ax = jnp.concatenate([b_a, b_x]).reshape(1, -1)
    nsp = row(jax.nn.softplus(-lam.astype(F32)))
    b_route = jnp.zeros((1, LANES), F32).at[0, :N_GROUPS + N_EXPERTS].set(
        jnp.concatenate([b_group, b_fine.reshape(-1)]))
    x1, route, counts = _mixer(
        x, kv, w_in.astype(BF16), w_pool.astype(BF16), row(pool_scale), conv_w, row(conv_b), w_ax, b_ax, nsp,
        row(mix_norm_g), w_out.astype(BF16), row(ln1_g), row(ln1_b), _route_weights(w_group, w_fine), b_route,
        alpha)
    x1 = x1.reshape(N, D)
    route = route.reshape(N, LANES)

    e_idx = route[:, 0:2].astype(jnp.int32)
    rank = route[:, 2:4].astype(jnp.int32)
    cnt = counts[0, :N_EXPERTS].astype(jnp.int32)
    padded = (cnt + R - 1) // R * R
    pad_ends = jnp.cumsum(padded)
    pad_starts = pad_ends - padded
    dest = pad_starts[e_idx] + rank
    n_blocks = (2 * N + N_EXPERTS * R) // R
    P = n_blocks * R
    block_exp = jnp.minimum(jnp.searchsorted(pad_ends, jnp.arange(n_blocks, dtype=jnp.int32) * R, side='right'),
                            N_EXPERTS - 1).astype(jnp.int32)
    slot = jnp.arange(N, dtype=jnp.int32)[:, None] + jnp.array([0, N], jnp.int32)[None, :]
    p = jnp.arange(P, dtype=jnp.int32)
    spare = 2 * N + ((p // R) % 2) * R + (p % R)
    buf_slot = spare.at[dest.reshape(-1)].set(slot.reshape(-1), unique_indices=True)
    buf_tok = jnp.where(buf_slot < 2 * N, buf_slot % N, 0)
    idx = jnp.concatenate([buf_tok.reshape(n_blocks, 1, R), buf_slot.reshape(n_blocks, 1, R)], axis=2)

    ys = _expert_ffn(x1, idx, block_exp, w_gate.astype(BF16), w_up.astype(BF16), w_down.astype(BF16),
                     2 * N + 2 * R)
    out = _combine(x1, ys, route, row(ln2_g), row(ln2_b), alpha)
    return out.reshape(B, S, D)


def kernel(x, mem, w_in, w_pool, pool_scale, conv_w, conv_b, w_a, b_a, w_x, b_x, lam, w_mem_kv, mix_norm_g, w_out,
           ln1_g, ln1_b, w_group, b_group, w_fine, b_fine, w_gate, w_up, w_down, ln2_g, ln2_b):
    depth = w_in.shape[0]
    alpha = (2 * depth) ** 0.25
    for l in range(depth):
        x = _layer(x, mem, w_in[l], w_pool[l], pool_scale[l], conv_w[l], conv_b[l], w_a[l], b_a[l], w_x[l], b_x[l],
                   lam[l], w_mem_kv[l], mix_norm_g[l], w_out[l], ln1_g[l], ln1_b[l], w_group[l], b_group[l],
                   w_fine[l], b_fine[l], w_gate[l], w_up[l], w_down[l], ln2_g[l], ln2_b[l], alpha)
    return x
```

```python
import functools

import jax
import jax.numpy as jnp
from jax import lax
from jax.experimental import pallas as pl
from jax.experimental.pallas import tpu as pltpu

F32 = jnp.float32
BF16 = jnp.bfloat16

POOL_WINDOWS = (2, 4, 8, 16)
LRU_C = 8.0
CONV_WIDTH = 4
N_GROUPS = 4
EXPERTS_PER_GROUP = 8
N_EXPERTS = N_GROUPS * EXPERTS_PER_GROUP
EPS = 1e-5

LANES = 128
HALO = 16
SEQ_TILE = 256
EXPERT_ROWS = 128
COMBINE_TILE = 512
ROUTE_LO = 64
NEG = -1e30
VMEM_LIMIT = 56 * 1024 * 1024


def _sigmoid(v):
    return 1.0 / (1.0 + jnp.exp(-v))


def _gelu_tanh(v):
    return 0.5 * v * (1.0 + jnp.tanh(0.7978845608028654 * (v + 0.044715 * (v * v * v))))


def _shift_rows(v, k):
    return pltpu.roll(v, k, 0)


def _kv_kernel(mem_ref, w_ref, kv_ref):
    kv_ref[0] = jnp.dot(mem_ref[0].astype(BF16), w_ref[...], preferred_element_type=F32).astype(BF16)


def _kv_project(mem, w_kv):
    B, M, D = mem.shape
    E = w_kv.shape[1]
    return pl.pallas_call(
        _kv_kernel,
        grid=(B,),
        in_specs=[pl.BlockSpec((1, M, D), lambda b: (b, 0, 0)),
                  pl.BlockSpec((D, E), lambda b: (0, 0))],
        out_specs=pl.BlockSpec((1, M, E), lambda b: (b, 0, 0)),
        out_shape=jax.ShapeDtypeStruct((B, M, E), BF16),
        compiler_params=pltpu.CompilerParams(dimension_semantics=("arbitrary",),
                                             vmem_limit_bytes=VMEM_LIMIT),
        name="kv_project",
    )(mem, w_kv)


def _mixer_kernel(dims, alpha,
                  x_ref, kv_ref, w_in_ref, w_pool_ref, pool_scale_ref, conv_w_ref, conv_b_ref,
                  w_ax_ref, b_ax_ref, nsp_ref, norm_g_ref, w_out_ref, ln_g_ref, ln_b_ref,
                  w_route_ref, b_route_ref,
                  x1_ref, route_ref, counts_ref,
                  h_ref, mix_ref, halo_ref, state_ref, cnt_ref):
    T, pool_w, lru_w, mem_w, n_heads, head_dim, gdim, bdim = dims
    b = pl.program_id(0)
    s = pl.program_id(1)
    o_lru = pool_w
    o_gate = o_lru + lru_w
    o_q = o_gate + lru_w

    @pl.when(s == 0)
    def _():
        halo_ref[...] = jnp.zeros_like(halo_ref)
        state_ref[...] = jnp.zeros_like(state_ref)

    @pl.when((b == 0) & (s == 0))
    def _():
        cnt_ref[...] = jnp.zeros_like(cnt_ref)

    x = x_ref[0]
    h_ref[...] = jnp.dot(x.astype(BF16), w_in_ref[...], preferred_element_type=F32)

    row = lax.broadcasted_iota(jnp.int32, (T, 1), 0)
    pos = (s * T + row + 1).astype(F32)

    ext = jnp.concatenate([halo_ref[:, :pool_w], h_ref[:, :pool_w]], axis=0)
    ssq = jnp.zeros((T, 1), F32)
    win = ext
    span = 1
    for g, w in enumerate(POOL_WINDOWS):
        while span < w:
            win = win + _shift_rows(win, span)
            span *= 2
        sw = win[HALO:, :gdim]
        if g + 1 < len(POOL_WINDOWS):
            win = win[:, gdim:]
        u = h_ref[:, g * gdim:(g + 1) * gdim]
        d = sw / jnp.minimum(pos, float(w)) - u
        y = jnp.dot(d.astype(BF16), w_pool_ref[g], preferred_element_type=F32)
        y = y * pool_scale_ref[:, g * gdim:(g + 1) * gdim]
        ssq = ssq + jnp.sum(y * y, axis=-1, keepdims=True)
        mix_ref[:, g * gdim:(g + 1) * gdim] = y
    inv = lax.rsqrt(ssq / pool_w + EPS)
    mix_ref[:, :pool_w] = mix_ref[:, :pool_w] * inv * norm_g_ref[:, :pool_w]

    ssq = jnp.zeros((T, 1), F32)
    for hh in range(lru_w // bdim):
        c0 = hh * bdim
        ext = jnp.concatenate([halo_ref[:, o_lru + c0:o_lru + c0 + bdim],
                               h_ref[:, o_lru + c0:o_lru + c0 + bdim]], axis=0)
        cw = conv_w_ref[:, c0:c0 + bdim]
        uc = ext * cw[CONV_WIDTH - 1:CONV_WIDTH]
        for k in range(1, CONV_WIDTH):
            uc = uc + _shift_rows(ext, k) * cw[CONV_WIDTH - 1 - k:CONV_WIDTH - k]
        uc = uc[HALO:] + conv_b_ref[:, c0:c0 + bdim]
        ga = jnp.dot(uc.astype(BF16), w_ax_ref[hh], preferred_element_type=F32)
        r = _sigmoid(ga[:, :bdim] + b_ax_ref[:, c0:c0 + bdim])
        i = _sigmoid(ga[:, bdim:] + b_ax_ref[:, lru_w + c0:lru_w + c0 + bdim])
        log_a = -LRU_C * r * nsp_ref[:, c0:c0 + bdim]
        a = jnp.exp(log_a)
        mult = jnp.sqrt(jnp.maximum(1.0 - a * a, 0.0))
        bt = mult * i * uc
        k = 1
        while k < T:
            if k < 8:
                keep = row >= k
                a_prev = jnp.where(keep, _shift_rows(a, k), 1.0)
                b_prev = jnp.where(keep, _shift_rows(bt, k), 0.0)
                bt = bt + a * b_prev
                a = a * a_prev
            else:
                bt = jnp.concatenate([bt[:k], bt[k:] + a[k:] * bt[:T - k]], axis=0)
                a = jnp.concatenate([a[:k], a[k:] * a[:T - k]], axis=0)
            k *= 2
        hs = a * state_ref[:, c0:c0 + bdim] + bt
        state_ref[:, c0:c0 + bdim] = hs[T - 1:T]
        y = _gelu_tanh(h_ref[:, o_gate + c0:o_gate + c0 + bdim]) * hs
        ssq = ssq + jnp.sum(y * y, axis=-1, keepdims=True)
        mix_ref[:, o_lru + c0:o_lru + c0 + bdim] = y
    inv = lax.rsqrt(ssq / lru_w + EPS)
    mix_ref[:, o_lru:o_gate] = mix_ref[:, o_lru:o_gate] * inv * norm_g_ref[:, o_lru:o_gate]

    halo_ref[...] = h_ref[T - HALO:, :o_gate]

    ssq = jnp.zeros((T, 1), F32)
    m0 = o_gate
    for hh in range(n_heads):
        c0 = hh * head_dim
        q = h_ref[:, o_q + c0:o_q + c0 + head_dim].astype(BF16)
        kh = kv_ref[0, :, c0:c0 + head_dim]
        vh = kv_ref[0, :, mem_w + c0:mem_w + c0 + head_dim]
        sc = lax.dot_general(q, kh, (((1,), (1,)), ((), ())), preferred_element_type=F32) * (head_dim ** -0.5)
        p = jnp.exp(sc - jnp.max(sc, axis=-1, keepdims=True))
        p = p / jnp.sum(p, axis=-1, keepdims=True)
        y = jnp.dot(p.astype(BF16), vh, preferred_element_type=F32)
        ssq = ssq + jnp.sum(y * y, axis=-1, keepdims=True)
        mix_ref[:, m0 + c0:m0 + c0 + head_dim] = y
    inv = lax.rsqrt(ssq / mem_w + EPS)
    mix_ref[:, m0:] = mix_ref[:, m0:] * inv * norm_g_ref[:, m0:]

    y = jnp.dot(mix_ref[...].astype(BF16), w_out_ref[...], preferred_element_type=F32)
    z = alpha * x + y
    mu = jnp.mean(z, axis=-1, keepdims=True)
    zc = z - mu
    var = jnp.mean(zc * zc, axis=-1, keepdims=True)
    x1 = zc * lax.rsqrt(var + EPS) * ln_g_ref[...] + ln_b_ref[...]
    x1_ref[0] = x1

    xh = x1.astype(BF16)
    xl = (x1 - xh.astype(F32)).astype(BF16)
    lg = jnp.dot(jnp.concatenate([xh, xl], axis=0), w_route_ref[...], preferred_element_type=F32)
    top = lg[:T]
    logits = top + lg[T:] + pltpu.roll(top, LANES - ROUTE_LO, 1) + b_route_ref[...]
    lane_i = lax.broadcasted_iota(jnp.int32, (T, LANES), 1)
    lane = lane_i.astype(F32)
    lane_grp = ((lane_i - N_GROUPS) >> 3).astype(F32)

    is_g = lane_i < N_GROUPS
    gl = jnp.where(is_g, logits, NEG)
    gmax = jnp.max(gl, axis=-1, keepdims=True)
    gsum = jnp.sum(jnp.where(is_g, jnp.exp(gl - gmax), 0.0), axis=-1, keepdims=True)
    g_p = 1.0 / gsum
    g_idx = jnp.min(jnp.where(gl == gmax, lane, float(LANES)), axis=-1, keepdims=True)

    in_grp = (lane_i >= N_GROUPS) & (lane_i < N_GROUPS + N_EXPERTS) & (lane_grp == g_idx)
    fl = jnp.where(in_grp, logits, NEG)
    l1 = jnp.max(fl, axis=-1, keepdims=True)
    i1 = jnp.min(jnp.where(fl == l1, lane, float(LANES)), axis=-1, keepdims=True)
    fl2 = jnp.where(lane == i1, NEG, fl)
    l2 = jnp.max(fl2, axis=-1, keepdims=True)
    i2 = jnp.min(jnp.where(fl2 == l2, lane, float(LANES)), axis=-1, keepdims=True)
    e21 = jnp.exp(l2 - l1)
    w1 = g_p / (1.0 + e21)
    w2 = g_p * e21 / (1.0 + e21)
    e1 = i1 - N_GROUPS
    e2 = i2 - N_GROUPS

    oh1 = (lane == e1).astype(F32)
    oh2 = (lane == e2).astype(F32)
    both = oh1 + oh2
    tri = (lax.broadcasted_iota(jnp.int32, (T, T), 0) > lax.broadcasted_iota(jnp.int32, (T, T), 1))
    pfx = jnp.dot(tri.astype(F32).astype(BF16), both.astype(BF16), preferred_element_type=F32)
    base = pfx + cnt_ref[...]
    r1 = jnp.sum(oh1 * base, axis=-1, keepdims=True)
    r2 = jnp.sum(oh2 * base, axis=-1, keepdims=True)
    cnt_ref[...] = cnt_ref[...] + jnp.sum(both, axis=0, keepdims=True)
    counts_ref[...] = jnp.broadcast_to(cnt_ref[...], counts_ref.shape)

    route = jnp.where(lane_i == 0, e1,
            jnp.where(lane_i == 1, e2,
            jnp.where(lane_i == 2, r1,
            jnp.where(lane_i == 3, r2,
            jnp.where(lane_i == 4, w1,
            jnp.where(lane_i == 5, w2, 0.0))))))
    route_ref[0] = route


def _mixer(x, kv, w_in, w_pool, pool_scale, conv_w, conv_b, w_ax, b_ax, nsp, norm_g, w_out, ln_g, ln_b,
           w_route, b_route, alpha):
    B, S, D = x.shape
    T = SEQ_TILE
    in_w = w_in.shape[1]
    pool_w = pool_scale.shape[1]
    lru_w = conv_b.shape[1]
    mem_w = kv.shape[2] // 2
    gdim = w_pool.shape[1]
    bdim = w_ax.shape[1]
    n_heads = 4
    head_dim = mem_w // n_heads
    mix_w = w_out.shape[0]
    dims = (T, pool_w, lru_w, mem_w, n_heads, head_dim, gdim, bdim)
    assert S % T == 0 and T % 8 == 0

    const2 = lambda b, s: (0, 0)
    const3 = lambda b, s: (0, 0, 0)
    single = pl.Buffered(1)

    def full(a):
        return pl.BlockSpec(a.shape, const2 if a.ndim == 2 else const3, pipeline_mode=single)

    in_specs = [
        pl.BlockSpec((1, T, D), lambda b, s: (b, s, 0)),
        pl.BlockSpec((1,) + kv.shape[1:], lambda b, s: (b, 0, 0)),
        full(w_in), full(w_pool), full(pool_scale), full(conv_w), full(conv_b), full(w_ax), full(b_ax),
        full(nsp), full(norm_g), full(w_out), full(ln_g), full(ln_b), full(w_route), full(b_route),
    ]
    out_specs = [
        pl.BlockSpec((1, T, D), lambda b, s: (b, s, 0)),
        pl.BlockSpec((1, T, LANES), lambda b, s: (b, s, 0)),
        pl.BlockSpec((8, LANES), const2),
    ]
    out_shape = [
        jax.ShapeDtypeStruct((B, S, D), F32),
        jax.ShapeDtypeStruct((B, S, LANES), F32),
        jax.ShapeDtypeStruct((8, LANES), F32),
    ]
    scratch = [
        pltpu.VMEM((T, in_w), F32),
        pltpu.VMEM((T, mix_w), F32),
        pltpu.VMEM((HALO, pool_w + lru_w), F32),
        pltpu.VMEM((1, lru_w), F32),
        pltpu.VMEM((1, LANES), F32),
    ]
    return pl.pallas_call(
        functools.partial(_mixer_kernel, dims, alpha),
        grid=(B, S // T),
        in_specs=in_specs,
        out_specs=out_specs,
        out_shape=out_shape,
        scratch_shapes=scratch,
        compiler_params=pltpu.CompilerParams(dimension_semantics=("arbitrary", "arbitrary"),
                                             vmem_limit_bytes=VMEM_LIMIT),
        name="mixer_router",
    )(x, kv, w_in, w_pool, pool_scale, conv_w, conv_b, w_ax, b_ax, nsp, norm_g, w_out, ln_g, ln_b,
      w_route, b_route)


def _ffn_kernel(bexp_ref, idx_cur_ref, idx_nxt_ref, x1_hbm, wg_ref, wu_ref, wd_ref, ys_hbm,
                xbuf, obuf, gsem, ssem):
    del bexp_ref
    R = EXPERT_ROWS
    r = pl.program_id(0)
    nb = pl.num_programs(0)
    cur = r % 2

    def gather_copy(tok, i, sl):
        return pltpu.make_async_copy(x1_hbm.at[pl.ds(tok, 1)], xbuf.at[sl, pl.ds(i, 1)], gsem.at[sl])

    def scatter_copy(dst, i, sl):
        return pltpu.make_async_copy(obuf.at[sl, pl.ds(i, 1)], ys_hbm.at[pl.ds(dst, 1)], ssem.at[sl])

    def wait_gathers(sl):
        pltpu.make_async_copy(x1_hbm.at[pl.ds(0, R)], xbuf.at[sl], gsem.at[sl]).wait()

    def wait_scatters(sl):
        pltpu.make_async_copy(obuf.at[sl], ys_hbm.at[pl.ds(0, R)], ssem.at[sl]).wait()

    @pl.when(r == 0)
    def _():
        for i in range(R):
            gather_copy(idx_cur_ref[0, 0, i], i, 0).start()

    wait_gathers(cur)

    @pl.when(r >= 2)
    def _():
        wait_scatters(cur)

    xb = xbuf[cur].astype(BF16)
    for i in range(R):
        gather_copy(idx_nxt_ref[0, 0, i], i, 1 - cur).start()
    g = jnp.dot(xb, wg_ref[0], preferred_element_type=F32)
    u = jnp.dot(xb, wu_ref[0], preferred_element_type=F32)
    hdn = (g * _sigmoid(g)) * u
    obuf[cur] = jnp.dot(hdn.astype(BF16), wd_ref[0], preferred_element_type=F32)

    for i in range(R):
        scatter_copy(idx_cur_ref[0, 0, R + i], i, cur).start()

    @pl.when(r == nb - 1)
    def _():
        wait_gathers(1 - cur)
        wait_scatters(cur)

        @pl.when(nb >= 2)
        def _():
            wait_scatters(1 - cur)


def _expert_ffn(x1, idx, block_exp, w_gate, w_up, w_down, n_rows_out):
    N, D = x1.shape
    nb = idx.shape[0]
    R = EXPERT_ROWS
    FF = w_gate.shape[2]
    grid_spec = pltpu.PrefetchScalarGridSpec(
        num_scalar_prefetch=1,
        grid=(nb,),
        in_specs=[
            pl.BlockSpec((1, 1, 2 * R), lambda r, be: (r, 0, 0), memory_space=pltpu.SMEM),
            pl.BlockSpec((1, 1, 2 * R), lambda r, be: (jnp.minimum(r + 1, nb - 1), 0, 0),
                         memory_space=pltpu.SMEM),
            pl.BlockSpec(memory_space=pl.ANY),
            pl.BlockSpec((1, D, FF), lambda r, be: (be[r], 0, 0)),
            pl.BlockSpec((1, D, FF), lambda r, be: (be[r], 0, 0)),
            pl.BlockSpec((1, FF, D), lambda r, be: (be[r], 0, 0)),
        ],
        out_specs=pl.BlockSpec(memory_space=pl.ANY),
        scratch_shapes=[
            pltpu.VMEM((2, R, D), F32),
            pltpu.VMEM((2, R, D), F32),
            pltpu.SemaphoreType.DMA((2,)),
            pltpu.SemaphoreType.DMA((2,)),
        ],
    )
    return pl.pallas_call(
        _ffn_kernel,
        grid_spec=grid_spec,
        out_shape=jax.ShapeDtypeStruct((n_rows_out, D), F32),
        compiler_params=pltpu.CompilerParams(dimension_semantics=("arbitrary",),
                                             vmem_limit_bytes=VMEM_LIMIT),
        name="expert_ffn",
    )(block_exp, idx, idx, x1, w_gate, w_up, w_down)


def _combine_kernel(alpha, x1_ref, y0_ref, y1_ref, route_ref, g_ref, b_ref, out_ref):
    rt = route_ref[...]
    z = alpha * x1_ref[...] + rt[:, 4:5] * y0_ref[...] + rt[:, 5:6] * y1_ref[...]
    mu = jnp.mean(z, axis=-1, keepdims=True)
    zc = z - mu
    var = jnp.mean(zc * zc, axis=-1, keepdims=True)
    out_ref[...] = zc * lax.rsqrt(var + EPS) * g_ref[...] + b_ref[...]


def _combine(x1, ys, route, ln_g, ln_b, alpha):
    N, D = x1.shape
    T = COMBINE_TILE
    assert N % T == 0
    off = N // T
    return pl.pallas_call(
        functools.partial(_combine_kernel, alpha),
        grid=(N // T,),
        in_specs=[
            pl.BlockSpec((T, D), lambda i: (i, 0)),
            pl.BlockSpec((T, D), lambda i: (i, 0)),
            pl.BlockSpec((T, D), lambda i: (i + off, 0)),
            pl.BlockSpec((T, LANES), lambda i: (i, 0)),
            pl.BlockSpec((1, D), lambda i: (0, 0)),
            pl.BlockSpec((1, D), lambda i: (0, 0)),
        ],
        out_specs=pl.BlockSpec((T, D), lambda i: (i, 0)),
        out_shape=jax.ShapeDtypeStruct((N, D), F32),
        compiler_params=pltpu.CompilerParams(dimension_semantics=("arbitrary",),
                                             vmem_limit_bytes=VMEM_LIMIT),
        name="combine_ln",
    )(x1, ys, ys, route, ln_g, ln_b)


def _route_weights(w_group, w_fine):
    w = jnp.concatenate([w_group, w_fine], axis=1)
    wh = w.astype(BF16)
    wl = (w - wh.astype(F32)).astype(BF16)
    n = w.shape[1]
    out = jnp.zeros((w.shape[0], LANES), BF16)
    out = out.at[:, :n].set(wh)
    return out.at[:, ROUTE_LO:ROUTE_LO + n].set(wl)


def _layer(x, mem, w_in, w_pool, pool_scale, conv_w, conv_b, w_a, b_a, w_x, b_x, lam, w_mem_kv, mix_norm_g,
           w_out, ln1_g, ln1_b, w_group, b_group, w_fine, b_fine, w_gate, w_up, w_down, ln2_g, ln2_b, alpha):
    B, S, D = x.shape
    N = B * S
    R = EXPERT_ROWS
    row = lambda v: v.reshape(1, -1).astype(F32)

    kv = _kv_project(mem, w_mem_kv.astype(BF16))
    w_ax = jnp.concatenate([w_a, w_x], axis=2).astype(BF16)
    b_ax = jnp.concatenate([b_a, b_x]).reshape(1, -1)
    nsp = row(jax.nn.softplus(-lam.astype(F32)))
    b_route = jnp.zeros((1, LANES), F32).at[0, :N_GROUPS + N_EXPERTS].set(
        jnp.concatenate([b_group, b_fine.reshape(-1)]))
    x1, route, counts = _mixer(
        x, kv, w_in.astype(BF16), w_pool.astype(BF16), row(pool_scale), conv_w, row(conv_b), w_ax, b_ax, nsp,
        row(mix_norm_g), w_out.astype(BF16), row(ln1_g), row(ln1_b), _route_weights(w_group, w_fine), b_route,
        alpha)
    x1 = x1.reshape(N, D)
    route = route.reshape(N, LANES)

    e_idx = route[:, 0:2].astype(jnp.int32)
    rank = route[:, 2:4].astype(jnp.int32)
    cnt = counts[0, :N_EXPERTS].astype(jnp.int32)
    padded = (cnt + R - 1) // R * R
    pad_ends = jnp.cumsum(padded)
    pad_starts = pad_ends - padded
    dest = pad_starts[e_idx] + rank
    n_blocks = (2 * N + N_EXPERTS * R) // R
    P = n_blocks * R
    block_start = jnp.arange(n_blocks, dtype=jnp.int32) * R
    block_exp = jnp.minimum(jnp.sum((pad_ends[None, :] <= block_start[:, None]).astype(jnp.int32), axis=1),
                            N_EXPERTS - 1)
    slot = jnp.arange(N, dtype=jnp.int32)[:, None] + jnp.array([0, N], jnp.int32)[None, :]
    p = jnp.arange(P, dtype=jnp.int32)
    spare = 2 * N + ((p // R) % 2) * R + (p % R)
    buf_slot = spare.at[dest.reshape(-1)].set(slot.reshape(-1), unique_indices=True)
    buf_tok = jnp.where(buf_slot < 2 * N, buf_slot % N, 0)
    idx = jnp.concatenate([buf_tok.reshape(n_blocks, 1, R), buf_slot.reshape(n_blocks, 1, R)], axis=2)

    ys = _expert_ffn(x1, idx, block_exp, w_gate.astype(BF16), w_up.astype(BF16), w_down.astype(BF16),
                     2 * N + 2 * R)
    out = _combine(x1, ys, route, row(ln2_g), row(ln2_b), alpha)
    return out.reshape(B, S, D)


def kernel(x, mem, w_in, w_pool, pool_scale, conv_w, conv_b, w_a, b_a, w_x, b_x, lam, w_mem_kv, mix_norm_g, w_out,
           ln1_g, ln1_b, w_group, b_group, w_fine, b_fine, w_gate, w_up, w_down, ln2_g, ln2_b):
    depth = w_in.shape[0]
    alpha = (2 * depth) ** 0.25
    for l in range(depth):
        x = _layer(x, mem, w_in[l], w_pool[l], pool_scale[l], conv_w[l], conv_b[l], w_a[l], b_a[l], w_x[l], b_x[l],
                   lam[l], w_mem_kv[l], mix_norm_g[l], w_out[l], ln1_g[l], ln1_b[l], w_group[l], b_group[l],
                   w_fine[l], b_fine[l], w_gate[l], w_up[l], w_down[l], ln2_g[l], ln2_b[l], alpha)
    return x
```

```python
import functools

import jax
import jax.numpy as jnp
from jax import lax
from jax.experimental import pallas as pl
from jax.experimental.pallas import tpu as pltpu

F32 = jnp.float32
BF16 = jnp.bfloat16

POOL_WINDOWS = (2, 4, 8, 16)
LRU_C = 8.0
CONV_WIDTH = 4
N_GROUPS = 4
EXPERTS_PER_GROUP = 8
N_EXPERTS = N_GROUPS * EXPERTS_PER_GROUP
EPS = 1e-5

LANES = 128
HALO = 16
SEQ_TILE = 256
EXPERT_ROWS = 128
COMBINE_TILE = 512
ROUTE_LO = 64
NEG = -1e30
VMEM_LIMIT = 56 * 1024 * 1024


def _sigmoid(v):
    return 1.0 / (1.0 + jnp.exp(-v))


def _gelu_tanh(v):
    return 0.5 * v * (1.0 + jnp.tanh(0.7978845608028654 * (v + 0.044715 * (v * v * v))))


def _shift_rows(v, k):
    return pltpu.roll(v, k, 0)


def _row_tiles(d):
    tiles, rem = divmod(d, LANES)
    assert rem == 0 and tiles % 8 == 0, "a row must be whole (8, 128) tiles to be one contiguous DMA"
    return tiles


def _vmem_pitch(tiles):
    return tiles if (tiles // 8) % 2 else tiles + 8


def _store_rows(ref, v, pitch):
    n, d = v.shape
    for c in range(d // LANES):
        ref[pl.ds(c, n, stride=pitch), :] = v[:, c * LANES:(c + 1) * LANES]


def _load_rows(ref, n, tiles, pitch):
    return jnp.concatenate([ref[pl.ds(c, n, stride=pitch), :] for c in range(tiles)], axis=1)


def _kv_kernel(mem_ref, w_ref, kv_ref):
    kv_ref[0] = jnp.dot(mem_ref[0].astype(BF16), w_ref[...], preferred_element_type=F32).astype(BF16)


def _kv_project(mem, w_kv):
    B, M, D = mem.shape
    E = w_kv.shape[1]
    return pl.pallas_call(
        _kv_kernel,
        grid=(B,),
        in_specs=[pl.BlockSpec((1, M, D), lambda b: (b, 0, 0)),
                  pl.BlockSpec((D, E), lambda b: (0, 0))],
        out_specs=pl.BlockSpec((1, M, E), lambda b: (b, 0, 0)),
        out_shape=jax.ShapeDtypeStruct((B, M, E), BF16),
        compiler_params=pltpu.CompilerParams(dimension_semantics=("arbitrary",),
                                             vmem_limit_bytes=VMEM_LIMIT),
        name="kv_project",
    )(mem, w_kv)


def _mixer_kernel(dims, alpha,
                  x_ref, kv_ref, w_in_ref, w_pool_ref, pool_scale_ref, conv_w_ref, conv_b_ref,
                  w_ax_ref, b_ax_ref, nsp_ref, norm_g_ref, w_out_ref, ln_g_ref, ln_b_ref,
                  w_route_ref, b_route_ref,
                  x1_ref, xp_ref, route_ref, counts_ref,
                  h_ref, mix_ref, halo_ref, state_ref, cnt_ref):
    T, pool_w, lru_w, mem_w, n_heads, head_dim, gdim, bdim = dims
    b = pl.program_id(0)
    s = pl.program_id(1)
    o_lru = pool_w
    o_gate = o_lru + lru_w
    o_q = o_gate + lru_w

    @pl.when(s == 0)
    def _():
        halo_ref[...] = jnp.zeros_like(halo_ref)
        state_ref[...] = jnp.zeros_like(state_ref)

    @pl.when((b == 0) & (s == 0))
    def _():
        cnt_ref[...] = jnp.zeros_like(cnt_ref)

    x = x_ref[0]
    h_ref[...] = jnp.dot(x.astype(BF16), w_in_ref[...], preferred_element_type=F32)

    row = lax.broadcasted_iota(jnp.int32, (T, 1), 0)
    pos = (s * T + row + 1).astype(F32)

    ext = jnp.concatenate([halo_ref[:, :pool_w], h_ref[:, :pool_w]], axis=0)
    ssq = jnp.zeros((T, 1), F32)
    win = ext
    span = 1
    for g, w in enumerate(POOL_WINDOWS):
        while span < w:
            win = win + _shift_rows(win, span)
            span *= 2
        sw = win[HALO:, :gdim]
        if g + 1 < len(POOL_WINDOWS):
            win = win[:, gdim:]
        u = h_ref[:, g * gdim:(g + 1) * gdim]
        d = sw / jnp.minimum(pos, float(w)) - u
        y = jnp.dot(d.astype(BF16), w_pool_ref[g], preferred_element_type=F32)
        y = y * pool_scale_ref[:, g * gdim:(g + 1) * gdim]
        ssq = ssq + jnp.sum(y * y, axis=-1, keepdims=True)
        mix_ref[:, g * gdim:(g + 1) * gdim] = y
    inv = lax.rsqrt(ssq / pool_w + EPS)
    mix_ref[:, :pool_w] = mix_ref[:, :pool_w] * inv * norm_g_ref[:, :pool_w]

    ssq = jnp.zeros((T, 1), F32)
    for hh in range(lru_w // bdim):
        c0 = hh * bdim
        ext = jnp.concatenate([halo_ref[:, o_lru + c0:o_lru + c0 + bdim],
                               h_ref[:, o_lru + c0:o_lru + c0 + bdim]], axis=0)
        cw = conv_w_ref[:, c0:c0 + bdim]
        uc = ext * cw[CONV_WIDTH - 1:CONV_WIDTH]
        for k in range(1, CONV_WIDTH):
            uc = uc + _shift_rows(ext, k) * cw[CONV_WIDTH - 1 - k:CONV_WIDTH - k]
        uc = uc[HALO:] + conv_b_ref[:, c0:c0 + bdim]
        ga = jnp.dot(uc.astype(BF16), w_ax_ref[hh], preferred_element_type=F32)
        r = _sigmoid(ga[:, :bdim] + b_ax_ref[:, c0:c0 + bdim])
        i = _sigmoid(ga[:, bdim:] + b_ax_ref[:, lru_w + c0:lru_w + c0 + bdim])
        log_a = -LRU_C * r * nsp_ref[:, c0:c0 + bdim]
        a = jnp.exp(log_a)
        mult = jnp.sqrt(jnp.maximum(1.0 - a * a, 0.0))
        bt = mult * i * uc
        k = 1
        while k < T:
            if k < 8:
                keep = row >= k
                a_prev = jnp.where(keep, _shift_rows(a, k), 1.0)
                b_prev = jnp.where(keep, _shift_rows(bt, k), 0.0)
                bt = bt + a * b_prev
                a = a * a_prev
            else:
                bt = jnp.concatenate([bt[:k], bt[k:] + a[k:] * bt[:T - k]], axis=0)
                a = jnp.concatenate([a[:k], a[k:] * a[:T - k]], axis=0)
            k *= 2
        hs = a * state_ref[:, c0:c0 + bdim] + bt
        state_ref[:, c0:c0 + bdim] = hs[T - 1:T]
        y = _gelu_tanh(h_ref[:, o_gate + c0:o_gate + c0 + bdim]) * hs
        ssq = ssq + jnp.sum(y * y, axis=-1, keepdims=True)
        mix_ref[:, o_lru + c0:o_lru + c0 + bdim] = y
    inv = lax.rsqrt(ssq / lru_w + EPS)
    mix_ref[:, o_lru:o_gate] = mix_ref[:, o_lru:o_gate] * inv * norm_g_ref[:, o_lru:o_gate]

    halo_ref[...] = h_ref[T - HALO:, :o_gate]

    ssq = jnp.zeros((T, 1), F32)
    m0 = o_gate
    for hh in range(n_heads):
        c0 = hh * head_dim
        q = h_ref[:, o_q + c0:o_q + c0 + head_dim].astype(BF16)
        kh = kv_ref[0, :, c0:c0 + head_dim]
        vh = kv_ref[0, :, mem_w + c0:mem_w + c0 + head_dim]
        sc = lax.dot_general(q, kh, (((1,), (1,)), ((), ())), preferred_element_type=F32) * (head_dim ** -0.5)
        p = jnp.exp(sc - jnp.max(sc, axis=-1, keepdims=True))
        p = p / jnp.sum(p, axis=-1, keepdims=True)
        y = jnp.dot(p.astype(BF16), vh, preferred_element_type=F32)
        ssq = ssq + jnp.sum(y * y, axis=-1, keepdims=True)
        mix_ref[:, m0 + c0:m0 + c0 + head_dim] = y
    inv = lax.rsqrt(ssq / mem_w + EPS)
    mix_ref[:, m0:] = mix_ref[:, m0:] * inv * norm_g_ref[:, m0:]

    y = jnp.dot(mix_ref[...].astype(BF16), w_out_ref[...], preferred_element_type=F32)
    z = alpha * x + y
    mu = jnp.mean(z, axis=-1, keepdims=True)
    zc = z - mu
    var = jnp.mean(zc * zc, axis=-1, keepdims=True)
    x1 = zc * lax.rsqrt(var + EPS) * ln_g_ref[...] + ln_b_ref[...]
    x1_ref[0] = x1

    xh = x1.astype(BF16)
    xl = (x1 - xh.astype(F32)).astype(BF16)
    _store_rows(xp_ref, x1, x1.shape[1] // LANES)
    lg = jnp.dot(jnp.concatenate([xh, xl], axis=0), w_route_ref[...], preferred_element_type=F32)
    top = lg[:T]
    logits = top + lg[T:] + pltpu.roll(top, LANES - ROUTE_LO, 1) + b_route_ref[...]
    lane_i = lax.broadcasted_iota(jnp.int32, (T, LANES), 1)
    lane = lane_i.astype(F32)
    lane_grp = ((lane_i - N_GROUPS) >> 3).astype(F32)

    is_g = lane_i < N_GROUPS
    gl = jnp.where(is_g, logits, NEG)
    gmax = jnp.max(gl, axis=-1, keepdims=True)
    gsum = jnp.sum(jnp.where(is_g, jnp.exp(gl - gmax), 0.0), axis=-1, keepdims=True)
    g_p = 1.0 / gsum
    g_idx = jnp.min(jnp.where(gl == gmax, lane, float(LANES)), axis=-1, keepdims=True)

    in_grp = (lane_i >= N_GROUPS) & (lane_i < N_GROUPS + N_EXPERTS) & (lane_grp == g_idx)
    fl = jnp.where(in_grp, logits, NEG)
    l1 = jnp.max(fl, axis=-1, keepdims=True)
    i1 = jnp.min(jnp.where(fl == l1, lane, float(LANES)), axis=-1, keepdims=True)
    fl2 = jnp.where(lane == i1, NEG, fl)
    l2 = jnp.max(fl2, axis=-1, keepdims=True)
    i2 = jnp.min(jnp.where(fl2 == l2, lane, float(LANES)), axis=-1, keepdims=True)
    e21 = jnp.exp(l2 - l1)
    w1 = g_p / (1.0 + e21)
    w2 = g_p * e21 / (1.0 + e21)
    e1 = i1 - N_GROUPS
    e2 = i2 - N_GROUPS

    oh1 = (lane == e1).astype(F32)
    oh2 = (lane == e2).astype(F32)
    both = oh1 + oh2
    tri = (lax.broadcasted_iota(jnp.int32, (T, T), 0) > lax.broadcasted_iota(jnp.int32, (T, T), 1))
    pfx = jnp.dot(tri.astype(F32).astype(BF16), both.astype(BF16), preferred_element_type=F32)
    base = pfx + cnt_ref[...]
    r1 = jnp.sum(oh1 * base, axis=-1, keepdims=True)
    r2 = jnp.sum(oh2 * base, axis=-1, keepdims=True)
    cnt_ref[...] = cnt_ref[...] + jnp.sum(both, axis=0, keepdims=True)
    counts_ref[...] = jnp.broadcast_to(cnt_ref[...], counts_ref.shape)

    route = jnp.where(lane_i == 0, e1,
            jnp.where(lane_i == 1, e2,
            jnp.where(lane_i == 2, r1,
            jnp.where(lane_i == 3, r2,
            jnp.where(lane_i == 4, w1,
            jnp.where(lane_i == 5, w2, 0.0))))))
    route_ref[0] = route


def _mixer(x, kv, w_in, w_pool, pool_scale, conv_w, conv_b, w_ax, b_ax, nsp, norm_g, w_out, ln_g, ln_b,
           w_route, b_route, alpha):
    B, S, D = x.shape
    T = SEQ_TILE
    in_w = w_in.shape[1]
    pool_w = pool_scale.shape[1]
    lru_w = conv_b.shape[1]
    mem_w = kv.shape[2] // 2
    gdim = w_pool.shape[1]
    bdim = w_ax.shape[1]
    n_heads = 4
    head_dim = mem_w // n_heads
    mix_w = w_out.shape[0]
    dims = (T, pool_w, lru_w, mem_w, n_heads, head_dim, gdim, bdim)
    assert S % T == 0 and T % 8 == 0

    const2 = lambda b, s: (0, 0)
    const3 = lambda b, s: (0, 0, 0)
    single = pl.Buffered(1)

    def full(a):
        return pl.BlockSpec(a.shape, const2 if a.ndim == 2 else const3, pipeline_mode=single)

    in_specs = [
        pl.BlockSpec((1, T, D), lambda b, s: (b, s, 0)),
        pl.BlockSpec((1,) + kv.shape[1:], lambda b, s: (b, 0, 0)),
        full(w_in), full(w_pool), full(pool_scale), full(conv_w), full(conv_b), full(w_ax), full(b_ax),
        full(nsp), full(norm_g), full(w_out), full(ln_g), full(ln_b), full(w_route), full(b_route),
    ]
    tiles = _row_tiles(D)
    n_seq_tiles = S // T
    out_specs = [
        pl.BlockSpec((1, T, D), lambda b, s: (b, s, 0)),
        pl.BlockSpec((T * tiles, LANES), lambda b, s: (b * n_seq_tiles + s, 0)),
        pl.BlockSpec((1, T, LANES), lambda b, s: (b, s, 0)),
        pl.BlockSpec((8, LANES), const2),
    ]
    out_shape = [
        jax.ShapeDtypeStruct((B, S, D), F32),
        jax.ShapeDtypeStruct((B * S * tiles, LANES), F32),
        jax.ShapeDtypeStruct((B, S, LANES), F32),
        jax.ShapeDtypeStruct((8, LANES), F32),
    ]
    scratch = [
        pltpu.VMEM((T, in_w), F32),
        pltpu.VMEM((T, mix_w), F32),
        pltpu.VMEM((HALO, pool_w + lru_w), F32),
        pltpu.VMEM((1, lru_w), F32),
        pltpu.VMEM((1, LANES), F32),
    ]
    return pl.pallas_call(
        functools.partial(_mixer_kernel, dims, alpha),
        grid=(B, S // T),
        in_specs=in_specs,
        out_specs=out_specs,
        out_shape=out_shape,
        scratch_shapes=scratch,
        compiler_params=pltpu.CompilerParams(dimension_semantics=("arbitrary", "arbitrary"),
                                             vmem_limit_bytes=VMEM_LIMIT),
        name="mixer_router",
    )(x, kv, w_in, w_pool, pool_scale, conv_w, conv_b, w_ax, b_ax, nsp, norm_g, w_out, ln_g, ln_b,
      w_route, b_route)


def _ffn_kernel(tiles, bexp_ref, idx_cur_ref, idx_nxt_ref, xp_hbm, wg_ref, wu_ref, wd_ref, ys_hbm,
                xbuf, obuf, gsem, ssem):
    del bexp_ref
    R = EXPERT_ROWS
    pitch = _vmem_pitch(tiles)
    r = pl.program_id(0)
    nb = pl.num_programs(0)
    cur = r % 2

    def gather_copy(src, i, sl):
        return pltpu.make_async_copy(xp_hbm.at[pl.ds(pl.multiple_of(src, tiles), tiles)],
                                     xbuf.at[sl, pl.ds(i * pitch, tiles)], gsem.at[sl])

    def scatter_copy(dst, i, sl):
        return pltpu.make_async_copy(obuf.at[sl, pl.ds(i * pitch, tiles)],
                                     ys_hbm.at[pl.ds(pl.multiple_of(dst, tiles), tiles)], ssem.at[sl])

    def wait_gathers(sl):
        pltpu.make_async_copy(xp_hbm.at[pl.ds(0, R * tiles)], xbuf.at[sl, pl.ds(0, R * tiles)], gsem.at[sl]).wait()

    def wait_scatters(sl):
        pltpu.make_async_copy(obuf.at[sl, pl.ds(0, R * tiles)], ys_hbm.at[pl.ds(0, R * tiles)], ssem.at[sl]).wait()

    @pl.when(r == 0)
    def _():
        for i in range(R):
            gather_copy(idx_cur_ref[0, 0, i], i, 0).start()

    wait_gathers(cur)

    @pl.when(r >= 2)
    def _():
        wait_scatters(cur)

    xb = _load_rows(xbuf.at[cur], R, tiles, pitch).astype(BF16)
    for i in range(R):
        gather_copy(idx_nxt_ref[0, 0, i], i, 1 - cur).start()
    g = jnp.dot(xb, wg_ref[0], preferred_element_type=F32)
    u = jnp.dot(xb, wu_ref[0], preferred_element_type=F32)
    hdn = (g * _sigmoid(g)) * u
    y = jnp.dot(hdn.astype(BF16), wd_ref[0], preferred_element_type=F32)
    _store_rows(obuf.at[cur], y, pitch)

    for i in range(R):
        scatter_copy(idx_cur_ref[0, 0, R + i], i, cur).start()

    @pl.when(r == nb - 1)
    def _():
        wait_gathers(1 - cur)
        wait_scatters(cur)

        @pl.when(nb >= 2)
        def _():
            wait_scatters(1 - cur)


def _expert_ffn(xp, idx, block_exp, w_gate, w_up, w_down, n_rows_out):
    nb = idx.shape[0]
    R = EXPERT_ROWS
    D, FF = w_gate.shape[1:]
    tiles = _row_tiles(D)
    pitch = _vmem_pitch(tiles)
    grid_spec = pltpu.PrefetchScalarGridSpec(
        num_scalar_prefetch=1,
        grid=(nb,),
        in_specs=[
            pl.BlockSpec((1, 1, 2 * R), lambda r, be: (r, 0, 0), memory_space=pltpu.SMEM),
            pl.BlockSpec((1, 1, 2 * R), lambda r, be: (jnp.minimum(r + 1, nb - 1), 0, 0),
                         memory_space=pltpu.SMEM),
            pl.BlockSpec(memory_space=pl.ANY),
            pl.BlockSpec((1, D, FF), lambda r, be: (be[r], 0, 0)),
            pl.BlockSpec((1, D, FF), lambda r, be: (be[r], 0, 0)),
            pl.BlockSpec((1, FF, D), lambda r, be: (be[r], 0, 0)),
        ],
        out_specs=pl.BlockSpec(memory_space=pl.ANY),
        scratch_shapes=[
            pltpu.VMEM((2, R * pitch, LANES), F32),
            pltpu.VMEM((2, R * pitch, LANES), F32),
            pltpu.SemaphoreType.DMA((2,)),
            pltpu.SemaphoreType.DMA((2,)),
        ],
    )
    return pl.pallas_call(
        functools.partial(_ffn_kernel, tiles),
        grid_spec=grid_spec,
        out_shape=jax.ShapeDtypeStruct((n_rows_out * tiles, LANES), F32),
        compiler_params=pltpu.CompilerParams(dimension_semantics=("arbitrary",),
                                             vmem_limit_bytes=VMEM_LIMIT),
        name="expert_ffn",
    )(block_exp, idx, idx, xp, w_gate, w_up, w_down)


def _combine_kernel(alpha, tiles, x1_ref, y0_ref, y1_ref, route_ref, g_ref, b_ref, out_ref):
    T = x1_ref.shape[0]
    rt = route_ref[...]
    y0 = _load_rows(y0_ref, T, tiles, tiles)
    y1 = _load_rows(y1_ref, T, tiles, tiles)
    z = alpha * x1_ref[...] + rt[:, 4:5] * y0 + rt[:, 5:6] * y1
    mu = jnp.mean(z, axis=-1, keepdims=True)
    zc = z - mu
    var = jnp.mean(zc * zc, axis=-1, keepdims=True)
    out_ref[...] = zc * lax.rsqrt(var + EPS) * g_ref[...] + b_ref[...]


def _combine(x1, ys, route, ln_g, ln_b, alpha):
    N, D = x1.shape
    T = COMBINE_TILE
    assert N % T == 0
    off = N // T
    tiles = _row_tiles(D)
    return pl.pallas_call(
        functools.partial(_combine_kernel, alpha, tiles),
        grid=(N // T,),
        in_specs=[
            pl.BlockSpec((T, D), lambda i: (i, 0)),
            pl.BlockSpec((T * tiles, LANES), lambda i: (i, 0)),
            pl.BlockSpec((T * tiles, LANES), lambda i: (i + off, 0)),
            pl.BlockSpec((T, LANES), lambda i: (i, 0)),
            pl.BlockSpec((1, D), lambda i: (0, 0)),
            pl.BlockSpec((1, D), lambda i: (0, 0)),
        ],
        out_specs=pl.BlockSpec((T, D), lambda i: (i, 0)),
        out_shape=jax.ShapeDtypeStruct((N, D), F32),
        compiler_params=pltpu.CompilerParams(dimension_semantics=("arbitrary",),
                                             vmem_limit_bytes=VMEM_LIMIT),
        name="combine_ln",
    )(x1, ys, ys, route, ln_g, ln_b)


def _route_weights(w_group, w_fine):
    w = jnp.concatenate([w_group, w_fine], axis=1)
    wh = w.astype(BF16)
    wl = (w - wh.astype(F32)).astype(BF16)
    n = w.shape[1]
    out = jnp.zeros((w.shape[0], LANES), BF16)
    out = out.at[:, :n].set(wh)
    return out.at[:, ROUTE_LO:ROUTE_LO + n].set(wl)


def _layer(x, mem, w_in, w_pool, pool_scale, conv_w, conv_b, w_a, b_a, w_x, b_x, lam, w_mem_kv, mix_norm_g,
           w_out, ln1_g, ln1_b, w_group, b_group, w_fine, b_fine, w_gate, w_up, w_down, ln2_g, ln2_b, alpha):
    B, S, D = x.shape
    N = B * S
    R = EXPERT_ROWS
    row = lambda v: v.reshape(1, -1).astype(F32)

    kv = _kv_project(mem, w_mem_kv.astype(BF16))
    w_ax = jnp.concatenate([w_a, w_x], axis=2).astype(BF16)
    b_ax = jnp.concatenate([b_a, b_x]).reshape(1, -1)
    nsp = row(jax.nn.softplus(-lam.astype(F32)))
    b_route = jnp.zeros((1, LANES), F32).at[0, :N_GROUPS + N_EXPERTS].set(
        jnp.concatenate([b_group, b_fine.reshape(-1)]))
    x1, xp, route, counts = _mixer(
        x, kv, w_in.astype(BF16), w_pool.astype(BF16), row(pool_scale), conv_w, row(conv_b), w_ax, b_ax, nsp,
        row(mix_norm_g), w_out.astype(BF16), row(ln1_g), row(ln1_b), _route_weights(w_group, w_fine), b_route,
        alpha)
    x1 = x1.reshape(N, D)
    route = route.reshape(N, LANES)

    e_idx = route[:, 0:2].astype(jnp.int32)
    rank = route[:, 2:4].astype(jnp.int32)
    cnt = counts[0, :N_EXPERTS].astype(jnp.int32)
    padded = (cnt + R - 1) // R * R
    pad_ends = jnp.cumsum(padded)
    pad_starts = pad_ends - padded
    dest = pad_starts[e_idx] + rank
    n_blocks = (2 * N + N_EXPERTS * R) // R
    P = n_blocks * R
    block_start = jnp.arange(n_blocks, dtype=jnp.int32) * R
    block_exp = jnp.minimum(jnp.sum((pad_ends[None, :] <= block_start[:, None]).astype(jnp.int32), axis=1),
                            N_EXPERTS - 1)
    slot = jnp.arange(N, dtype=jnp.int32)[:, None] + jnp.array([0, N], jnp.int32)[None, :]
    p = jnp.arange(P, dtype=jnp.int32)
    spare = 2 * N + ((p // R) % 2) * R + (p % R)
    buf_slot = spare.at[dest.reshape(-1)].set(slot.reshape(-1), unique_indices=True)
    buf_tok = jnp.where(buf_slot < 2 * N, buf_slot % N, 0)
    tiles = _row_tiles(D)
    idx = jnp.concatenate([buf_tok.reshape(n_blocks, 1, R), buf_slot.reshape(n_blocks, 1, R)], axis=2) * tiles

    ys = _expert_ffn(xp, idx, block_exp, w_gate.astype(BF16), w_up.astype(BF16), w_down.astype(BF16),
                     2 * N + 2 * R)
    out = _combine(x1, ys, route, row(ln2_g), row(ln2_b), alpha)
    return out.reshape(B, S, D)


def kernel(x, mem, w_in, w_pool, pool_scale, conv_w, conv_b, w_a, b_a, w_x, b_x, lam, w_mem_kv, mix_norm_g, w_out,
           ln1_g, ln1_b, w_group, b_group, w_fine, b_fine, w_gate, w_up, w_down, ln2_g, ln2_b):
    depth = w_in.shape[0]
    alpha = (2 * depth) ** 0.25
    for l in range(depth):
        x = _layer(x, mem, w_in[l], w_pool[l], pool_scale[l], conv_w[l], conv_b[l], w_a[l], b_a[l], w_x[l], b_x[l],
                   lam[l], w_mem_kv[l], mix_norm_g[l], w_out[l], ln1_g[l], ln1_b[l], w_group[l], b_group[l],
                   w_fine[l], b_fine[l], w_gate[l], w_up[l], w_down[l], ln2_g[l], ln2_b[l], alpha)
    return x
```

```python
import functools

import jax
import jax.numpy as jnp
from jax import lax
from jax.experimental import pallas as pl
from jax.experimental.pallas import tpu as pltpu

F32 = jnp.float32
BF16 = jnp.bfloat16

POOL_WINDOWS = (2, 4, 8, 16)
LRU_C = 8.0
CONV_WIDTH = 4
N_GROUPS = 4
EXPERTS_PER_GROUP = 8
N_EXPERTS = N_GROUPS * EXPERTS_PER_GROUP
EPS = 1e-5

LANES = 128
HALO = 16
SEQ_TILE = 256
EXPERT_ROWS = 128
COMBINE_TILE = 512
ROUTE_LO = 64
NEG = -1e30
VMEM_LIMIT = 56 * 1024 * 1024


def _sigmoid(v):
    return 1.0 / (1.0 + jnp.exp(-v))


def _gelu_tanh(v):
    return 0.5 * v * (1.0 + jnp.tanh(0.7978845608028654 * (v + 0.044715 * (v * v * v))))


def _shift_rows(v, k):
    return pltpu.roll(v, k, 0)


def _row_tiles(d):
    tiles, rem = divmod(d, LANES)
    assert rem == 0 and tiles % 8 == 0, "a row must be whole (8, 128) tiles to be one contiguous DMA"
    return tiles


def _vmem_pitch(tiles):
    return tiles if (tiles // 8) % 2 else tiles + 8


def _store_rows(ref, v, pitch):
    n, d = v.shape
    for c in range(d // LANES):
        ref[pl.ds(c, n, stride=pitch), :] = v[:, c * LANES:(c + 1) * LANES]


def _load_rows(ref, n, tiles, pitch):
    return jnp.concatenate([ref[pl.ds(c, n, stride=pitch), :] for c in range(tiles)], axis=1)


def _kv_kernel(mem_ref, w_ref, kv_ref):
    kv_ref[0] = jnp.dot(mem_ref[0].astype(BF16), w_ref[...], preferred_element_type=F32).astype(BF16)


def _kv_project(mem, w_kv):
    B, M, D = mem.shape
    E = w_kv.shape[1]
    return pl.pallas_call(
        _kv_kernel,
        grid=(B,),
        in_specs=[pl.BlockSpec((1, M, D), lambda b: (b, 0, 0)),
                  pl.BlockSpec((D, E), lambda b: (0, 0))],
        out_specs=pl.BlockSpec((1, M, E), lambda b: (b, 0, 0)),
        out_shape=jax.ShapeDtypeStruct((B, M, E), BF16),
        compiler_params=pltpu.CompilerParams(dimension_semantics=("arbitrary",),
                                             vmem_limit_bytes=VMEM_LIMIT),
        name="kv_project",
    )(mem, w_kv)


def _mixer_kernel(dims, alpha,
                  x_ref, kv_ref, w_in_ref, w_pool_ref, pool_scale_ref, conv_w_ref, conv_b_ref,
                  w_ax_ref, b_ax_ref, nsp_ref, norm_g_ref, w_out_ref, ln_g_ref, ln_b_ref,
                  w_route_ref, b_route_ref,
                  x1_ref, xp_ref, route_ref, counts_ref,
                  h_ref, mix_ref, halo_ref, state_ref, cnt_ref):
    T, pool_w, lru_w, mem_w, n_heads, head_dim, gdim, bdim = dims
    b = pl.program_id(0)
    s = pl.program_id(1)
    o_lru = pool_w
    o_gate = o_lru + lru_w
    o_q = o_gate + lru_w

    @pl.when(s == 0)
    def _():
        halo_ref[...] = jnp.zeros_like(halo_ref)
        state_ref[...] = jnp.zeros_like(state_ref)

    @pl.when((b == 0) & (s == 0))
    def _():
        cnt_ref[...] = jnp.zeros_like(cnt_ref)

    x = x_ref[0]
    h_ref[...] = jnp.dot(x.astype(BF16), w_in_ref[...], preferred_element_type=F32)

    row = lax.broadcasted_iota(jnp.int32, (T, 1), 0)
    pos = (s * T + row + 1).astype(F32)

    ext = jnp.concatenate([halo_ref[:, :pool_w], h_ref[:, :pool_w]], axis=0)
    ssq = jnp.zeros((T, 1), F32)
    win = ext
    span = 1
    for g, w in enumerate(POOL_WINDOWS):
        while span < w:
            win = win + _shift_rows(win, span)
            span *= 2
        sw = win[HALO:, :gdim]
        if g + 1 < len(POOL_WINDOWS):
            win = win[:, gdim:]
        u = h_ref[:, g * gdim:(g + 1) * gdim]
        d = sw / jnp.minimum(pos, float(w)) - u
        y = jnp.dot(d.astype(BF16), w_pool_ref[g], preferred_element_type=F32)
        y = y * pool_scale_ref[:, g * gdim:(g + 1) * gdim]
        ssq = ssq + jnp.sum(y * y, axis=-1, keepdims=True)
        mix_ref[:, g * gdim:(g + 1) * gdim] = y
    inv = lax.rsqrt(ssq / pool_w + EPS)
    mix_ref[:, :pool_w] = mix_ref[:, :pool_w] * inv * norm_g_ref[:, :pool_w]

    ssq = jnp.zeros((T, 1), F32)
    for hh in range(lru_w // bdim):
        c0 = hh * bdim
        ext = jnp.concatenate([halo_ref[:, o_lru + c0:o_lru + c0 + bdim],
                               h_ref[:, o_lru + c0:o_lru + c0 + bdim]], axis=0)
        cw = conv_w_ref[:, c0:c0 + bdim]
        uc = ext * cw[CONV_WIDTH - 1:CONV_WIDTH]
        for k in range(1, CONV_WIDTH):
            uc = uc + _shift_rows(ext, k) * cw[CONV_WIDTH - 1 - k:CONV_WIDTH - k]
        uc = uc[HALO:] + conv_b_ref[:, c0:c0 + bdim]
        ga = jnp.dot(uc.astype(BF16), w_ax_ref[hh], preferred_element_type=F32)
        r = _sigmoid(ga[:, :bdim] + b_ax_ref[:, c0:c0 + bdim])
        i = _sigmoid(ga[:, bdim:] + b_ax_ref[:, lru_w + c0:lru_w + c0 + bdim])
        log_a = -LRU_C * r * nsp_ref[:, c0:c0 + bdim]
        a = jnp.exp(log_a)
        mult = jnp.sqrt(jnp.maximum(1.0 - a * a, 0.0))
        bt = mult * i * uc
        k = 1
        while k < T:
            if k < 8:
                keep = row >= k
                a_prev = jnp.where(keep, _shift_rows(a, k), 1.0)
                b_prev = jnp.where(keep, _shift_rows(bt, k), 0.0)
                bt = bt + a * b_prev
                a = a * a_prev
            else:
                bt = jnp.concatenate([bt[:k], bt[k:] + a[k:] * bt[:T - k]], axis=0)
                a = jnp.concatenate([a[:k], a[k:] * a[:T - k]], axis=0)
            k *= 2
        hs = a * state_ref[:, c0:c0 + bdim] + bt
        state_ref[:, c0:c0 + bdim] = hs[T - 1:T]
        y = _gelu_tanh(h_ref[:, o_gate + c0:o_gate + c0 + bdim]) * hs
        ssq = ssq + jnp.sum(y * y, axis=-1, keepdims=True)
        mix_ref[:, o_lru + c0:o_lru + c0 + bdim] = y
    inv = lax.rsqrt(ssq / lru_w + EPS)
    mix_ref[:, o_lru:o_gate] = mix_ref[:, o_lru:o_gate] * inv * norm_g_ref[:, o_lru:o_gate]

    halo_ref[...] = h_ref[T - HALO:, :o_gate]

    ssq = jnp.zeros((T, 1), F32)
    m0 = o_gate
    for hh in range(n_heads):
        c0 = hh * head_dim
        q = h_ref[:, o_q + c0:o_q + c0 + head_dim].astype(BF16)
        kh = kv_ref[0, :, c0:c0 + head_dim]
        vh = kv_ref[0, :, mem_w + c0:mem_w + c0 + head_dim]
        sc = lax.dot_general(q, kh, (((1,), (1,)), ((), ())), preferred_element_type=F32) * (head_dim ** -0.5)
        p = jnp.exp(sc - jnp.max(sc, axis=-1, keepdims=True))
        p = p / jnp.sum(p, axis=-1, keepdims=True)
        y = jnp.dot(p.astype(BF16), vh, preferred_element_type=F32)
        ssq = ssq + jnp.sum(y * y, axis=-1, keepdims=True)
        mix_ref[:, m0 + c0:m0 + c0 + head_dim] = y
    inv = lax.rsqrt(ssq / mem_w + EPS)
    mix_ref[:, m0:] = mix_ref[:, m0:] * inv * norm_g_ref[:, m0:]

    y = jnp.dot(mix_ref[...].astype(BF16), w_out_ref[...], preferred_element_type=F32)
    z = alpha * x + y
    mu = jnp.mean(z, axis=-1, keepdims=True)
    zc = z - mu
    var = jnp.mean(zc * zc, axis=-1, keepdims=True)
    x1 = zc * lax.rsqrt(var + EPS) * ln_g_ref[...] + ln_b_ref[...]
    x1_ref[0] = x1

    xh = x1.astype(BF16)
    xl = (x1 - xh.astype(F32)).astype(BF16)
    _store_rows(xp_ref, x1, x1.shape[1] // LANES)
    lg = jnp.dot(jnp.concatenate([xh, xl], axis=0), w_route_ref[...], preferred_element_type=F32)
    top = lg[:T]
    logits = top + lg[T:] + pltpu.roll(top, LANES - ROUTE_LO, 1) + b_route_ref[...]
    lane_i = lax.broadcasted_iota(jnp.int32, (T, LANES), 1)
    lane = lane_i.astype(F32)
    lane_grp = ((lane_i - N_GROUPS) >> 3).astype(F32)

    is_g = lane_i < N_GROUPS
    gl = jnp.where(is_g, logits, NEG)
    gmax = jnp.max(gl, axis=-1, keepdims=True)
    gsum = jnp.sum(jnp.where(is_g, jnp.exp(gl - gmax), 0.0), axis=-1, keepdims=True)
    g_p = 1.0 / gsum
    g_idx = jnp.min(jnp.where(gl == gmax, lane, float(LANES)), axis=-1, keepdims=True)

    in_grp = (lane_i >= N_GROUPS) & (lane_i < N_GROUPS + N_EXPERTS) & (lane_grp == g_idx)
    fl = jnp.where(in_grp, logits, NEG)
    l1 = jnp.max(fl, axis=-1, keepdims=True)
    i1 = jnp.min(jnp.where(fl == l1, lane, float(LANES)), axis=-1, keepdims=True)
    fl2 = jnp.where(lane == i1, NEG, fl)
    l2 = jnp.max(fl2, axis=-1, keepdims=True)
    i2 = jnp.min(jnp.where(fl2 == l2, lane, float(LANES)), axis=-1, keepdims=True)
    e21 = jnp.exp(l2 - l1)
    w1 = g_p / (1.0 + e21)
    w2 = g_p * e21 / (1.0 + e21)
    e1 = i1 - N_GROUPS
    e2 = i2 - N_GROUPS

    oh1 = (lane == e1).astype(F32)
    oh2 = (lane == e2).astype(F32)
    both = oh1 + oh2
    tri = (lax.broadcasted_iota(jnp.int32, (T, T), 0) > lax.broadcasted_iota(jnp.int32, (T, T), 1))
    pfx = jnp.dot(tri.astype(F32).astype(BF16), both.astype(BF16), preferred_element_type=F32)
    base = pfx + cnt_ref[...]
    r1 = jnp.sum(oh1 * base, axis=-1, keepdims=True)
    r2 = jnp.sum(oh2 * base, axis=-1, keepdims=True)
    cnt_ref[...] = cnt_ref[...] + jnp.sum(both, axis=0, keepdims=True)
    counts_ref[...] = jnp.broadcast_to(cnt_ref[...], counts_ref.shape)

    route = jnp.where(lane_i == 0, e1,
            jnp.where(lane_i == 1, e2,
            jnp.where(lane_i == 2, r1,
            jnp.where(lane_i == 3, r2,
            jnp.where(lane_i == 4, w1,
            jnp.where(lane_i == 5, w2, 0.0))))))
    route_ref[0] = route


def _mixer(x, kv, w_in, w_pool, pool_scale, conv_w, conv_b, w_ax, b_ax, nsp, norm_g, w_out, ln_g, ln_b,
           w_route, b_route, alpha):
    B, S, D = x.shape
    T = SEQ_TILE
    in_w = w_in.shape[1]
    pool_w = pool_scale.shape[1]
    lru_w = conv_b.shape[1]
    mem_w = kv.shape[2] // 2
    gdim = w_pool.shape[1]
    bdim = w_ax.shape[1]
    n_heads = 4
    head_dim = mem_w // n_heads
    mix_w = w_out.shape[0]
    dims = (T, pool_w, lru_w, mem_w, n_heads, head_dim, gdim, bdim)
    assert S % T == 0 and T % 8 == 0

    const2 = lambda b, s: (0, 0)
    const3 = lambda b, s: (0, 0, 0)
    single = pl.Buffered(1)

    def full(a):
        return pl.BlockSpec(a.shape, const2 if a.ndim == 2 else const3, pipeline_mode=single)

    in_specs = [
        pl.BlockSpec((1, T, D), lambda b, s: (b, s, 0)),
        pl.BlockSpec((1,) + kv.shape[1:], lambda b, s: (b, 0, 0)),
        full(w_in), full(w_pool), full(pool_scale), full(conv_w), full(conv_b), full(w_ax), full(b_ax),
        full(nsp), full(norm_g), full(w_out), full(ln_g), full(ln_b), full(w_route), full(b_route),
    ]
    tiles = _row_tiles(D)
    n_seq_tiles = S // T
    out_specs = [
        pl.BlockSpec((1, T, D), lambda b, s: (b, s, 0)),
        pl.BlockSpec((T * tiles, LANES), lambda b, s: (b * n_seq_tiles + s, 0)),
        pl.BlockSpec((1, T, LANES), lambda b, s: (b, s, 0)),
        pl.BlockSpec((8, LANES), const2),
    ]
    out_shape = [
        jax.ShapeDtypeStruct((B, S, D), F32),
        jax.ShapeDtypeStruct((B * S * tiles, LANES), F32),
        jax.ShapeDtypeStruct((B, S, LANES), F32),
        jax.ShapeDtypeStruct((8, LANES), F32),
    ]
    scratch = [
        pltpu.VMEM((T, in_w), F32),
        pltpu.VMEM((T, mix_w), F32),
        pltpu.VMEM((HALO, pool_w + lru_w), F32),
        pltpu.VMEM((1, lru_w), F32),
        pltpu.VMEM((1, LANES), F32),
    ]
    return pl.pallas_call(
        functools.partial(_mixer_kernel, dims, alpha),
        grid=(B, S // T),
        in_specs=in_specs,
        out_specs=out_specs,
        out_shape=out_shape,
        scratch_shapes=scratch,
        compiler_params=pltpu.CompilerParams(dimension_semantics=("arbitrary", "arbitrary"),
                                             vmem_limit_bytes=VMEM_LIMIT),
        name="mixer_router",
    )(x, kv, w_in, w_pool, pool_scale, conv_w, conv_b, w_ax, b_ax, nsp, norm_g, w_out, ln_g, ln_b,
      w_route, b_route)


def _ffn_kernel(tiles, bexp_ref, idx_cur_ref, idx_nxt_ref, xp_hbm, wg_ref, wu_ref, wd_ref, ys_hbm,
                xbuf, obuf, gsem, ssem):
    del bexp_ref
    R = EXPERT_ROWS
    pitch = _vmem_pitch(tiles)
    r = pl.program_id(0)
    nb = pl.num_programs(0)
    cur = r % 2

    def gather_copy(src, i, sl):
        return pltpu.make_async_copy(xp_hbm.at[pl.ds(pl.multiple_of(src, tiles), tiles)],
                                     xbuf.at[sl, pl.ds(i * pitch, tiles)], gsem.at[sl])

    def scatter_copy(dst, i, sl):
        return pltpu.make_async_copy(obuf.at[sl, pl.ds(i * pitch, tiles)],
                                     ys_hbm.at[pl.ds(pl.multiple_of(dst, tiles), tiles)], ssem.at[sl])

    def wait_gathers(sl):
        pltpu.make_async_copy(xp_hbm.at[pl.ds(0, R * tiles)], xbuf.at[sl, pl.ds(0, R * tiles)], gsem.at[sl]).wait()

    def wait_scatters(sl):
        pltpu.make_async_copy(obuf.at[sl, pl.ds(0, R * tiles)], ys_hbm.at[pl.ds(0, R * tiles)], ssem.at[sl]).wait()

    @pl.when(r == 0)
    def _():
        for i in range(R):
            gather_copy(idx_cur_ref[0, 0, i], i, 0).start()

    for i in range(R):
        gather_copy(idx_nxt_ref[0, 0, i], i, 1 - cur).start()

    wait_gathers(cur)

    @pl.when(r >= 2)
    def _():
        wait_scatters(cur)

    xb = _load_rows(xbuf.at[cur], R, tiles, pitch).astype(BF16)
    g = jnp.dot(xb, wg_ref[0], preferred_element_type=F32)
    u = jnp.dot(xb, wu_ref[0], preferred_element_type=F32)
    hdn = (g * _sigmoid(g)) * u
    y = jnp.dot(hdn.astype(BF16), wd_ref[0], preferred_element_type=F32)
    _store_rows(obuf.at[cur], y, pitch)

    for i in range(R):
        scatter_copy(idx_cur_ref[0, 0, R + i], i, cur).start()

    @pl.when(r == nb - 1)
    def _():
        wait_gathers(1 - cur)
        wait_scatters(cur)

        @pl.when(nb >= 2)
        def _():
            wait_scatters(1 - cur)


def _expert_ffn(xp, idx, block_exp, w_gate, w_up, w_down, n_rows_out):
    nb = idx.shape[0]
    R = EXPERT_ROWS
    D, FF = w_gate.shape[1:]
    tiles = _row_tiles(D)
    pitch = _vmem_pitch(tiles)
    grid_spec = pltpu.PrefetchScalarGridSpec(
        num_scalar_prefetch=1,
        grid=(nb,),
        in_specs=[
            pl.BlockSpec((1, 1, 2 * R), lambda r, be: (r, 0, 0), memory_space=pltpu.SMEM),
            pl.BlockSpec((1, 1, 2 * R), lambda r, be: (jnp.minimum(r + 1, nb - 1), 0, 0),
                         memory_space=pltpu.SMEM),
            pl.BlockSpec(memory_space=pl.ANY),
            pl.BlockSpec((1, D, FF), lambda r, be: (be[r], 0, 0)),
            pl.BlockSpec((1, D, FF), lambda r, be: (be[r], 0, 0)),
            pl.BlockSpec((1, FF, D), lambda r, be: (be[r], 0, 0)),
        ],
        out_specs=pl.BlockSpec(memory_space=pl.ANY),
        scratch_shapes=[
            pltpu.VMEM((2, R * pitch, LANES), F32),
            pltpu.VMEM((2, R * pitch, LANES), F32),
            pltpu.SemaphoreType.DMA((2,)),
            pltpu.SemaphoreType.DMA((2,)),
        ],
    )
    return pl.pallas_call(
        functools.partial(_ffn_kernel, tiles),
        grid_spec=grid_spec,
        out_shape=jax.ShapeDtypeStruct((n_rows_out * tiles, LANES), F32),
        compiler_params=pltpu.CompilerParams(dimension_semantics=("arbitrary",),
                                             vmem_limit_bytes=VMEM_LIMIT),
        name="expert_ffn",
    )(block_exp, idx, idx, xp, w_gate, w_up, w_down)


def _combine_kernel(alpha, tiles, x1_ref, y0_ref, y1_ref, route_ref, g_ref, b_ref, out_ref):
    T = x1_ref.shape[0]
    rt = route_ref[...]
    y0 = _load_rows(y0_ref, T, tiles, tiles)
    y1 = _load_rows(y1_ref, T, tiles, tiles)
    z = alpha * x1_ref[...] + rt[:, 4:5] * y0 + rt[:, 5:6] * y1
    mu = jnp.mean(z, axis=-1, keepdims=True)
    zc = z - mu
    var = jnp.mean(zc * zc, axis=-1, keepdims=True)
    out_ref[...] = zc * lax.rsqrt(var + EPS) * g_ref[...] + b_ref[...]


def _combine(x1, ys, route, ln_g, ln_b, alpha):
    N, D = x1.shape
    T = COMBINE_TILE
    assert N % T == 0
    off = N // T
    tiles = _row_tiles(D)
    return pl.pallas_call(
        functools.partial(_combine_kernel, alpha, tiles),
        grid=(N // T,),
        in_specs=[
            pl.BlockSpec((T, D), lambda i: (i, 0)),
            pl.BlockSpec((T * tiles, LANES), lambda i: (i, 0)),
            pl.BlockSpec((T * tiles, LANES), lambda i: (i + off, 0)),
            pl.BlockSpec((T, LANES), lambda i: (i, 0)),
            pl.BlockSpec((1, D), lambda i: (0, 0)),
            pl.BlockSpec((1, D), lambda i: (0, 0)),
        ],
        out_specs=pl.BlockSpec((T, D), lambda i: (i, 0)),
        out_shape=jax.ShapeDtypeStruct((N, D), F32),
        compiler_params=pltpu.CompilerParams(dimension_semantics=("arbitrary",),
                                             vmem_limit_bytes=VMEM_LIMIT),
        name="combine_ln",
    )(x1, ys, ys, route, ln_g, ln_b)


def _route_weights(w_group, w_fine):
    w = jnp.concatenate([w_group, w_fine], axis=1)
    wh = w.astype(BF16)
    wl = (w - wh.astype(F32)).astype(BF16)
    n = w.shape[1]
    out = jnp.zeros((w.shape[0], LANES), BF16)
    out = out.at[:, :n].set(wh)
    return out.at[:, ROUTE_LO:ROUTE_LO + n].set(wl)


def _layer(x, mem, w_in, w_pool, pool_scale, conv_w, conv_b, w_a, b_a, w_x, b_x, lam, w_mem_kv, mix_norm_g,
           w_out, ln1_g, ln1_b, w_group, b_group, w_fine, b_fine, w_gate, w_up, w_down, ln2_g, ln2_b, alpha):
    B, S, D = x.shape
    N = B * S
    R = EXPERT_ROWS
    row = lambda v: v.reshape(1, -1).astype(F32)

    kv = _kv_project(mem, w_mem_kv.astype(BF16))
    w_ax = jnp.concatenate([w_a, w_x], axis=2).astype(BF16)
    b_ax = jnp.concatenate([b_a, b_x]).reshape(1, -1)
    nsp = row(jax.nn.softplus(-lam.astype(F32)))
    b_route = jnp.zeros((1, LANES), F32).at[0, :N_GROUPS + N_EXPERTS].set(
        jnp.concatenate([b_group, b_fine.reshape(-1)]))
    x1, xp, route, counts = _mixer(
        x, kv, w_in.astype(BF16), w_pool.astype(BF16), row(pool_scale), conv_w, row(conv_b), w_ax, b_ax, nsp,
        row(mix_norm_g), w_out.astype(BF16), row(ln1_g), row(ln1_b), _route_weights(w_group, w_fine), b_route,
        alpha)
    x1 = x1.reshape(N, D)
    route = route.reshape(N, LANES)

    e_idx = route[:, 0:2].astype(jnp.int32)
    rank = route[:, 2:4].astype(jnp.int32)
    cnt = counts[0, :N_EXPERTS].astype(jnp.int32)
    padded = (cnt + R - 1) // R * R
    pad_ends = jnp.cumsum(padded)
    pad_starts = pad_ends - padded
    dest = pad_starts[e_idx] + rank
    n_blocks = (2 * N + N_EXPERTS * R) // R
    P = n_blocks * R
    block_start = jnp.arange(n_blocks, dtype=jnp.int32) * R
    block_exp = jnp.minimum(jnp.sum((pad_ends[None, :] <= block_start[:, None]).astype(jnp.int32), axis=1),
                            N_EXPERTS - 1)
    slot = jnp.arange(N, dtype=jnp.int32)[:, None] + jnp.array([0, N], jnp.int32)[None, :]
    p = jnp.arange(P, dtype=jnp.int32)
    spare = 2 * N + ((p // R) % 2) * R + (p % R)
    buf_slot = spare.at[dest.reshape(-1)].set(slot.reshape(-1), unique_indices=True)
    buf_tok = jnp.where(buf_slot < 2 * N, buf_slot % N, 0)
    tiles = _row_tiles(D)
    idx = jnp.concatenate([buf_tok.reshape(n_blocks, 1, R), buf_slot.reshape(n_blocks, 1, R)], axis=2) * tiles

    ys = _expert_ffn(xp, idx, block_exp, w_gate.astype(BF16), w_up.astype(BF16), w_down.astype(BF16),
                     2 * N + 2 * R)
    out = _combine(x1, ys, route, row(ln2_g), row(ln2_b), alpha)
    return out.reshape(B, S, D)


def kernel(x, mem, w_in, w_pool, pool_scale, conv_w, conv_b, w_a, b_a, w_x, b_x, lam, w_mem_kv, mix_norm_g, w_out,
           ln1_g, ln1_b, w_group, b_group, w_fine, b_fine, w_gate, w_up, w_down, ln2_g, ln2_b):
    depth = w_in.shape[0]
    alpha = (2 * depth) ** 0.25
    for l in range(depth):
        x = _layer(x, mem, w_in[l], w_pool[l], pool_scale[l], conv_w[l], conv_b[l], w_a[l], b_a[l], w_x[l], b_x[l],
                   lam[l], w_mem_kv[l], mix_norm_g[l], w_out[l], ln1_g[l], ln1_b[l], w_group[l], b_group[l],
                   w_fine[l], b_fine[l], w_gate[l], w_up[l], w_down[l], ln2_g[l], ln2_b[l], alpha)
    return x
```

```python
import functools

import jax
import jax.numpy as jnp
from jax import lax
from jax.experimental import pallas as pl
from jax.experimental.pallas import tpu as pltpu

F32 = jnp.float32
BF16 = jnp.bfloat16

POOL_WINDOWS = (2, 4, 8, 16)
LRU_C = 8.0
CONV_WIDTH = 4
N_GROUPS = 4
EXPERTS_PER_GROUP = 8
N_EXPERTS = N_GROUPS * EXPERTS_PER_GROUP
EPS = 1e-5

LANES = 128
HALO = 16
SEQ_TILE = 256
EXPERT_ROWS = 128
COMBINE_TILE = 512
ROUTE_LO = 64
NEG = -1e30
VMEM_LIMIT = 56 * 1024 * 1024


def _sigmoid(v):
    return 1.0 / (1.0 + jnp.exp(-v))


def _gelu_tanh(v):
    return 0.5 * v * (1.0 + jnp.tanh(0.7978845608028654 * (v + 0.044715 * (v * v * v))))


def _shift_rows(v, k):
    return pltpu.roll(v, k, 0)


def _row_tiles(d):
    tiles, rem = divmod(d, LANES)
    assert rem == 0 and tiles % 8 == 0, "a row must be whole (8, 128) tiles to be one contiguous DMA"
    return tiles


def _vmem_pitch(tiles):
    return tiles if (tiles // 8) % 2 else tiles + 8


def _store_rows(ref, v, pitch):
    n, d = v.shape
    for c in range(d // LANES):
        ref[pl.ds(c, n, stride=pitch), :] = v[:, c * LANES:(c + 1) * LANES]


def _load_rows(ref, n, tiles, pitch):
    return jnp.concatenate([ref[pl.ds(c, n, stride=pitch), :] for c in range(tiles)], axis=1)


def _kv_kernel(mem_ref, w_ref, kv_ref):
    kv_ref[0] = jnp.dot(mem_ref[0].astype(BF16), w_ref[...], preferred_element_type=F32).astype(BF16)


def _kv_project(mem, w_kv):
    B, M, D = mem.shape
    E = w_kv.shape[1]
    return pl.pallas_call(
        _kv_kernel,
        grid=(B,),
        in_specs=[pl.BlockSpec((1, M, D), lambda b: (b, 0, 0)),
                  pl.BlockSpec((D, E), lambda b: (0, 0))],
        out_specs=pl.BlockSpec((1, M, E), lambda b: (b, 0, 0)),
        out_shape=jax.ShapeDtypeStruct((B, M, E), BF16),
        compiler_params=pltpu.CompilerParams(dimension_semantics=("arbitrary",),
                                             vmem_limit_bytes=VMEM_LIMIT),
        name="kv_project",
    )(mem, w_kv)


def _mixer_kernel(dims, alpha,
                  x_ref, kv_ref, w_in_ref, w_pool_ref, pool_scale_ref, conv_w_ref, conv_b_ref,
                  w_ax_ref, b_ax_ref, nsp_ref, norm_g_ref, w_out_ref, ln_g_ref, ln_b_ref,
                  w_route_ref, b_route_ref,
                  x1_ref, xp_ref, route_ref, counts_ref,
                  h_ref, mix_ref, halo_ref, state_ref, cnt_ref):
    T, pool_w, lru_w, mem_w, n_heads, head_dim, gdim, bdim = dims
    b = pl.program_id(0)
    s = pl.program_id(1)
    o_lru = pool_w
    o_gate = o_lru + lru_w
    o_q = o_gate + lru_w

    @pl.when(s == 0)
    def _():
        halo_ref[...] = jnp.zeros_like(halo_ref)
        state_ref[...] = jnp.zeros_like(state_ref)

    @pl.when((b == 0) & (s == 0))
    def _():
        cnt_ref[...] = jnp.zeros_like(cnt_ref)

    x = x_ref[0]
    h_ref[...] = jnp.dot(x.astype(BF16), w_in_ref[...], preferred_element_type=F32)

    row = lax.broadcasted_iota(jnp.int32, (T, 1), 0)
    pos = (s * T + row + 1).astype(F32)

    ext = jnp.concatenate([halo_ref[:, :pool_w], h_ref[:, :pool_w]], axis=0)
    ssq = jnp.zeros((T, 1), F32)
    win = ext
    span = 1
    for g, w in enumerate(POOL_WINDOWS):
        while span < w:
            win = win + _shift_rows(win, span)
            span *= 2
        sw = win[HALO:, :gdim]
        if g + 1 < len(POOL_WINDOWS):
            win = win[:, gdim:]
        u = h_ref[:, g * gdim:(g + 1) * gdim]
        d = sw / jnp.minimum(pos, float(w)) - u
        y = jnp.dot(d.astype(BF16), w_pool_ref[g], preferred_element_type=F32)
        y = y * pool_scale_ref[:, g * gdim:(g + 1) * gdim]
        ssq = ssq + jnp.sum(y * y, axis=-1, keepdims=True)
        mix_ref[:, g * gdim:(g + 1) * gdim] = y
    inv = lax.rsqrt(ssq / pool_w + EPS)
    mix_ref[:, :pool_w] = mix_ref[:, :pool_w] * inv * norm_g_ref[:, :pool_w]

    ssq = jnp.zeros((T, 1), F32)
    for hh in range(lru_w // bdim):
        c0 = hh * bdim
        ext = jnp.concatenate([halo_ref[:, o_lru + c0:o_lru + c0 + bdim],
                               h_ref[:, o_lru + c0:o_lru + c0 + bdim]], axis=0)
        cw = conv_w_ref[:, c0:c0 + bdim]
        uc = ext * cw[CONV_WIDTH - 1:CONV_WIDTH]
        for k in range(1, CONV_WIDTH):
            uc = uc + _shift_rows(ext, k) * cw[CONV_WIDTH - 1 - k:CONV_WIDTH - k]
        uc = uc[HALO:] + conv_b_ref[:, c0:c0 + bdim]
        ga = jnp.dot(uc.astype(BF16), w_ax_ref[hh], preferred_element_type=F32)
        r = _sigmoid(ga[:, :bdim] + b_ax_ref[:, c0:c0 + bdim])
        i = _sigmoid(ga[:, bdim:] + b_ax_ref[:, lru_w + c0:lru_w + c0 + bdim])
        log_a = -LRU_C * r * nsp_ref[:, c0:c0 + bdim]
        a = jnp.exp(log_a)
        mult = jnp.sqrt(jnp.maximum(1.0 - a * a, 0.0))
        bt = mult * i * uc
        k = 1
        while k < T:
            if k < 8:
                keep = row >= k
                a_prev = jnp.where(keep, _shift_rows(a, k), 1.0)
                b_prev = jnp.where(keep, _shift_rows(bt, k), 0.0)
                bt = bt + a * b_prev
                a = a * a_prev
            else:
                bt = jnp.concatenate([bt[:k], bt[k:] + a[k:] * bt[:T - k]], axis=0)
                a = jnp.concatenate([a[:k], a[k:] * a[:T - k]], axis=0)
            k *= 2
        hs = a * state_ref[:, c0:c0 + bdim] + bt
        state_ref[:, c0:c0 + bdim] = hs[T - 1:T]
        y = _gelu_tanh(h_ref[:, o_gate + c0:o_gate + c0 + bdim]) * hs
        ssq = ssq + jnp.sum(y * y, axis=-1, keepdims=True)
        mix_ref[:, o_lru + c0:o_lru + c0 + bdim] = y
    inv = lax.rsqrt(ssq / lru_w + EPS)
    mix_ref[:, o_lru:o_gate] = mix_ref[:, o_lru:o_gate] * inv * norm_g_ref[:, o_lru:o_gate]

    halo_ref[...] = h_ref[T - HALO:, :o_gate]

    ssq = jnp.zeros((T, 1), F32)
    m0 = o_gate
    for hh in range(n_heads):
        c0 = hh * head_dim
        q = h_ref[:, o_q + c0:o_q + c0 + head_dim].astype(BF16)
        kh = kv_ref[0, :, c0:c0 + head_dim]
        vh = kv_ref[0, :, mem_w + c0:mem_w + c0 + head_dim]
        sc = lax.dot_general(q, kh, (((1,), (1,)), ((), ())), preferred_element_type=F32) * (head_dim ** -0.5)
        p = jnp.exp(sc - jnp.max(sc, axis=-1, keepdims=True))
        p = p / jnp.sum(p, axis=-1, keepdims=True)
        y = jnp.dot(p.astype(BF16), vh, preferred_element_type=F32)
        ssq = ssq + jnp.sum(y * y, axis=-1, keepdims=True)
        mix_ref[:, m0 + c0:m0 + c0 + head_dim] = y
    inv = lax.rsqrt(ssq / mem_w + EPS)
    mix_ref[:, m0:] = mix_ref[:, m0:] * inv * norm_g_ref[:, m0:]

    y = jnp.dot(mix_ref[...].astype(BF16), w_out_ref[...], preferred_element_type=F32)
    z = alpha * x + y
    mu = jnp.mean(z, axis=-1, keepdims=True)
    zc = z - mu
    var = jnp.mean(zc * zc, axis=-1, keepdims=True)
    x1 = zc * lax.rsqrt(var + EPS) * ln_g_ref[...] + ln_b_ref[...]
    x1_ref[0] = x1

    xh = x1.astype(BF16)
    xl = (x1 - xh.astype(F32)).astype(BF16)
    _store_rows(xp_ref, x1, x1.shape[1] // LANES)
    lg = jnp.dot(jnp.concatenate([xh, xl], axis=0), w_route_ref[...], preferred_element_type=F32)
    top = lg[:T]
    logits = top + lg[T:] + pltpu.roll(top, LANES - ROUTE_LO, 1) + b_route_ref[...]
    lane_i = lax.broadcasted_iota(jnp.int32, (T, LANES), 1)
    lane = lane_i.astype(F32)
    lane_grp = ((lane_i - N_GROUPS) >> 3).astype(F32)

    is_g = lane_i < N_GROUPS
    gl = jnp.where(is_g, logits, NEG)
    gmax = jnp.max(gl, axis=-1, keepdims=True)
    gsum = jnp.sum(jnp.where(is_g, jnp.exp(gl - gmax), 0.0), axis=-1, keepdims=True)
    g_p = 1.0 / gsum
    g_idx = jnp.min(jnp.where(gl == gmax, lane, float(LANES)), axis=-1, keepdims=True)

    in_grp = (lane_i >= N_GROUPS) & (lane_i < N_GROUPS + N_EXPERTS) & (lane_grp == g_idx)
    fl = jnp.where(in_grp, logits, NEG)
    l1 = jnp.max(fl, axis=-1, keepdims=True)
    i1 = jnp.min(jnp.where(fl == l1, lane, float(LANES)), axis=-1, keepdims=True)
    fl2 = jnp.where(lane == i1, NEG, fl)
    l2 = jnp.max(fl2, axis=-1, keepdims=True)
    i2 = jnp.min(jnp.where(fl2 == l2, lane, float(LANES)), axis=-1, keepdims=True)
    e21 = jnp.exp(l2 - l1)
    w1 = g_p / (1.0 + e21)
    w2 = g_p * e21 / (1.0 + e21)
    e1 = i1 - N_GROUPS
    e2 = i2 - N_GROUPS

    oh1 = (lane == e1).astype(F32)
    oh2 = (lane == e2).astype(F32)
    both = oh1 + oh2
    tri = (lax.broadcasted_iota(jnp.int32, (T, T), 0) > lax.broadcasted_iota(jnp.int32, (T, T), 1))
    pfx = jnp.dot(tri.astype(F32).astype(BF16), both.astype(BF16), preferred_element_type=F32)
    base = pfx + cnt_ref[...]
    r1 = jnp.sum(oh1 * base, axis=-1, keepdims=True)
    r2 = jnp.sum(oh2 * base, axis=-1, keepdims=True)
    cnt_ref[...] = cnt_ref[...] + jnp.sum(both, axis=0, keepdims=True)
    counts_ref[...] = jnp.broadcast_to(cnt_ref[...], counts_ref.shape)

    route = jnp.where(lane_i == 0, e1,
            jnp.where(lane_i == 1, e2,
            jnp.where(lane_i == 2, r1,
            jnp.where(lane_i == 3, r2,
            jnp.where(lane_i == 4, w1,
            jnp.where(lane_i == 5, w2, 0.0))))))
    route_ref[0] = route


def _mixer(x, kv, w_in, w_pool, pool_scale, conv_w, conv_b, w_ax, b_ax, nsp, norm_g, w_out, ln_g, ln_b,
           w_route, b_route, alpha):
    B, S, D = x.shape
    T = SEQ_TILE
    in_w = w_in.shape[1]
    pool_w = pool_scale.shape[1]
    lru_w = conv_b.shape[1]
    mem_w = kv.shape[2] // 2
    gdim = w_pool.shape[1]
    bdim = w_ax.shape[1]
    n_heads = 4
    head_dim = mem_w // n_heads
    mix_w = w_out.shape[0]
    dims = (T, pool_w, lru_w, mem_w, n_heads, head_dim, gdim, bdim)
    assert S % T == 0 and T % 8 == 0

    const2 = lambda b, s: (0, 0)
    const3 = lambda b, s: (0, 0, 0)
    single = pl.Buffered(1)

    def full(a):
        return pl.BlockSpec(a.shape, const2 if a.ndim == 2 else const3, pipeline_mode=single)

    in_specs = [
        pl.BlockSpec((1, T, D), lambda b, s: (b, s, 0)),
        pl.BlockSpec((1,) + kv.shape[1:], lambda b, s: (b, 0, 0)),
        full(w_in), full(w_pool), full(pool_scale), full(conv_w), full(conv_b), full(w_ax), full(b_ax),
        full(nsp), full(norm_g), full(w_out), full(ln_g), full(ln_b), full(w_route), full(b_route),
    ]
    tiles = _row_tiles(D)
    n_seq_tiles = S // T
    out_specs = [
        pl.BlockSpec((1, T, D), lambda b, s: (b, s, 0)),
        pl.BlockSpec((T * tiles, LANES), lambda b, s: (b * n_seq_tiles + s, 0)),
        pl.BlockSpec((1, T, LANES), lambda b, s: (b, s, 0)),
        pl.BlockSpec((8, LANES), const2),
    ]
    out_shape = [
        jax.ShapeDtypeStruct((B, S, D), F32),
        jax.ShapeDtypeStruct((B * S * tiles, LANES), F32),
        jax.ShapeDtypeStruct((B, S, LANES), F32),
        jax.ShapeDtypeStruct((8, LANES), F32),
    ]
    scratch = [
        pltpu.VMEM((T, in_w), F32),
        pltpu.VMEM((T, mix_w), F32),
        pltpu.VMEM((HALO, pool_w + lru_w), F32),
        pltpu.VMEM((1, lru_w), F32),
        pltpu.VMEM((1, LANES), F32),
    ]
    return pl.pallas_call(
        functools.partial(_mixer_kernel, dims, alpha),
        grid=(B, S // T),
        in_specs=in_specs,
        out_specs=out_specs,
        out_shape=out_shape,
        scratch_shapes=scratch,
        compiler_params=pltpu.CompilerParams(dimension_semantics=("arbitrary", "arbitrary"),
                                             vmem_limit_bytes=VMEM_LIMIT),
        name="mixer_router",
    )(x, kv, w_in, w_pool, pool_scale, conv_w, conv_b, w_ax, b_ax, nsp, norm_g, w_out, ln_g, ln_b,
      w_route, b_route)


def _ffn_kernel(tiles, bexp_ref, first_ref, nexte_ref, nused_ref,
                idx_cur_ref, idx_nxt_ref, xp_hbm, wg_hbm, wu_hbm, wd_hbm, ys_hbm,
                xbuf, obuf, stage_g, stage_u, stage_d, wg_ref, wu_ref, wd_ref, gsem, ssem, wsem):
    R = EXPERT_ROWS
    pitch = _vmem_pitch(tiles)
    r = pl.program_id(0)
    n_used = nused_ref[0]
    cur = r % 2

    def weight_copies(e):
        return (pltpu.make_async_copy(wg_hbm.at[e], stage_g, wsem.at[0]),
                pltpu.make_async_copy(wu_hbm.at[e], stage_u, wsem.at[1]),
                pltpu.make_async_copy(wd_hbm.at[e], stage_d, wsem.at[2]))

    def gather_copy(src, i, sl):
        return pltpu.make_async_copy(xp_hbm.at[pl.ds(pl.multiple_of(src, tiles), tiles)],
                                     xbuf.at[sl, pl.ds(i * pitch, tiles)], gsem.at[sl])

    def scatter_copy(dst, i, sl):
        return pltpu.make_async_copy(obuf.at[sl, pl.ds(i * pitch, tiles)],
                                     ys_hbm.at[pl.ds(pl.multiple_of(dst, tiles), tiles)], ssem.at[sl])

    def wait_gathers(sl):
        pltpu.make_async_copy(xp_hbm.at[pl.ds(0, R * tiles)], xbuf.at[sl, pl.ds(0, R * tiles)], gsem.at[sl]).wait()

    def wait_scatters(sl):
        pltpu.make_async_copy(obuf.at[sl, pl.ds(0, R * tiles)], ys_hbm.at[pl.ds(0, R * tiles)], ssem.at[sl]).wait()

    @pl.when(r == 0)
    def _():
        for c in weight_copies(bexp_ref[0]):
            c.start()
        for i in range(R):
            gather_copy(idx_cur_ref[0, 0, i], i, 0).start()
        obuf[1] = jnp.zeros(obuf.shape[1:], F32)
        spare = ys_hbm.shape[0] - 2 * R * tiles
        for k in range(2):
            pltpu.make_async_copy(obuf.at[1, pl.ds(0, R * tiles)],
                                  ys_hbm.at[pl.ds(spare + k * R * tiles, R * tiles)], ssem.at[1]).start()
        for k in range(2):
            wait_scatters(1)

    @pl.when(r < n_used)
    def _():
        @pl.when(r + 1 < n_used)
        def _():
            for i in range(R):
                gather_copy(idx_nxt_ref[0, 0, i], i, 1 - cur).start()

        @pl.when(first_ref[r] == 1)
        def _():
            for c in weight_copies(0):
                c.wait()
            n_chunks = 8
            kg = stage_g.shape[0] // n_chunks
            kd = stage_d.shape[0] // n_chunks

            def cast_chunk(j, carry):
                sg = pl.ds(pl.multiple_of(j * kg, kg), kg)
                sd = pl.ds(pl.multiple_of(j * kd, kd), kd)
                wg_ref[sg, :] = stage_g[sg, :].astype(BF16)
                wu_ref[sg, :] = stage_u[sg, :].astype(BF16)
                wd_ref[sd, :] = stage_d[sd, :].astype(BF16)
                return carry
            lax.fori_loop(0, n_chunks, cast_chunk, 0)

            @pl.when(nexte_ref[r] >= 0)
            def _():
                for c in weight_copies(nexte_ref[r]):
                    c.start()

        wait_gathers(cur)

        @pl.when(r >= 2)
        def _():
            wait_scatters(cur)

        xb = _load_rows(xbuf.at[cur], R, tiles, pitch).astype(BF16)
        g = jnp.dot(xb, wg_ref[...], preferred_element_type=F32)
        u = jnp.dot(xb, wu_ref[...], preferred_element_type=F32)
        hdn = (g * _sigmoid(g)) * u
        y = jnp.dot(hdn.astype(BF16), wd_ref[...], preferred_element_type=F32)
        _store_rows(obuf.at[cur], y, pitch)

        for i in range(R):
            scatter_copy(idx_cur_ref[0, 0, R + i], i, cur).start()

        @pl.when(r == n_used - 1)
        def _():
            wait_scatters(cur)

            @pl.when(r >= 1)
            def _():
                wait_scatters(1 - cur)


def _expert_ffn(xp, idx, plan, w_gate, w_up, w_down, n_rows_out):
    nb = idx.shape[0]
    R = EXPERT_ROWS
    D, FF = w_gate.shape[1:]
    tiles = _row_tiles(D)
    pitch = _vmem_pitch(tiles)
    idx_block = (1, 1, 2 * R)
    grid_spec = pltpu.PrefetchScalarGridSpec(
        num_scalar_prefetch=4,
        grid=(nb,),
        in_specs=[
            pl.BlockSpec(idx_block, lambda r, *_: (r, 0, 0), memory_space=pltpu.SMEM),
            pl.BlockSpec(idx_block, lambda r, *_: (jnp.minimum(r + 1, nb - 1), 0, 0), memory_space=pltpu.SMEM),
            pl.BlockSpec(memory_space=pl.ANY),
            pl.BlockSpec(memory_space=pl.ANY),
            pl.BlockSpec(memory_space=pl.ANY),
            pl.BlockSpec(memory_space=pl.ANY),
        ],
        out_specs=pl.BlockSpec(memory_space=pl.ANY),
        scratch_shapes=[
            pltpu.VMEM((2, R * pitch, LANES), F32),
            pltpu.VMEM((2, R * pitch, LANES), F32),
            pltpu.VMEM((D, FF), F32),
            pltpu.VMEM((D, FF), F32),
            pltpu.VMEM((FF, D), F32),
            pltpu.VMEM((D, FF), BF16),
            pltpu.VMEM((D, FF), BF16),
            pltpu.VMEM((FF, D), BF16),
            pltpu.SemaphoreType.DMA((2,)),
            pltpu.SemaphoreType.DMA((2,)),
            pltpu.SemaphoreType.DMA((3,)),
        ],
    )
    return pl.pallas_call(
        functools.partial(_ffn_kernel, tiles),
        grid_spec=grid_spec,
        out_shape=jax.ShapeDtypeStruct((n_rows_out * tiles, LANES), F32),
        compiler_params=pltpu.CompilerParams(dimension_semantics=("arbitrary",),
                                             vmem_limit_bytes=VMEM_LIMIT),
        name="expert_ffn",
    )(*plan, idx, idx, xp, w_gate, w_up, w_down)


def _combine_kernel(alpha, tiles, x1_ref, y0_ref, y1_ref, route_ref, g_ref, b_ref, out_ref):
    T = x1_ref.shape[0]
    rt = route_ref[...]
    y0 = _load_rows(y0_ref, T, tiles, tiles)
    y1 = _load_rows(y1_ref, T, tiles, tiles)
    z = alpha * x1_ref[...] + rt[:, 4:5] * y0 + rt[:, 5:6] * y1
    mu = jnp.mean(z, axis=-1, keepdims=True)
    zc = z - mu
    var = jnp.mean(zc * zc, axis=-1, keepdims=True)
    out_ref[...] = zc * lax.rsqrt(var + EPS) * g_ref[...] + b_ref[...]


def _combine(x1, ys, route, ln_g, ln_b, alpha):
    N, D = x1.shape
    T = COMBINE_TILE
    assert N % T == 0
    off = N // T
    tiles = _row_tiles(D)
    return pl.pallas_call(
        functools.partial(_combine_kernel, alpha, tiles),
        grid=(N // T,),
        in_specs=[
            pl.BlockSpec((T, D), lambda i: (i, 0)),
            pl.BlockSpec((T * tiles, LANES), lambda i: (i, 0)),
            pl.BlockSpec((T * tiles, LANES), lambda i: (i + off, 0)),
            pl.BlockSpec((T, LANES), lambda i: (i, 0)),
            pl.BlockSpec((1, D), lambda i: (0, 0)),
            pl.BlockSpec((1, D), lambda i: (0, 0)),
        ],
        out_specs=pl.BlockSpec((T, D), lambda i: (i, 0)),
        out_shape=jax.ShapeDtypeStruct((N, D), F32),
        compiler_params=pltpu.CompilerParams(dimension_semantics=("arbitrary",),
                                             vmem_limit_bytes=VMEM_LIMIT),
        name="combine_ln",
    )(x1, ys, ys, route, ln_g, ln_b)


def _route_weights(w_group, w_fine):
    w = jnp.concatenate([w_group, w_fine], axis=1)
    wh = w.astype(BF16)
    wl = (w - wh.astype(F32)).astype(BF16)
    n = w.shape[1]
    out = jnp.zeros((w.shape[0], LANES), BF16)
    out = out.at[:, :n].set(wh)
    return out.at[:, ROUTE_LO:ROUTE_LO + n].set(wl)


def _layer(x, mem, w_in, w_pool, pool_scale, conv_w, conv_b, w_a, b_a, w_x, b_x, lam, w_mem_kv, mix_norm_g,
           w_out, ln1_g, ln1_b, w_group, b_group, w_fine, b_fine, w_gate, w_up, w_down, ln2_g, ln2_b, alpha):
    B, S, D = x.shape
    N = B * S
    R = EXPERT_ROWS
    row = lambda v: v.reshape(1, -1).astype(F32)

    kv = _kv_project(mem, w_mem_kv.astype(BF16))
    w_ax = jnp.concatenate([w_a, w_x], axis=2).astype(BF16)
    b_ax = jnp.concatenate([b_a, b_x]).reshape(1, -1)
    nsp = row(jax.nn.softplus(-lam.astype(F32)))
    b_route = jnp.zeros((1, LANES), F32).at[0, :N_GROUPS + N_EXPERTS].set(
        jnp.concatenate([b_group, b_fine.reshape(-1)]))
    x1, xp, route, counts = _mixer(
        x, kv, w_in.astype(BF16), w_pool.astype(BF16), row(pool_scale), conv_w, row(conv_b), w_ax, b_ax, nsp,
        row(mix_norm_g), w_out.astype(BF16), row(ln1_g), row(ln1_b), _route_weights(w_group, w_fine), b_route,
        alpha)
    x1 = x1.reshape(N, D)
    route = route.reshape(N, LANES)

    e_idx = route[:, 0:2].astype(jnp.int32)
    rank = route[:, 2:4].astype(jnp.int32)
    cnt = counts[0, :N_EXPERTS].astype(jnp.int32)
    padded = (cnt + R - 1) // R * R
    pad_ends = jnp.cumsum(padded)
    pad_starts = pad_ends - padded
    dest = pad_starts[e_idx] + rank
    n_blocks = (2 * N + N_EXPERTS * R) // R
    P = n_blocks * R
    block_start = jnp.arange(n_blocks, dtype=jnp.int32) * R
    block_exp = jnp.minimum(jnp.sum((pad_ends[None, :] <= block_start[:, None]).astype(jnp.int32), axis=1),
                            N_EXPERTS - 1)
    slot = jnp.arange(N, dtype=jnp.int32)[:, None] + jnp.array([0, N], jnp.int32)[None, :]
    p = jnp.arange(P, dtype=jnp.int32)
    spare = 2 * N + ((p // R) % 2) * R + (p % R)
    buf_slot = spare.at[dest.reshape(-1)].set(slot.reshape(-1), unique_indices=True)
    buf_tok = jnp.where(buf_slot < 2 * N, buf_slot % N, 0)
    tiles = _row_tiles(D)
    idx = jnp.concatenate([buf_tok.reshape(n_blocks, 1, R), buf_slot.reshape(n_blocks, 1, R)], axis=2) * tiles

    n_used = (pad_ends[-1:] // R).astype(jnp.int32)
    blk = jnp.arange(n_blocks, dtype=jnp.int32)
    first = ((blk < n_used) & ((blk == 0) | (block_exp != jnp.roll(block_exp, 1)))).astype(jnp.int32)
    ex = jnp.arange(N_EXPERTS, dtype=jnp.int32)
    later_active = (padded[None, :] > 0) & (ex[None, :] > ex[:, None])
    next_active = jnp.min(jnp.where(later_active, ex[None, :], N_EXPERTS), axis=1)
    next_active = jnp.where(next_active == N_EXPERTS, -1, next_active).astype(jnp.int32)
    plan = (block_exp.astype(jnp.int32), first, next_active[block_exp], n_used)

    ys = _expert_ffn(xp, idx, plan, w_gate, w_up, w_down, 2 * N + 2 * R)
    out = _combine(x1, ys, route, row(ln2_g), row(ln2_b), alpha)
    return out.reshape(B, S, D)


def kernel(x, mem, w_in, w_pool, pool_scale, conv_w, conv_b, w_a, b_a, w_x, b_x, lam, w_mem_kv, mix_norm_g, w_out,
           ln1_g, ln1_b, w_group, b_group, w_fine, b_fine, w_gate, w_up, w_down, ln2_g, ln2_b):
    depth = w_in.shape[0]
    alpha = (2 * depth) ** 0.25
    for l in range(depth):
        x = _layer(x, mem, w_in[l], w_pool[l], pool_scale[l], conv_w[l], conv_b[l], w_a[l], b_a[l], w_x[l], b_x[l],
                   lam[l], w_mem_kv[l], mix_norm_g[l], w_out[l], ln1_g[l], ln1_b[l], w_group[l], b_group[l],
                   w_fine[l], b_fine[l], w_gate[l], w_up[l], w_down[l], ln2_g[l], ln2_b[l], alpha)
    return x
```

```python
import functools

import jax
import jax.numpy as jnp
from jax import lax
from jax.experimental import pallas as pl
from jax.experimental.pallas import tpu as pltpu

F32 = jnp.float32
BF16 = jnp.bfloat16

POOL_WINDOWS = (2, 4, 8, 16)
LRU_C = 8.0
CONV_WIDTH = 4
N_GROUPS = 4
EXPERTS_PER_GROUP = 8
N_EXPERTS = N_GROUPS * EXPERTS_PER_GROUP
EPS = 1e-5

LANES = 128
HALO = 16
SEQ_TILE = 256
PROJ_CHUNK = 256
EXPERT_ROWS = 128
WEIGHT_DMA_PRIORITY = 1
COMBINE_TILE = 512
ROUTE_LO = 64
NEG = -1e30
VMEM_LIMIT = 56 * 1024 * 1024


def _sigmoid(v):
    return 1.0 / (1.0 + jnp.exp(-v))


def _gelu_tanh(v):
    return 0.5 * v * (1.0 + jnp.tanh(0.7978845608028654 * (v + 0.044715 * (v * v * v))))


def _shift_rows(v, k):
    return pltpu.roll(v, k, 0)


def _row_tiles(d):
    tiles, rem = divmod(d, LANES)
    assert rem == 0 and tiles % 8 == 0, "a row must be whole (8, 128) tiles to be one contiguous DMA"
    return tiles


def _vmem_pitch(tiles):
    return tiles if (tiles // 8) % 2 else tiles + 8


def _store_rows(ref, v, pitch):
    n, d = v.shape
    for c in range(d // LANES):
        ref[pl.ds(c, n, stride=pitch), :] = v[:, c * LANES:(c + 1) * LANES]


def _load_rows(ref, n, tiles, pitch):
    return jnp.concatenate([ref[pl.ds(c, n, stride=pitch), :] for c in range(tiles)], axis=1)


def _kv_kernel(mem_ref, w_ref, kv_ref):
    kv_ref[0] = jnp.dot(mem_ref[0].astype(BF16), w_ref[...], preferred_element_type=F32).astype(BF16)


def _kv_project(mem, w_kv):
    B, M, D = mem.shape
    E = w_kv.shape[1]
    return pl.pallas_call(
        _kv_kernel,
        grid=(B,),
        in_specs=[pl.BlockSpec((1, M, D), lambda b: (b, 0, 0)),
                  pl.BlockSpec((D, E), lambda b: (0, 0))],
        out_specs=pl.BlockSpec((1, M, E), lambda b: (b, 0, 0)),
        out_shape=jax.ShapeDtypeStruct((B, M, E), BF16),
        compiler_params=pltpu.CompilerParams(dimension_semantics=("arbitrary",),
                                             vmem_limit_bytes=VMEM_LIMIT),
        name="kv_project",
    )(mem, w_kv)


def _mixer_kernel(dims, alpha,
                  x_ref, xn_ref, kv_ref, w_in_ref, w_pool_ref, pool_scale_ref, conv_w_ref, conv_b_ref,
                  w_ax_ref, b_ax_ref, nsp_ref, norm_g_ref, w_out_ref, ln_g_ref, ln_b_ref,
                  w_route_ref, b_route_ref,
                  x1_ref, xp_ref, route_ref, counts_ref,
                  h_ref, hn_ref, mix_ref, halo_ref, state_ref, cnt_ref):
    T, pool_w, lru_w, mem_w, n_heads, head_dim, gdim, bdim = dims
    b = pl.program_id(0)
    s = pl.program_id(1)
    o_lru = pool_w
    o_gate = o_lru + lru_w
    o_q = o_gate + lru_w

    @pl.when(s == 0)
    def _():
        halo_ref[...] = jnp.zeros_like(halo_ref)
        state_ref[...] = jnp.zeros_like(state_ref)

    @pl.when((b == 0) & (s == 0))
    def _():
        cnt_ref[...] = jnp.zeros_like(cnt_ref)
        hn_ref[...] = jnp.dot(x_ref[0].astype(BF16), w_in_ref[...], preferred_element_type=F32)

    x = x_ref[0]
    h_ref[...] = hn_ref[...]
    xnb = xn_ref[0].astype(BF16)
    in_w = w_in_ref.shape[1]
    n_proj = in_w // PROJ_CHUNK
    proj_done = [0]

    def project_next(n):
        for j in range(proj_done[0], min(proj_done[0] + n, n_proj)):
            cols = slice(j * PROJ_CHUNK, (j + 1) * PROJ_CHUNK)
            hn_ref[:, cols] = jnp.dot(xnb, w_in_ref[:, cols], preferred_element_type=F32)
        proj_done[0] = min(proj_done[0] + n, n_proj)

    row = lax.broadcasted_iota(jnp.int32, (T, 1), 0)
    pos = (s * T + row + 1).astype(F32)

    ext = jnp.concatenate([halo_ref[:, :pool_w], h_ref[:, :pool_w]], axis=0)
    ssq = jnp.zeros((T, LANES), F32)
    win = ext
    span = 1
    for g, w in enumerate(POOL_WINDOWS):
        while span < w:
            win = win + _shift_rows(win, span)
            span *= 2
        sw = win[HALO:, :gdim]
        if g + 1 < len(POOL_WINDOWS):
            win = win[:, gdim:]
        u = h_ref[:, g * gdim:(g + 1) * gdim]
        d = sw / jnp.minimum(pos, float(w)) - u
        y = jnp.dot(d.astype(BF16), w_pool_ref[g], preferred_element_type=F32)
        y = y * pool_scale_ref[:, g * gdim:(g + 1) * gdim]
        ssq = ssq + y * y
        mix_ref[:, g * gdim:(g + 1) * gdim] = y
    inv = lax.rsqrt(jnp.sum(ssq, axis=-1, keepdims=True) / pool_w + EPS)
    mix_ref[:, :pool_w] = mix_ref[:, :pool_w] * inv * norm_g_ref[:, :pool_w]
    project_next(2)

    ssq = jnp.zeros((T, LANES), F32)
    for hh in range(lru_w // bdim):
        c0 = hh * bdim
        ext = jnp.concatenate([halo_ref[:, o_lru + c0:o_lru + c0 + bdim],
                               h_ref[:, o_lru + c0:o_lru + c0 + bdim]], axis=0)
        cw = conv_w_ref[:, c0:c0 + bdim]
        uc = ext * cw[CONV_WIDTH - 1:CONV_WIDTH]
        for k in range(1, CONV_WIDTH):
            uc = uc + _shift_rows(ext, k) * cw[CONV_WIDTH - 1 - k:CONV_WIDTH - k]
        uc = uc[HALO:] + conv_b_ref[:, c0:c0 + bdim]
        ga = jnp.dot(uc.astype(BF16), w_ax_ref[hh], preferred_element_type=F32)
        r = _sigmoid(ga[:, :bdim] + b_ax_ref[:, c0:c0 + bdim])
        i = _sigmoid(ga[:, bdim:] + b_ax_ref[:, lru_w + c0:lru_w + c0 + bdim])
        log_a = -LRU_C * r * nsp_ref[:, c0:c0 + bdim]
        a = jnp.exp(log_a)
        mult = jnp.sqrt(jnp.maximum(1.0 - a * a, 0.0))
        bt = mult * i * uc
        k = 1
        while k < T:
            if k < 8:
                keep = row >= k
                a_prev = jnp.where(keep, _shift_rows(a, k), 1.0)
                b_prev = jnp.where(keep, _shift_rows(bt, k), 0.0)
                bt = bt + a * b_prev
                a = a * a_prev
            else:
                bt = jnp.concatenate([bt[:k], bt[k:] + a[k:] * bt[:T - k]], axis=0)
                a = jnp.concatenate([a[:k], a[k:] * a[:T - k]], axis=0)
            k *= 2
        hs = a * state_ref[:, c0:c0 + bdim] + bt
        state_ref[:, c0:c0 + bdim] = hs[T - 1:T]
        y = _gelu_tanh(h_ref[:, o_gate + c0:o_gate + c0 + bdim]) * hs
        ssq = ssq + y * y
        mix_ref[:, o_lru + c0:o_lru + c0 + bdim] = y
        project_next(1)
    inv = lax.rsqrt(jnp.sum(ssq, axis=-1, keepdims=True) / lru_w + EPS)
    mix_ref[:, o_lru:o_gate] = mix_ref[:, o_lru:o_gate] * inv * norm_g_ref[:, o_lru:o_gate]

    halo_ref[...] = h_ref[T - HALO:, :o_gate]

    ssq = jnp.zeros((T, LANES), F32)
    m0 = o_gate
    for hh in range(n_heads):
        c0 = hh * head_dim
        q = h_ref[:, o_q + c0:o_q + c0 + head_dim].astype(BF16)
        kh = kv_ref[0, :, c0:c0 + head_dim]
        vh = kv_ref[0, :, mem_w + c0:mem_w + c0 + head_dim]
        sc = lax.dot_general(q, kh, (((1,), (1,)), ((), ())), preferred_element_type=F32) * (head_dim ** -0.5)
        p = jnp.exp(sc - jnp.max(sc, axis=-1, keepdims=True))
        p = p / jnp.sum(p, axis=-1, keepdims=True)
        y = jnp.dot(p.astype(BF16), vh, preferred_element_type=F32)
        ssq = ssq + y * y
        mix_ref[:, m0 + c0:m0 + c0 + head_dim] = y
        project_next(1)
    inv = lax.rsqrt(jnp.sum(ssq, axis=-1, keepdims=True) / mem_w + EPS)
    mix_ref[:, m0:] = mix_ref[:, m0:] * inv * norm_g_ref[:, m0:]

    project_next(n_proj)

    y = jnp.dot(mix_ref[...].astype(BF16), w_out_ref[...], preferred_element_type=F32)
    z = alpha * x + y
    mu = jnp.mean(z, axis=-1, keepdims=True)
    zc = z - mu
    var = jnp.mean(zc * zc, axis=-1, keepdims=True)
    x1 = zc * lax.rsqrt(var + EPS) * ln_g_ref[...] + ln_b_ref[...]
    x1_ref[0] = x1

    xh = x1.astype(BF16)
    xl = (x1 - xh.astype(F32)).astype(BF16)
    _store_rows(xp_ref, x1, x1.shape[1] // LANES)
    lg = jnp.dot(jnp.concatenate([xh, xl], axis=0), w_route_ref[...], preferred_element_type=F32)
    top = lg[:T]
    logits = top + lg[T:] + pltpu.roll(top, LANES - ROUTE_LO, 1) + b_route_ref[...]
    lane_i = lax.broadcasted_iota(jnp.int32, (T, LANES), 1)
    lane = lane_i.astype(F32)
    lane_grp = ((lane_i - N_GROUPS) >> 3).astype(F32)

    is_g = lane_i < N_GROUPS
    gl = jnp.where(is_g, logits, NEG)
    gmax = jnp.max(gl, axis=-1, keepdims=True)
    gsum = jnp.sum(jnp.where(is_g, jnp.exp(gl - gmax), 0.0), axis=-1, keepdims=True)
    g_p = 1.0 / gsum
    g_idx = jnp.min(jnp.where(gl == gmax, lane, float(LANES)), axis=-1, keepdims=True)

    in_grp = (lane_i >= N_GROUPS) & (lane_i < N_GROUPS + N_EXPERTS) & (lane_grp == g_idx)
    fl = jnp.where(in_grp, logits, NEG)
    l1 = jnp.max(fl, axis=-1, keepdims=True)
    i1 = jnp.min(jnp.where(fl == l1, lane, float(LANES)), axis=-1, keepdims=True)
    fl2 = jnp.where(lane == i1, NEG, fl)
    l2 = jnp.max(fl2, axis=-1, keepdims=True)
    i2 = jnp.min(jnp.where(fl2 == l2, lane, float(LANES)), axis=-1, keepdims=True)
    e21 = jnp.exp(l2 - l1)
    w1 = g_p / (1.0 + e21)
    w2 = g_p * e21 / (1.0 + e21)
    e1 = i1 - N_GROUPS
    e2 = i2 - N_GROUPS

    oh1 = (lane == e1).astype(F32)
    oh2 = (lane == e2).astype(F32)
    both = oh1 + oh2
    tri = (lax.broadcasted_iota(jnp.int32, (T, T), 0) > lax.broadcasted_iota(jnp.int32, (T, T), 1))
    pfx = jnp.dot(tri.astype(F32).astype(BF16), both.astype(BF16), preferred_element_type=F32)
    base = pfx + cnt_ref[...]
    r1 = jnp.sum(oh1 * base, axis=-1, keepdims=True)
    r2 = jnp.sum(oh2 * base, axis=-1, keepdims=True)
    cnt_ref[...] = cnt_ref[...] + jnp.sum(both, axis=0, keepdims=True)
    counts_ref[...] = jnp.broadcast_to(cnt_ref[...], counts_ref.shape)

    route = jnp.where(lane_i == 0, e1,
            jnp.where(lane_i == 1, e2,
            jnp.where(lane_i == 2, r1,
            jnp.where(lane_i == 3, r2,
            jnp.where(lane_i == 4, w1,
            jnp.where(lane_i == 5, w2, 0.0))))))
    route_ref[0] = route


def _mixer(x, kv, w_in, w_pool, pool_scale, conv_w, conv_b, w_ax, b_ax, nsp, norm_g, w_out, ln_g, ln_b,
           w_route, b_route, alpha):
    B, S, D = x.shape
    T = SEQ_TILE
    in_w = w_in.shape[1]
    pool_w = pool_scale.shape[1]
    lru_w = conv_b.shape[1]
    mem_w = kv.shape[2] // 2
    gdim = w_pool.shape[1]
    bdim = w_ax.shape[1]
    n_heads = 4
    head_dim = mem_w // n_heads
    mix_w = w_out.shape[0]
    dims = (T, pool_w, lru_w, mem_w, n_heads, head_dim, gdim, bdim)
    assert S % T == 0 and T % 8 == 0
    assert gdim == bdim == head_dim == LANES and in_w % PROJ_CHUNK == 0

    const2 = lambda b, s: (0, 0)
    const3 = lambda b, s: (0, 0, 0)
    single = pl.Buffered(1)

    def full(a):
        return pl.BlockSpec(a.shape, const2 if a.ndim == 2 else const3, pipeline_mode=single)

    n_s = S // T

    def next_tile(b, s):
        g = jnp.minimum(b * n_s + s + 1, B * n_s - 1)
        return (g // n_s, g % n_s, 0)

    in_specs = [
        pl.BlockSpec((1, T, D), lambda b, s: (b, s, 0)),
        pl.BlockSpec((1, T, D), next_tile),
        pl.BlockSpec((1,) + kv.shape[1:], lambda b, s: (b, 0, 0)),
        full(w_in), full(w_pool), full(pool_scale), full(conv_w), full(conv_b), full(w_ax), full(b_ax),
        full(nsp), full(norm_g), full(w_out), full(ln_g), full(ln_b), full(w_route), full(b_route),
    ]
    tiles = _row_tiles(D)
    n_seq_tiles = S // T
    out_specs = [
        pl.BlockSpec((1, T, D), lambda b, s: (b, s, 0)),
        pl.BlockSpec((T * tiles, LANES), lambda b, s: (b * n_seq_tiles + s, 0)),
        pl.BlockSpec((1, T, LANES), lambda b, s: (b, s, 0)),
        pl.BlockSpec((8, LANES), const2),
    ]
    out_shape = [
        jax.ShapeDtypeStruct((B, S, D), F32),
        jax.ShapeDtypeStruct((B * S * tiles, LANES), F32),
        jax.ShapeDtypeStruct((B, S, LANES), F32),
        jax.ShapeDtypeStruct((8, LANES), F32),
    ]
    scratch = [
        pltpu.VMEM((T, in_w), F32),
        pltpu.VMEM((T, in_w), F32),
        pltpu.VMEM((T, mix_w), F32),
        pltpu.VMEM((HALO, pool_w + lru_w), F32),
        pltpu.VMEM((1, lru_w), F32),
        pltpu.VMEM((1, LANES), F32),
    ]
    return pl.pallas_call(
        functools.partial(_mixer_kernel, dims, alpha),
        grid=(B, S // T),
        in_specs=in_specs,
        out_specs=out_specs,
        out_shape=out_shape,
        scratch_shapes=scratch,
        compiler_params=pltpu.CompilerParams(dimension_semantics=("arbitrary", "arbitrary"),
                                             vmem_limit_bytes=VMEM_LIMIT),
        name="mixer_router",
    )(x, x, kv, w_in, w_pool, pool_scale, conv_w, conv_b, w_ax, b_ax, nsp, norm_g, w_out, ln_g, ln_b,
      w_route, b_route)


def _ffn_kernel(tiles, bexp_ref, first_ref, nexte_ref, nused_ref,
                idx_cur_ref, idx_nxt_ref, xp_hbm, wg_hbm, wu_hbm, wd_hbm, ys_hbm,
                xbuf, obuf, stage_g, stage_u, stage_d, wg_ref, wu_ref, wd_ref, gsem, ssem, wsem):
    R = EXPERT_ROWS
    pitch = _vmem_pitch(tiles)
    r = pl.program_id(0)
    n_used = nused_ref[0]
    cur = r % 2

    def weight_copies(e):
        return (pltpu.make_async_copy(wg_hbm.at[e], stage_g, wsem.at[0]),
                pltpu.make_async_copy(wu_hbm.at[e], stage_u, wsem.at[1]),
                pltpu.make_async_copy(wd_hbm.at[e], stage_d, wsem.at[2]))

    def gather_copy(src, i, sl):
        return pltpu.make_async_copy(xp_hbm.at[pl.ds(pl.multiple_of(src, tiles), tiles)],
                                     xbuf.at[sl, pl.ds(i * pitch, tiles)], gsem.at[sl])

    def scatter_copy(dst, i, sl):
        return pltpu.make_async_copy(obuf.at[sl, pl.ds(i * pitch, tiles)],
                                     ys_hbm.at[pl.ds(pl.multiple_of(dst, tiles), tiles)], ssem.at[sl])

    def wait_gathers(sl):
        pltpu.make_async_copy(xp_hbm.at[pl.ds(0, R * tiles)], xbuf.at[sl, pl.ds(0, R * tiles)], gsem.at[sl]).wait()

    def wait_scatters(sl):
        pltpu.make_async_copy(obuf.at[sl, pl.ds(0, R * tiles)], ys_hbm.at[pl.ds(0, R * tiles)], ssem.at[sl]).wait()

    @pl.when(r == 0)
    def _():
        for c in weight_copies(bexp_ref[0]):
            c.start(priority=WEIGHT_DMA_PRIORITY)
        for i in range(R):
            gather_copy(idx_cur_ref[0, 0, i], i, 0).start()
        obuf[1] = jnp.zeros(obuf.shape[1:], F32)
        spare = ys_hbm.shape[0] - 2 * R * tiles
        for k in range(2):
            pltpu.make_async_copy(obuf.at[1, pl.ds(0, R * tiles)],
                                  ys_hbm.at[pl.ds(spare + k * R * tiles, R * tiles)], ssem.at[1]).start()
        for k in range(2):
            wait_scatters(1)

    @pl.when(r < n_used)
    def _():
        @pl.when(r + 1 < n_used)
        def _():
            for i in range(R):
                gather_copy(idx_nxt_ref[0, 0, i], i, 1 - cur).start()

        @pl.when(first_ref[r] == 1)
        def _():
            for c in weight_copies(0):
                c.wait()
            n_chunks = 8
            kg = stage_g.shape[0] // n_chunks
            kd = stage_d.shape[0] // n_chunks

            def cast_chunk(j, carry):
                sg = pl.ds(pl.multiple_of(j * kg, kg), kg)
                sd = pl.ds(pl.multiple_of(j * kd, kd), kd)
                wg_ref[sg, :] = stage_g[sg, :].astype(BF16)
                wu_ref[sg, :] = stage_u[sg, :].astype(BF16)
                wd_ref[sd, :] = stage_d[sd, :].astype(BF16)
                return carry
            lax.fori_loop(0, n_chunks, cast_chunk, 0)

            @pl.when(nexte_ref[r] >= 0)
            def _():
                for c in weight_copies(nexte_ref[r]):
                    c.start(priority=WEIGHT_DMA_PRIORITY)

        wait_gathers(cur)

        @pl.when(r >= 2)
        def _():
            wait_scatters(cur)

        xb = _load_rows(xbuf.at[cur], R, tiles, pitch).astype(BF16)
        g = jnp.dot(xb, wg_ref[...], preferred_element_type=F32)
        u = jnp.dot(xb, wu_ref[...], preferred_element_type=F32)
        hdn = (g * _sigmoid(g)) * u
        y = jnp.dot(hdn.astype(BF16), wd_ref[...], preferred_element_type=F32)
        _store_rows(obuf.at[cur], y, pitch)

        for i in range(R):
            scatter_copy(idx_cur_ref[0, 0, R + i], i, cur).start()

        @pl.when(r == n_used - 1)
        def _():
            wait_scatters(cur)

            @pl.when(r >= 1)
            def _():
                wait_scatters(1 - cur)


def _expert_ffn(xp, idx, plan, w_gate, w_up, w_down, n_rows_out):
    nb = idx.shape[0]
    R = EXPERT_ROWS
    D, FF = w_gate.shape[1:]
    tiles = _row_tiles(D)
    pitch = _vmem_pitch(tiles)
    idx_block = (1, 1, 2 * R)
    grid_spec = pltpu.PrefetchScalarGridSpec(
        num_scalar_prefetch=4,
        grid=(nb,),
        in_specs=[
            pl.BlockSpec(idx_block, lambda r, *_: (r, 0, 0), memory_space=pltpu.SMEM),
            pl.BlockSpec(idx_block, lambda r, *_: (jnp.minimum(r + 1, nb - 1), 0, 0), memory_space=pltpu.SMEM),
            pl.BlockSpec(memory_space=pl.ANY),
            pl.BlockSpec(memory_space=pl.ANY),
            pl.BlockSpec(memory_space=pl.ANY),
            pl.BlockSpec(memory_space=pl.ANY),
        ],
        out_specs=pl.BlockSpec(memory_space=pl.ANY),
        scratch_shapes=[
            pltpu.VMEM((2, R * pitch, LANES), F32),
            pltpu.VMEM((2, R * pitch, LANES), F32),
            pltpu.VMEM((D, FF), F32),
            pltpu.VMEM((D, FF), F32),
            pltpu.VMEM((FF, D), F32),
            pltpu.VMEM((D, FF), BF16),
            pltpu.VMEM((D, FF), BF16),
            pltpu.VMEM((FF, D), BF16),
            pltpu.SemaphoreType.DMA((2,)),
            pltpu.SemaphoreType.DMA((2,)),
            pltpu.SemaphoreType.DMA((3,)),
        ],
    )
    return pl.pallas_call(
        functools.partial(_ffn_kernel, tiles),
        grid_spec=grid_spec,
        out_shape=jax.ShapeDtypeStruct((n_rows_out * tiles, LANES), F32),
        compiler_params=pltpu.CompilerParams(dimension_semantics=("arbitrary",),
                                             vmem_limit_bytes=VMEM_LIMIT),
        name="expert_ffn",
    )(*plan, idx, idx, xp, w_gate, w_up, w_down)


def _combine_kernel(alpha, tiles, x1_ref, y0_ref, y1_ref, route_ref, g_ref, b_ref, out_ref):
    T = x1_ref.shape[0]
    rt = route_ref[...]
    y0 = _load_rows(y0_ref, T, tiles, tiles)
    y1 = _load_rows(y1_ref, T, tiles, tiles)
    z = alpha * x1_ref[...] + rt[:, 4:5] * y0 + rt[:, 5:6] * y1
    mu = jnp.mean(z, axis=-1, keepdims=True)
    zc = z - mu
    var = jnp.mean(zc * zc, axis=-1, keepdims=True)
    out_ref[...] = zc * lax.rsqrt(var + EPS) * g_ref[...] + b_ref[...]


def _combine(x1, ys, route, ln_g, ln_b, alpha):
    N, D = x1.shape
    T = COMBINE_TILE
    assert N % T == 0
    off = N // T
    tiles = _row_tiles(D)
    return pl.pallas_call(
        functools.partial(_combine_kernel, alpha, tiles),
        grid=(N // T,),
        in_specs=[
            pl.BlockSpec((T, D), lambda i: (i, 0)),
            pl.BlockSpec((T * tiles, LANES), lambda i: (i, 0)),
            pl.BlockSpec((T * tiles, LANES), lambda i: (i + off, 0)),
            pl.BlockSpec((T, LANES), lambda i: (i, 0)),
            pl.BlockSpec((1, D), lambda i: (0, 0)),
            pl.BlockSpec((1, D), lambda i: (0, 0)),
        ],
        out_specs=pl.BlockSpec((T, D), lambda i: (i, 0)),
        out_shape=jax.ShapeDtypeStruct((N, D), F32),
        compiler_params=pltpu.CompilerParams(dimension_semantics=("arbitrary",),
                                             vmem_limit_bytes=VMEM_LIMIT),
        name="combine_ln",
    )(x1, ys, ys, route, ln_g, ln_b)


def _route_weights(w_group, w_fine):
    w = jnp.concatenate([w_group, w_fine], axis=1)
    wh = w.astype(BF16)
    wl = (w - wh.astype(F32)).astype(BF16)
    n = w.shape[1]
    out = jnp.zeros((w.shape[0], LANES), BF16)
    out = out.at[:, :n].set(wh)
    return out.at[:, ROUTE_LO:ROUTE_LO + n].set(wl)


def _layer(x, mem, w_in, w_pool, pool_scale, conv_w, conv_b, w_a, b_a, w_x, b_x, lam, w_mem_kv, mix_norm_g,
           w_out, ln1_g, ln1_b, w_group, b_group, w_fine, b_fine, w_gate, w_up, w_down, ln2_g, ln2_b, alpha):
    B, S, D = x.shape
    N = B * S
    R = EXPERT_ROWS
    row = lambda v: v.reshape(1, -1).astype(F32)

    kv = _kv_project(mem, w_mem_kv.astype(BF16))
    w_ax = jnp.concatenate([w_a, w_x], axis=2).astype(BF16)
    b_ax = jnp.concatenate([b_a, b_x]).reshape(1, -1)
    nsp = row(jax.nn.softplus(-lam.astype(F32)))
    b_route = jnp.zeros((1, LANES), F32).at[0, :N_GROUPS + N_EXPERTS].set(
        jnp.concatenate([b_group, b_fine.reshape(-1)]))
    x1, xp, route, counts = _mixer(
        x, kv, w_in.astype(BF16), w_pool.astype(BF16), row(pool_scale), conv_w, row(conv_b), w_ax, b_ax, nsp,
        row(mix_norm_g), w_out.astype(BF16), row(ln1_g), row(ln1_b), _route_weights(w_group, w_fine), b_route,
        alpha)
    x1 = x1.reshape(N, D)
    route = route.reshape(N, LANES)

    e_idx = route[:, 0:2].astype(jnp.int32)
    rank = route[:, 2:4].astype(jnp.int32)
    cnt = counts[0, :N_EXPERTS].astype(jnp.int32)
    padded = (cnt + R - 1) // R * R
    pad_ends = jnp.cumsum(padded)
    pad_starts = pad_ends - padded
    dest = pad_starts[e_idx] + rank
    n_blocks = (2 * N + N_EXPERTS * R) // R
    P = n_blocks * R
    block_start = jnp.arange(n_blocks, dtype=jnp.int32) * R
    block_exp = jnp.minimum(jnp.sum((pad_ends[None, :] <= block_start[:, None]).astype(jnp.int32), axis=1),
                            N_EXPERTS - 1)
    slot = jnp.arange(N, dtype=jnp.int32)[:, None] + jnp.array([0, N], jnp.int32)[None, :]
    p = jnp.arange(P, dtype=jnp.int32)
    spare = 2 * N + ((p // R) % 2) * R + (p % R)
    buf_slot = spare.at[dest.reshape(-1)].set(slot.reshape(-1), unique_indices=True)
    buf_tok = jnp.where(buf_slot < 2 * N, buf_slot % N, 0)
    tiles = _row_tiles(D)
    idx = jnp.concatenate([buf_tok.reshape(n_blocks, 1, R), buf_slot.reshape(n_blocks, 1, R)], axis=2) * tiles

    n_used = (pad_ends[-1:] // R).astype(jnp.int32)
    blk = jnp.arange(n_blocks, dtype=jnp.int32)
    first = ((blk < n_used) & ((blk == 0) | (block_exp != jnp.roll(block_exp, 1)))).astype(jnp.int32)
    ex = jnp.arange(N_EXPERTS, dtype=jnp.int32)
    later_active = (padded[None, :] > 0) & (ex[None, :] > ex[:, None])
    next_active = jnp.min(jnp.where(later_active, ex[None, :], N_EXPERTS), axis=1)
    next_active = jnp.where(next_active == N_EXPERTS, -1, next_active).astype(jnp.int32)
    plan = (block_exp.astype(jnp.int32), first, next_active[block_exp], n_used)

    ys = _expert_ffn(xp, idx, plan, w_gate, w_up, w_down, 2 * N + 2 * R)
    out = _combine(x1, ys, route, row(ln2_g), row(ln2_b), alpha)
    return out.reshape(B, S, D)


def kernel(x, mem, w_in, w_pool, pool_scale, conv_w, conv_b, w_a, b_a, w_x, b_x, lam, w_mem_kv, mix_norm_g, w_out,
           ln1_g, ln1_b, w_group, b_group, w_fine, b_fine, w_gate, w_up, w_down, ln2_g, ln2_b):
    depth = w_in.shape[0]
    alpha = (2 * depth) ** 0.25
    for l in range(depth):
        x = _layer(x, mem, w_in[l], w_pool[l], pool_scale[l], conv_w[l], conv_b[l], w_a[l], b_a[l], w_x[l], b_x[l],
                   lam[l], w_mem_kv[l], mix_norm_g[l], w_out[l], ln1_g[l], ln1_b[l], w_group[l], b_group[l],
                   w_fine[l], b_fine[l], w_gate[l], w_up[l], w_down[l], ln2_g[l], ln2_b[l], alpha)
    return x
```

```python
import functools

import jax
import jax.numpy as jnp
from jax import lax
from jax.experimental import pallas as pl
from jax.experimental.pallas import tpu as pltpu

F32 = jnp.float32
BF16 = jnp.bfloat16

POOL_WINDOWS = (2, 4, 8, 16)
LRU_C = 8.0
CONV_WIDTH = 4
N_GROUPS = 4
EXPERTS_PER_GROUP = 8
N_EXPERTS = N_GROUPS * EXPERTS_PER_GROUP
EPS = 1e-5

LANES = 128
HALO = 16
SEQ_TILE = 256
PROJ_CHUNK = 256
EXPERT_ROWS = 128
WEIGHT_DMA_PRIORITY = 1
COMBINE_TILE = 512
ROUTE_LO = 64
NEG = -1e30
VMEM_LIMIT = 56 * 1024 * 1024


def _sigmoid(v):
    return 1.0 / (1.0 + jnp.exp(-v))


def _gelu_tanh(v):
    return 0.5 * v * (1.0 + jnp.tanh(0.7978845608028654 * (v + 0.044715 * (v * v * v))))


def _shift_rows(v, k):
    return pltpu.roll(v, k, 0)


def _linear_scan(a, b, h0):
    n = a.shape[0]
    groups = n // 8
    a3 = a.reshape(groups, 8, LANES)
    b3 = b.reshape(groups, 8, LANES)
    sub = lax.broadcasted_iota(jnp.int32, (groups, 8, LANES), 1)
    for k in (1, 2, 4):
        keep = sub >= k
        b_prev = jnp.where(keep, pltpu.roll(b3, k, 1), 0.0)
        a_prev = jnp.where(keep, pltpu.roll(a3, k, 1), 1.0)
        b3 = b3 + a3 * b_prev
        a3 = a3 * a_prev
    carry = h0
    out = []
    for g in range(groups):
        hg = a3[g] * carry + b3[g]
        out.append(hg)
        carry = hg[7:8]
    return jnp.concatenate(out, axis=0)


def _row_tiles(d):
    tiles, rem = divmod(d, LANES)
    assert rem == 0 and tiles % 8 == 0, "a row must be whole (8, 128) tiles to be one contiguous DMA"
    return tiles


def _vmem_pitch(tiles):
    return tiles if (tiles // 8) % 2 else tiles + 8


def _store_rows(ref, v, pitch):
    n, d = v.shape
    for c in range(d // LANES):
        ref[pl.ds(c, n, stride=pitch), :] = v[:, c * LANES:(c + 1) * LANES]


def _load_rows(ref, n, tiles, pitch):
    return jnp.concatenate([ref[pl.ds(c, n, stride=pitch), :] for c in range(tiles)], axis=1)


def _kv_kernel(mem_ref, w_ref, kv_ref):
    kv_ref[0] = jnp.dot(mem_ref[0].astype(BF16), w_ref[...], preferred_element_type=F32).astype(BF16)


def _kv_project(mem, w_kv):
    B, M, D = mem.shape
    E = w_kv.shape[1]
    return pl.pallas_call(
        _kv_kernel,
        grid=(B,),
        in_specs=[pl.BlockSpec((1, M, D), lambda b: (b, 0, 0)),
                  pl.BlockSpec((D, E), lambda b: (0, 0))],
        out_specs=pl.BlockSpec((1, M, E), lambda b: (b, 0, 0)),
        out_shape=jax.ShapeDtypeStruct((B, M, E), BF16),
        compiler_params=pltpu.CompilerParams(dimension_semantics=("arbitrary",),
                                             vmem_limit_bytes=VMEM_LIMIT),
        name="kv_project",
    )(mem, w_kv)


def _mixer_kernel(dims, alpha, n_s,
                  xr_ref, xn_ref, kv_ref, w_in_ref, w_pool_ref, pool_scale_ref, conv_w_ref, conv_b_ref,
                  w_ax_ref, b_ax_ref, nsp_ref, norm_g_ref, w_out_ref, ln_g_ref, ln_b_ref,
                  w_route_ref, b_route_ref,
                  x1_ref, xp_ref, route_ref, route_t_ref, counts_ref,
                  h_ref, hn_ref, mix_ref, mixb_ref, mixn_ref, y_ref, halo_ref, state_ref, cnt_ref):
    T, pool_w, lru_w, mem_w, n_heads, head_dim, gdim, bdim = dims
    g_step = pl.program_id(0)
    s = lax.rem(g_step, n_s)
    o_lru = pool_w
    o_gate = o_lru + lru_w
    o_q = o_gate + lru_w

    @pl.when(s == 0)
    def _():
        halo_ref[...] = jnp.zeros_like(halo_ref)
        state_ref[...] = jnp.zeros_like(state_ref)

    @pl.when(g_step == 0)
    def _():
        cnt_ref[...] = jnp.zeros_like(cnt_ref)
        hn_ref[...] = jnp.dot(xr_ref[0].astype(BF16), w_in_ref[...], preferred_element_type=F32)
        mixn_ref[...] = jnp.zeros_like(mixn_ref)

    h_ref[...] = hn_ref[...]
    mixb_ref[...] = mixn_ref[...]
    xnb = xn_ref[0].astype(BF16)
    n_proj = w_in_ref.shape[1] // PROJ_CHUNK
    n_out = w_out_ref.shape[1] // PROJ_CHUNK
    done = {"proj": 0, "out": 0}

    def fill_mxu(n_in, n_o):
        for j in range(done["proj"], min(done["proj"] + n_in, n_proj)):
            cols = slice(j * PROJ_CHUNK, (j + 1) * PROJ_CHUNK)
            hn_ref[:, cols] = jnp.dot(xnb, w_in_ref[:, cols], preferred_element_type=F32)
        for j in range(done["out"], min(done["out"] + n_o, n_out)):
            cols = slice(j * PROJ_CHUNK, (j + 1) * PROJ_CHUNK)
            y_ref[:, cols] = jnp.dot(mixb_ref[...], w_out_ref[:, cols], preferred_element_type=F32)
        done["proj"] = min(done["proj"] + n_in, n_proj)
        done["out"] = min(done["out"] + n_o, n_out)

    row = lax.broadcasted_iota(jnp.int32, (T, 1), 0)
    pos = (s * T + row + 1).astype(F32)

    ext = jnp.concatenate([halo_ref[:, :pool_w], h_ref[:, :pool_w]], axis=0)
    ssq = jnp.zeros((T, LANES), F32)
    win = ext
    span = 1
    for g, w in enumerate(POOL_WINDOWS):
        while span < w:
            win = win + _shift_rows(win, span)
            span *= 2
        sw = win[HALO:, :gdim]
        if g + 1 < len(POOL_WINDOWS):
            win = win[:, gdim:]
        u = h_ref[:, g * gdim:(g + 1) * gdim]
        d = sw / jnp.minimum(pos, float(w)) - u
        y = jnp.dot(d.astype(BF16), w_pool_ref[g], preferred_element_type=F32)
        y = y * pool_scale_ref[:, g * gdim:(g + 1) * gdim]
        ssq = ssq + y * y
        mix_ref[:, g * gdim:(g + 1) * gdim] = y
    inv = lax.rsqrt(jnp.sum(ssq, axis=-1, keepdims=True) / pool_w + EPS)
    mixn_ref[:, :pool_w] = (mix_ref[:, :pool_w] * inv * norm_g_ref[:, :pool_w]).astype(BF16)
    fill_mxu(2, 0)

    ssq = jnp.zeros((T, LANES), F32)
    for hh in range(lru_w // bdim):
        c0 = hh * bdim
        ext = jnp.concatenate([halo_ref[:, o_lru + c0:o_lru + c0 + bdim],
                               h_ref[:, o_lru + c0:o_lru + c0 + bdim]], axis=0)
        cw = conv_w_ref[:, c0:c0 + bdim]
        uc = ext * cw[CONV_WIDTH - 1:CONV_WIDTH]
        for k in range(1, CONV_WIDTH):
            uc = uc + _shift_rows(ext, k) * cw[CONV_WIDTH - 1 - k:CONV_WIDTH - k]
        uc = uc[HALO:] + conv_b_ref[:, c0:c0 + bdim]
        ga = jnp.dot(uc.astype(BF16), w_ax_ref[hh], preferred_element_type=F32)
        r = _sigmoid(ga[:, :bdim] + b_ax_ref[:, c0:c0 + bdim])
        i = _sigmoid(ga[:, bdim:] + b_ax_ref[:, lru_w + c0:lru_w + c0 + bdim])
        log_a = -LRU_C * r * nsp_ref[:, c0:c0 + bdim]
        a = jnp.exp(log_a)
        mult = jnp.sqrt(jnp.maximum(1.0 - a * a, 0.0))
        bt = mult * i * uc
        hs = _linear_scan(a, bt, state_ref[:, c0:c0 + bdim])
        state_ref[:, c0:c0 + bdim] = hs[T - 1:T]
        y = _gelu_tanh(h_ref[:, o_gate + c0:o_gate + c0 + bdim]) * hs
        ssq = ssq + y * y
        mix_ref[:, o_lru + c0:o_lru + c0 + bdim] = y
        fill_mxu(1, 1)
    inv = lax.rsqrt(jnp.sum(ssq, axis=-1, keepdims=True) / lru_w + EPS)
    mixn_ref[:, o_lru:o_gate] = (mix_ref[:, o_lru:o_gate] * inv * norm_g_ref[:, o_lru:o_gate]).astype(BF16)

    halo_ref[...] = h_ref[T - HALO:, :o_gate]

    ssq = jnp.zeros((T, LANES), F32)
    m0 = o_gate
    for hh in range(n_heads):
        c0 = hh * head_dim
        q = h_ref[:, o_q + c0:o_q + c0 + head_dim].astype(BF16)
        kh = kv_ref[0, :, c0:c0 + head_dim]
        vh = kv_ref[0, :, mem_w + c0:mem_w + c0 + head_dim]
        sc = lax.dot_general(q, kh, (((1,), (1,)), ((), ())), preferred_element_type=F32) * (head_dim ** -0.5)
        p = jnp.exp(sc - jnp.max(sc, axis=-1, keepdims=True))
        p = p / jnp.sum(p, axis=-1, keepdims=True)
        y = jnp.dot(p.astype(BF16), vh, preferred_element_type=F32)
        ssq = ssq + y * y
        mix_ref[:, m0 + c0:m0 + c0 + head_dim] = y
        fill_mxu(1, 1)
    inv = lax.rsqrt(jnp.sum(ssq, axis=-1, keepdims=True) / mem_w + EPS)
    mixn_ref[:, m0:] = (mix_ref[:, m0:] * inv * norm_g_ref[:, m0:]).astype(BF16)

    fill_mxu(n_proj, n_out)

    z = alpha * xr_ref[0] + y_ref[...]
    mu = jnp.mean(z, axis=-1, keepdims=True)
    zc = z - mu
    var = jnp.mean(zc * zc, axis=-1, keepdims=True)
    x1 = zc * lax.rsqrt(var + EPS) * ln_g_ref[...] + ln_b_ref[...]
    x1_ref[0] = x1

    xh = x1.astype(BF16)
    xl = (x1 - xh.astype(F32)).astype(BF16)
    _store_rows(xp_ref, x1, x1.shape[1] // LANES)
    lg = jnp.dot(jnp.concatenate([xh, xl], axis=0), w_route_ref[...], preferred_element_type=F32)
    top = lg[:T]
    logits = top + lg[T:] + pltpu.roll(top, LANES - ROUTE_LO, 1) + b_route_ref[...]
    lane_i = lax.broadcasted_iota(jnp.int32, (T, LANES), 1)
    lane = lane_i.astype(F32)
    lane_grp = ((lane_i - N_GROUPS) >> 3).astype(F32)

    is_g = lane_i < N_GROUPS
    gl = jnp.where(is_g, logits, NEG)
    gmax = jnp.max(gl, axis=-1, keepdims=True)
    gsum = jnp.sum(jnp.where(is_g, jnp.exp(gl - gmax), 0.0), axis=-1, keepdims=True)
    g_p = 1.0 / gsum
    g_idx = jnp.min(jnp.where(gl == gmax, lane, float(LANES)), axis=-1, keepdims=True)

    in_grp = (lane_i >= N_GROUPS) & (lane_i < N_GROUPS + N_EXPERTS) & (lane_grp == g_idx)
    fl = jnp.where(in_grp, logits, NEG)
    l1 = jnp.max(fl, axis=-1, keepdims=True)
    i1 = jnp.min(jnp.where(fl == l1, lane, float(LANES)), axis=-1, keepdims=True)
    fl2 = jnp.where(lane == i1, NEG, fl)
    l2 = jnp.max(fl2, axis=-1, keepdims=True)
    i2 = jnp.min(jnp.where(fl2 == l2, lane, float(LANES)), axis=-1, keepdims=True)
    e21 = jnp.exp(l2 - l1)
    w1 = g_p / (1.0 + e21)
    w2 = g_p * e21 / (1.0 + e21)
    e1 = i1 - N_GROUPS
    e2 = i2 - N_GROUPS

    oh1 = (lane == e1).astype(F32)
    oh2 = (lane == e2).astype(F32)
    both = (oh1 + oh2) * (g_step > 0).astype(F32)
    tri =(lax.broadcasted_iota(jnp.int32, (T, T), 0) > lax.broadcasted_iota(jnp.int32, (T, T), 1))
    pfx = jnp.dot(tri.astype(F32).astype(BF16), both.astype(BF16), preferred_element_type=F32)
    base = pfx + cnt_ref[...]
    r1 = jnp.sum(oh1 * base, axis=-1, keepdims=True)
    r2 = jnp.sum(oh2 * base, axis=-1, keepdims=True)
    cnt_ref[...] = cnt_ref[...] + jnp.sum(both, axis=0, keepdims=True)
    counts_ref[...] = jnp.broadcast_to(cnt_ref[...], counts_ref.shape)

    route = jnp.where(lane_i == 0, e1,
            jnp.where(lane_i == 1, e2,
            jnp.where(lane_i == 2, r1,
            jnp.where(lane_i == 3, r2,
            jnp.where(lane_i == 4, w1,
            jnp.where(lane_i == 5, w2, 0.0))))))
    route_ref[0] = route
    route_t_ref[0] = route.T[:8]


def _mixer(x, kv, w_in, w_pool, pool_scale, conv_w, conv_b, w_ax, b_ax, nsp, norm_g, w_out, ln_g, ln_b,
           w_route, b_route, alpha):
    B, S, D = x.shape
    T = SEQ_TILE
    in_w = w_in.shape[1]
    pool_w = pool_scale.shape[1]
    lru_w = conv_b.shape[1]
    mem_w = kv.shape[2] // 2
    gdim = w_pool.shape[1]
    bdim = w_ax.shape[1]
    n_heads = 4
    head_dim = mem_w // n_heads
    mix_w = w_out.shape[0]
    dims = (T, pool_w, lru_w, mem_w, n_heads, head_dim, gdim, bdim)
    assert S % T == 0 and T % 8 == 0
    assert gdim == bdim == head_dim == LANES and in_w % PROJ_CHUNK == 0

    const2 = lambda g: (0, 0)
    const3 = lambda g: (0, 0, 0)
    single = pl.Buffered(1)

    def full(a):
        return pl.BlockSpec(a.shape, const2 if a.ndim == 2 else const3, pipeline_mode=single)

    n_s = S // T
    n_tiles = B * n_s

    def tile_index(t):
        return (t // n_s, t % n_s, 0)

    prev_tile = lambda g: jnp.maximum(g - 1, 0)
    in_specs = [
        pl.BlockSpec((1, T, D), lambda g: tile_index(prev_tile(g))),
        pl.BlockSpec((1, T, D), lambda g: tile_index(jnp.minimum(g + 1, n_tiles - 1))),
        pl.BlockSpec((1,) + kv.shape[1:], lambda g: (jnp.minimum(g // n_s, B - 1), 0, 0)),
        full(w_in), full(w_pool), full(pool_scale), full(conv_w), full(conv_b), full(w_ax), full(b_ax),
        full(nsp), full(norm_g), full(w_out), full(ln_g), full(ln_b), full(w_route), full(b_route),
    ]
    tiles = _row_tiles(D)
    out_specs = [
        pl.BlockSpec((1, T, D), lambda g: tile_index(prev_tile(g))),
        pl.BlockSpec((T * tiles, LANES), lambda g: (prev_tile(g), 0)),
        pl.BlockSpec((1, T, LANES), lambda g: tile_index(prev_tile(g))),
        pl.BlockSpec((1, 8, T), lambda g: (prev_tile(g), 0, 0)),
        pl.BlockSpec((8, LANES), const2),
    ]
    out_shape = [
        jax.ShapeDtypeStruct((B, S, D), F32),
        jax.ShapeDtypeStruct((B * S * tiles, LANES), F32),
        jax.ShapeDtypeStruct((B, S, LANES), F32),
        jax.ShapeDtypeStruct((n_tiles, 8, T), F32),
        jax.ShapeDtypeStruct((8, LANES), F32),
    ]
    scratch = [
        pltpu.VMEM((T, in_w), F32),
        pltpu.VMEM((T, in_w), F32),
        pltpu.VMEM((T, mix_w), F32),
        pltpu.VMEM((T, mix_w), BF16),
        pltpu.VMEM((T, mix_w), BF16),
        pltpu.VMEM((T, D), F32),
        pltpu.VMEM((HALO, pool_w + lru_w), F32),
        pltpu.VMEM((1, lru_w), F32),
        pltpu.VMEM((1, LANES), F32),
    ]
    return pl.pallas_call(
        functools.partial(_mixer_kernel, dims, alpha, n_s),
        grid=(n_tiles + 1,),
        in_specs=in_specs,
        out_specs=out_specs,
        out_shape=out_shape,
        scratch_shapes=scratch,
        compiler_params=pltpu.CompilerParams(dimension_semantics=("arbitrary",),
                                             vmem_limit_bytes=VMEM_LIMIT),
        name="mixer_router",
    )(x, x, kv, w_in, w_pool, pool_scale, conv_w, conv_b, w_ax, b_ax, nsp, norm_g, w_out, ln_g, ln_b,
      w_route, b_route)


def _ffn_kernel(tiles, bexp_ref, first_ref, nexte_ref, nused_ref,
                idx_cur_ref, idx_nxt_ref, xp_hbm, wg_hbm, wu_hbm, wd_hbm, ys_hbm,
                xbuf, obuf, stage_g, stage_u, stage_d, wg_ref, wu_ref, wd_ref, gsem, ssem, wsem):
    R = EXPERT_ROWS
    pitch = _vmem_pitch(tiles)
    r = pl.program_id(0)
    n_used = nused_ref[0]
    cur = r % 2

    def weight_copies(e):
        return (pltpu.make_async_copy(wg_hbm.at[e], stage_g, wsem.at[0]),
                pltpu.make_async_copy(wu_hbm.at[e], stage_u, wsem.at[1]),
                pltpu.make_async_copy(wd_hbm.at[e], stage_d, wsem.at[2]))

    def gather_copy(src, i, sl):
        return pltpu.make_async_copy(xp_hbm.at[pl.ds(pl.multiple_of(src, tiles), tiles)],
                                     xbuf.at[sl, pl.ds(i * pitch, tiles)], gsem.at[sl])

    def scatter_copy(dst, i, sl):
        return pltpu.make_async_copy(obuf.at[sl, pl.ds(i * pitch, tiles)],
                                     ys_hbm.at[pl.ds(pl.multiple_of(dst, tiles), tiles)], ssem.at[sl])

    def wait_gathers(sl):
        pltpu.make_async_copy(xp_hbm.at[pl.ds(0, R * tiles)], xbuf.at[sl, pl.ds(0, R * tiles)], gsem.at[sl]).wait()

    def wait_scatters(sl):
        pltpu.make_async_copy(obuf.at[sl, pl.ds(0, R * tiles)], ys_hbm.at[pl.ds(0, R * tiles)], ssem.at[sl]).wait()

    @pl.when(r == 0)
    def _():
        for c in weight_copies(bexp_ref[0]):
            c.start(priority=WEIGHT_DMA_PRIORITY)
        for i in range(R):
            gather_copy(idx_cur_ref[0, 0, i], i, 0).start()
        obuf[1] = jnp.zeros(obuf.shape[1:], F32)
        spare = ys_hbm.shape[0] - 2 * R * tiles
        for k in range(2):
            pltpu.make_async_copy(obuf.at[1, pl.ds(0, R * tiles)],
                                  ys_hbm.at[pl.ds(spare + k * R * tiles, R * tiles)], ssem.at[1]).start()
        for k in range(2):
            wait_scatters(1)

    @pl.when(r < n_used)
    def _():
        @pl.when(r + 1 < n_used)
        def _():
            for i in range(R):
                gather_copy(idx_nxt_ref[0, 0, i], i, 1 - cur).start()

        @pl.when(first_ref[r] == 1)
        def _():
            for c in weight_copies(0):
                c.wait()
            n_chunks = 8
            kg = stage_g.shape[0] // n_chunks
            kd = stage_d.shape[0] // n_chunks

            def cast_chunk(j, carry):
                sg = pl.ds(pl.multiple_of(j * kg, kg), kg)
                sd = pl.ds(pl.multiple_of(j * kd, kd), kd)
                wg_ref[sg, :] = stage_g[sg, :].astype(BF16)
                wu_ref[sg, :] = stage_u[sg, :].astype(BF16)
                wd_ref[sd, :] = stage_d[sd, :].astype(BF16)
                return carry
            lax.fori_loop(0, n_chunks, cast_chunk, 0)

            @pl.when(nexte_ref[r] >= 0)
            def _():
                for c in weight_copies(nexte_ref[r]):
                    c.start(priority=WEIGHT_DMA_PRIORITY)

        wait_gathers(cur)

        @pl.when(r >= 2)
        def _():
            wait_scatters(cur)

        xb = _load_rows(xbuf.at[cur], R, tiles, pitch).astype(BF16)
        g = jnp.dot(xb, wg_ref[...], preferred_element_type=F32)
        u = jnp.dot(xb, wu_ref[...], preferred_element_type=F32)
        hdn = (g * _sigmoid(g)) * u
        y = jnp.dot(hdn.astype(BF16), wd_ref[...], preferred_element_type=F32)
        _store_rows(obuf.at[cur], y, pitch)

        for i in range(R):
            scatter_copy(idx_cur_ref[0, 0, R + i], i, cur).start(priority=i % 2)

        @pl.when(r == n_used - 1)
        def _():
            wait_scatters(cur)

            @pl.when(r >= 1)
            def _():
                wait_scatters(1 - cur)


def _expert_ffn(xp, idx, plan, w_gate, w_up, w_down, n_rows_out):
    nb = idx.shape[0]
    R = EXPERT_ROWS
    D, FF = w_gate.shape[1:]
    tiles = _row_tiles(D)
    pitch = _vmem_pitch(tiles)
    idx_block = (1, 1, 2 * R)
    grid_spec = pltpu.PrefetchScalarGridSpec(
        num_scalar_prefetch=4,
        grid=(nb,),
        in_specs=[
            pl.BlockSpec(idx_block, lambda r, *_: (r, 0, 0), memory_space=pltpu.SMEM),
            pl.BlockSpec(idx_block, lambda r, *_: (jnp.minimum(r + 1, nb - 1), 0, 0), memory_space=pltpu.SMEM),
            pl.BlockSpec(memory_space=pl.ANY),
            pl.BlockSpec(memory_space=pl.ANY),
            pl.BlockSpec(memory_space=pl.ANY),
            pl.BlockSpec(memory_space=pl.ANY),
        ],
        out_specs=pl.BlockSpec(memory_space=pl.ANY),
        scratch_shapes=[
            pltpu.VMEM((2, R * pitch, LANES), F32),
            pltpu.VMEM((2, R * pitch, LANES), F32),
            pltpu.VMEM((D, FF), F32),
            pltpu.VMEM((D, FF), F32),
            pltpu.VMEM((FF, D), F32),
            pltpu.VMEM((D, FF), BF16),
            pltpu.VMEM((D, FF), BF16),
            pltpu.VMEM((FF, D), BF16),
            pltpu.SemaphoreType.DMA((2,)),
            pltpu.SemaphoreType.DMA((2,)),
            pltpu.SemaphoreType.DMA((3,)),
        ],
    )
    return pl.pallas_call(
        functools.partial(_ffn_kernel, tiles),
        grid_spec=grid_spec,
        out_shape=jax.ShapeDtypeStruct((n_rows_out * tiles, LANES), F32),
        compiler_params=pltpu.CompilerParams(dimension_semantics=("arbitrary",),
                                             vmem_limit_bytes=VMEM_LIMIT),
        name="expert_ffn",
    )(*plan, idx, idx, xp, w_gate, w_up, w_down)


def _combine_kernel(alpha, tiles, x1_ref, y0_ref, y1_ref, route_ref, g_ref, b_ref, out_ref):
    T = x1_ref.shape[0]
    rt = route_ref[...]
    y0 = _load_rows(y0_ref, T, tiles, tiles)
    y1 = _load_rows(y1_ref, T, tiles, tiles)
    z = alpha * x1_ref[...] + rt[:, 4:5] * y0 + rt[:, 5:6] * y1
    mu = jnp.mean(z, axis=-1, keepdims=True)
    zc = z - mu
    var = jnp.mean(zc * zc, axis=-1, keepdims=True)
    out_ref[...] = zc * lax.rsqrt(var + EPS) * g_ref[...] + b_ref[...]


def _combine(x1, ys, route, ln_g, ln_b, alpha):
    N, D = x1.shape
    T = COMBINE_TILE
    assert N % T == 0
    off = N // T
    tiles = _row_tiles(D)
    return pl.pallas_call(
        functools.partial(_combine_kernel, alpha, tiles),
        grid=(N // T,),
        in_specs=[
            pl.BlockSpec((T, D), lambda i: (i, 0)),
            pl.BlockSpec((T * tiles, LANES), lambda i: (i, 0)),
            pl.BlockSpec((T * tiles, LANES), lambda i: (i + off, 0)),
            pl.BlockSpec((T, LANES), lambda i: (i, 0)),
            pl.BlockSpec((1, D), lambda i: (0, 0)),
            pl.BlockSpec((1, D), lambda i: (0, 0)),
        ],
        out_specs=pl.BlockSpec((T, D), lambda i: (i, 0)),
        out_shape=jax.ShapeDtypeStruct((N, D), F32),
        compiler_params=pltpu.CompilerParams(dimension_semantics=("arbitrary",),
                                             vmem_limit_bytes=VMEM_LIMIT),
        name="combine_ln",
    )(x1, ys, ys, route, ln_g, ln_b)


def _route_weights(w_group, w_fine):
    w = jnp.concatenate([w_group, w_fine], axis=1)
    wh = w.astype(BF16)
    wl = (w - wh.astype(F32)).astype(BF16)
    n = w.shape[1]
    out = jnp.zeros((w.shape[0], LANES), BF16)
    out = out.at[:, :n].set(wh)
    return out.at[:, ROUTE_LO:ROUTE_LO + n].set(wl)


def _layer(x, mem, w_in, w_pool, pool_scale, conv_w, conv_b, w_a, b_a, w_x, b_x, lam, w_mem_kv, mix_norm_g,
           w_out, ln1_g, ln1_b, w_group, b_group, w_fine, b_fine, w_gate, w_up, w_down, ln2_g, ln2_b, alpha):
    B, S, D = x.shape
    N = B * S
    R = EXPERT_ROWS
    row = lambda v: v.reshape(1, -1).astype(F32)

    kv = _kv_project(mem, w_mem_kv.astype(BF16))
    w_ax = jnp.concatenate([w_a, w_x], axis=2).astype(BF16)
    b_ax = jnp.concatenate([b_a, b_x]).reshape(1, -1)
    nsp = row(jax.nn.softplus(-lam.astype(F32)))
    b_route = jnp.zeros((1, LANES), F32).at[0, :N_GROUPS + N_EXPERTS].set(
        jnp.concatenate([b_group, b_fine.reshape(-1)]))
    x1, xp, route, route_t, counts = _mixer(
        x, kv, w_in.astype(BF16), w_pool.astype(BF16), row(pool_scale), conv_w, row(conv_b), w_ax, b_ax, nsp,
        row(mix_norm_g), w_out.astype(BF16), row(ln1_g), row(ln1_b), _route_weights(w_group, w_fine), b_route,
        alpha)
    x1 = x1.reshape(N, D)
    route = route.reshape(N, LANES)

    e_idx = route_t[:, 0:2, :].astype(jnp.int32)
    rank = route_t[:, 2:4, :].astype(jnp.int32)
    cnt = counts[0, :N_EXPERTS].astype(jnp.int32)
    padded = (cnt + R - 1) // R * R
    pad_ends = jnp.cumsum(padded)
    pad_starts = pad_ends - padded
    ex = jnp.arange(N_EXPERTS, dtype=jnp.int32)
    start_of = jnp.sum(jnp.where(e_idx[..., None] == ex, pad_starts, 0), axis=-1)
    dest = start_of + rank
    n_blocks = (2 * N + N_EXPERTS * R) // R
    P = n_blocks * R
    block_start = jnp.arange(n_blocks, dtype=jnp.int32) * R
    block_exp = jnp.minimum(jnp.sum((pad_ends[None, :] <= block_start[:, None]).astype(jnp.int32), axis=1),
                            N_EXPERTS - 1)
    tok_tile = route_t.shape[2]
    slot = (jnp.arange(N, dtype=jnp.int32).reshape(-1, 1, tok_tile)
            + jnp.array([0, N], jnp.int32).reshape(1, 2, 1))
    p = jnp.arange(P, dtype=jnp.int32)
    spare = 2 * N + ((p // R) % 2) * R + (p % R)
    buf_slot = spare.at[dest.reshape(-1)].set(slot.reshape(-1), unique_indices=True)
    buf_tok = jnp.where(buf_slot < 2 * N, buf_slot % N, 0)
    tiles = _row_tiles(D)
    idx = jnp.concatenate([buf_tok.reshape(n_blocks, 1, R), buf_slot.reshape(n_blocks, 1, R)], axis=2) * tiles

    n_used = (pad_ends[-1:] // R).astype(jnp.int32)
    blk = jnp.arange(n_blocks, dtype=jnp.int32)
    first = ((blk < n_used) & ((blk == 0) | (block_exp != jnp.roll(block_exp, 1)))).astype(jnp.int32)
    later_active = (padded[None, :] > 0) & (ex[None, :] > ex[:, None])
    next_active = jnp.min(jnp.where(later_active, ex[None, :], N_EXPERTS), axis=1)
    next_active = jnp.where(next_active == N_EXPERTS, -1, next_active).astype(jnp.int32)
    plan = (block_exp.astype(jnp.int32), first, next_active[block_exp], n_used)

    ys = _expert_ffn(xp, idx, plan, w_gate, w_up, w_down, 2 * N + 2 * R)
    out = _combine(x1, ys, route, row(ln2_g), row(ln2_b), alpha)
    return out.reshape(B, S, D)


def kernel(x, mem, w_in, w_pool, pool_scale, conv_w, conv_b, w_a, b_a, w_x, b_x, lam, w_mem_kv, mix_norm_g, w_out,
           ln1_g, ln1_b, w_group, b_group, w_fine, b_fine, w_gate, w_up, w_down, ln2_g, ln2_b):
    depth = w_in.shape[0]
    alpha = (2 * depth) ** 0.25
    for l in range(depth):
        x = _layer(x, mem, w_in[l], w_pool[l], pool_scale[l], conv_w[l], conv_b[l], w_a[l], b_a[l], w_x[l], b_x[l],
                   lam[l], w_mem_kv[l], mix_norm_g[l], w_out[l], ln1_g[l], ln1_b[l], w_group[l], b_group[l],
                   w_fine[l], b_fine[l], w_gate[l], w_up[l], w_down[l], ln2_g[l], ln2_b[l], alpha)
    return x
```

```python
import functools

import jax
import jax.numpy as jnp
from jax import lax
from jax.experimental import pallas as pl
from jax.experimental.pallas import tpu as pltpu

F32 = jnp.float32
BF16 = jnp.bfloat16

POOL_WINDOWS = (2, 4, 8, 16)
LRU_C = 8.0
CONV_WIDTH = 4
N_GROUPS = 4
EXPERTS_PER_GROUP = 8
N_EXPERTS = N_GROUPS * EXPERTS_PER_GROUP
EPS = 1e-5

LANES = 128
HALO = 16
SEQ_TILE = 256
PROJ_CHUNK = 256
EXPERT_ROWS = 128
BLOCKS_PER_STEP = 4
WEIGHT_DMA_PRIORITY = 1
COMBINE_TILE = 512
ROUTE_LO = 64
NEG = -1e30
VMEM_LIMIT = 56 * 1024 * 1024


def _sigmoid(v):
    return 1.0 / (1.0 + jnp.exp(-v))


def _gelu_tanh(v):
    return 0.5 * v * (1.0 + jnp.tanh(0.7978845608028654 * (v + 0.044715 * (v * v * v))))


def _shift_rows(v, k):
    return pltpu.roll(v, k, 0)


def _linear_scan(a, b, h0):
    n = a.shape[0]
    groups = n // 8
    a3 = a.reshape(groups, 8, LANES)
    b3 = b.reshape(groups, 8, LANES)
    sub = lax.broadcasted_iota(jnp.int32, (groups, 8, LANES), 1)
    for k in (1, 2, 4):
        keep = sub >= k
        b_prev = jnp.where(keep, pltpu.roll(b3, k, 1), 0.0)
        a_prev = jnp.where(keep, pltpu.roll(a3, k, 1), 1.0)
        b3 = b3 + a3 * b_prev
        a3 = a3 * a_prev
    carry = h0
    out = []
    for g in range(groups):
        hg = a3[g] * carry + b3[g]
        out.append(hg)
        carry = hg[7:8]
    return jnp.concatenate(out, axis=0)


def _row_tiles(d):
    tiles, rem = divmod(d, LANES)
    assert rem == 0 and tiles % 8 == 0, "a row must be whole (8, 128) tiles to be one contiguous DMA"
    return tiles


def _vmem_pitch(tiles):
    return tiles if (tiles // 8) % 2 else tiles + 8


def _store_rows(ref, v, pitch):
    n, d = v.shape
    for c in range(d // LANES):
        ref[pl.ds(c, n, stride=pitch), :] = v[:, c * LANES:(c + 1) * LANES]


def _load_rows(ref, n, tiles, pitch):
    return jnp.concatenate([ref[pl.ds(c, n, stride=pitch), :] for c in range(tiles)], axis=1)


def _kv_kernel(mem_ref, w_ref, kv_ref):
    kv_ref[0] = jnp.dot(mem_ref[0].astype(BF16), w_ref[...], preferred_element_type=F32).astype(BF16)


def _kv_project(mem, w_kv):
    B, M, D = mem.shape
    E = w_kv.shape[1]
    return pl.pallas_call(
        _kv_kernel,
        grid=(B,),
        in_specs=[pl.BlockSpec((1, M, D), lambda b: (b, 0, 0)),
                  pl.BlockSpec((D, E), lambda b: (0, 0))],
        out_specs=pl.BlockSpec((1, M, E), lambda b: (b, 0, 0)),
        out_shape=jax.ShapeDtypeStruct((B, M, E), BF16),
        compiler_params=pltpu.CompilerParams(dimension_semantics=("arbitrary",),
                                             vmem_limit_bytes=VMEM_LIMIT),
        name="kv_project",
    )(mem, w_kv)


def _mixer_kernel(dims, alpha, n_s,
                  xr_ref, xn_ref, kv_ref, w_in_ref, w_pool_ref, pool_scale_ref, conv_w_ref, conv_b_ref,
                  w_ax_ref, b_ax_ref, nsp_ref, norm_g_ref, w_out_ref, ln_g_ref, ln_b_ref,
                  w_route_ref, b_route_ref,
                  x1_ref, xp_ref, route_ref, route_t_ref, counts_ref,
                  h_ref, hn_ref, mix_ref, mixb_ref, mixn_ref, y_ref, halo_ref, state_ref, cnt_ref):
    T, pool_w, lru_w, mem_w, n_heads, head_dim, gdim, bdim = dims
    g_step = pl.program_id(0)
    s = lax.rem(g_step, n_s)
    o_lru = pool_w
    o_gate = o_lru + lru_w
    o_q = o_gate + lru_w

    @pl.when(s == 0)
    def _():
        halo_ref[...] = jnp.zeros_like(halo_ref)
        state_ref[...] = jnp.zeros_like(state_ref)

    @pl.when(g_step == 0)
    def _():
        cnt_ref[...] = jnp.zeros_like(cnt_ref)
        hn_ref[...] = jnp.dot(xr_ref[0].astype(BF16), w_in_ref[...], preferred_element_type=F32)
        mixn_ref[...] = jnp.zeros_like(mixn_ref)

    h_ref[...] = hn_ref[...]
    mixb_ref[...] = mixn_ref[...]
    xnb = xn_ref[0].astype(BF16)
    n_proj = w_in_ref.shape[1] // PROJ_CHUNK
    n_out = w_out_ref.shape[1] // PROJ_CHUNK
    done = {"proj": 0, "out": 0}

    def fill_mxu(n_in, n_o):
        for j in range(done["proj"], min(done["proj"] + n_in, n_proj)):
            cols = slice(j * PROJ_CHUNK, (j + 1) * PROJ_CHUNK)
            hn_ref[:, cols] = jnp.dot(xnb, w_in_ref[:, cols], preferred_element_type=F32)
        for j in range(done["out"], min(done["out"] + n_o, n_out)):
            cols = slice(j * PROJ_CHUNK, (j + 1) * PROJ_CHUNK)
            y_ref[:, cols] = jnp.dot(mixb_ref[...], w_out_ref[:, cols], preferred_element_type=F32)
        done["proj"] = min(done["proj"] + n_in, n_proj)
        done["out"] = min(done["out"] + n_o, n_out)

    row = lax.broadcasted_iota(jnp.int32, (T, 1), 0)
    pos = (s * T + row + 1).astype(F32)

    ext = jnp.concatenate([halo_ref[:, :pool_w], h_ref[:, :pool_w]], axis=0)
    ssq = jnp.zeros((T, LANES), F32)
    win = ext
    span = 1
    for g, w in enumerate(POOL_WINDOWS):
        while span < w:
            win = win + _shift_rows(win, span)
            span *= 2
        sw = win[HALO:, :gdim]
        if g + 1 < len(POOL_WINDOWS):
            win = win[:, gdim:]
        u = h_ref[:, g * gdim:(g + 1) * gdim]
        d = sw / jnp.minimum(pos, float(w)) - u
        y = jnp.dot(d.astype(BF16), w_pool_ref[g], preferred_element_type=F32)
        y = y * pool_scale_ref[:, g * gdim:(g + 1) * gdim]
        ssq = ssq + y * y
        mix_ref[:, g * gdim:(g + 1) * gdim] = y
    inv = lax.rsqrt(jnp.sum(ssq, axis=-1, keepdims=True) / pool_w + EPS)
    mixn_ref[:, :pool_w] = (mix_ref[:, :pool_w] * inv * norm_g_ref[:, :pool_w]).astype(BF16)
    fill_mxu(2, 0)

    ssq = jnp.zeros((T, LANES), F32)
    for hh in range(lru_w // bdim):
        c0 = hh * bdim
        ext = jnp.concatenate([halo_ref[:, o_lru + c0:o_lru + c0 + bdim],
                               h_ref[:, o_lru + c0:o_lru + c0 + bdim]], axis=0)
        cw = conv_w_ref[:, c0:c0 + bdim]
        uc = ext * cw[CONV_WIDTH - 1:CONV_WIDTH]
        for k in range(1, CONV_WIDTH):
            uc = uc + _shift_rows(ext, k) * cw[CONV_WIDTH - 1 - k:CONV_WIDTH - k]
        uc = uc[HALO:] + conv_b_ref[:, c0:c0 + bdim]
        ga = jnp.dot(uc.astype(BF16), w_ax_ref[hh], preferred_element_type=F32)
        r = _sigmoid(ga[:, :bdim] + b_ax_ref[:, c0:c0 + bdim])
        i = _sigmoid(ga[:, bdim:] + b_ax_ref[:, lru_w + c0:lru_w + c0 + bdim])
        log_a = -LRU_C * r * nsp_ref[:, c0:c0 + bdim]
        a = jnp.exp(log_a)
        mult = jnp.sqrt(jnp.maximum(1.0 - a * a, 0.0))
        bt = mult * i * uc
        hs = _linear_scan(a, bt, state_ref[:, c0:c0 + bdim])
        state_ref[:, c0:c0 + bdim] = hs[T - 1:T]
        y = _gelu_tanh(h_ref[:, o_gate + c0:o_gate + c0 + bdim]) * hs
        ssq = ssq + y * y
        mix_ref[:, o_lru + c0:o_lru + c0 + bdim] = y
        fill_mxu(1, 1)
    inv = lax.rsqrt(jnp.sum(ssq, axis=-1, keepdims=True) / lru_w + EPS)
    mixn_ref[:, o_lru:o_gate] = (mix_ref[:, o_lru:o_gate] * inv * norm_g_ref[:, o_lru:o_gate]).astype(BF16)

    halo_ref[...] = h_ref[T - HALO:, :o_gate]

    ssq = jnp.zeros((T, LANES), F32)
    m0 = o_gate
    for hh in range(n_heads):
        c0 = hh * head_dim
        q = h_ref[:, o_q + c0:o_q + c0 + head_dim].astype(BF16)
        kh = kv_ref[0, :, c0:c0 + head_dim]
        vh = kv_ref[0, :, mem_w + c0:mem_w + c0 + head_dim]
        sc = lax.dot_general(q, kh, (((1,), (1,)), ((), ())), preferred_element_type=F32) * (head_dim ** -0.5)
        p = jnp.exp(sc - jnp.max(sc, axis=-1, keepdims=True))
        p = p / jnp.sum(p, axis=-1, keepdims=True)
        y = jnp.dot(p.astype(BF16), vh, preferred_element_type=F32)
        ssq = ssq + y * y
        mix_ref[:, m0 + c0:m0 + c0 + head_dim] = y
        fill_mxu(1, 1)
    inv = lax.rsqrt(jnp.sum(ssq, axis=-1, keepdims=True) / mem_w + EPS)
    mixn_ref[:, m0:] = (mix_ref[:, m0:] * inv * norm_g_ref[:, m0:]).astype(BF16)

    fill_mxu(n_proj, n_out)

    z = alpha * xr_ref[0] + y_ref[...]
    mu = jnp.mean(z, axis=-1, keepdims=True)
    zc = z - mu
    var = jnp.mean(zc * zc, axis=-1, keepdims=True)
    x1 = zc * lax.rsqrt(var + EPS) * ln_g_ref[...] + ln_b_ref[...]
    x1_ref[0] = x1

    xh = x1.astype(BF16)
    xl = (x1 - xh.astype(F32)).astype(BF16)
    _store_rows(xp_ref, x1, x1.shape[1] // LANES)
    lg = jnp.dot(jnp.concatenate([xh, xl], axis=0), w_route_ref[...], preferred_element_type=F32)
    top = lg[:T]
    logits = top + lg[T:] + pltpu.roll(top, LANES - ROUTE_LO, 1) + b_route_ref[...]
    lane_i = lax.broadcasted_iota(jnp.int32, (T, LANES), 1)
    lane = lane_i.astype(F32)
    lane_grp = ((lane_i - N_GROUPS) >> 3).astype(F32)

    is_g = lane_i < N_GROUPS
    gl = jnp.where(is_g, logits, NEG)
    gmax = jnp.max(gl, axis=-1, keepdims=True)
    gsum = jnp.sum(jnp.where(is_g, jnp.exp(gl - gmax), 0.0), axis=-1, keepdims=True)
    g_p = 1.0 / gsum
    g_idx = jnp.min(jnp.where(gl == gmax, lane, float(LANES)), axis=-1, keepdims=True)

    in_grp = (lane_i >= N_GROUPS) & (lane_i < N_GROUPS + N_EXPERTS) & (lane_grp == g_idx)
    fl = jnp.where(in_grp, logits, NEG)
    l1 = jnp.max(fl, axis=-1, keepdims=True)
    i1 = jnp.min(jnp.where(fl == l1, lane, float(LANES)), axis=-1, keepdims=True)
    fl2 = jnp.where(lane == i1, NEG, fl)
    l2 = jnp.max(fl2, axis=-1, keepdims=True)
    i2 = jnp.min(jnp.where(fl2 == l2, lane, float(LANES)), axis=-1, keepdims=True)
    e21 = jnp.exp(l2 - l1)
    w1 = g_p / (1.0 + e21)
    w2 = g_p * e21 / (1.0 + e21)
    e1 = i1 - N_GROUPS
    e2 = i2 - N_GROUPS

    oh1 = (lane == e1).astype(F32)
    oh2 = (lane == e2).astype(F32)
    both = (oh1 + oh2) * (g_step > 0).astype(F32)
    tri =(lax.broadcasted_iota(jnp.int32, (T, T), 0) > lax.broadcasted_iota(jnp.int32, (T, T), 1))
    pfx = jnp.dot(tri.astype(F32).astype(BF16), both.astype(BF16), preferred_element_type=F32)
    base = pfx + cnt_ref[...]
    r1 = jnp.sum(oh1 * base, axis=-1, keepdims=True)
    r2 = jnp.sum(oh2 * base, axis=-1, keepdims=True)
    cnt_ref[...] = cnt_ref[...] + jnp.sum(both, axis=0, keepdims=True)
    counts_ref[...] = jnp.broadcast_to(cnt_ref[...], counts_ref.shape)

    route = jnp.where(lane_i == 0, e1,
            jnp.where(lane_i == 1, e2,
            jnp.where(lane_i == 2, r1,
            jnp.where(lane_i == 3, r2,
            jnp.where(lane_i == 4, w1,
            jnp.where(lane_i == 5, w2, 0.0))))))
    route_ref[0] = route
    route_t_ref[0] = route.T[:8]


def _mixer(x, kv, w_in, w_pool, pool_scale, conv_w, conv_b, w_ax, b_ax, nsp, norm_g, w_out, ln_g, ln_b,
           w_route, b_route, alpha):
    B, S, D = x.shape
    T = SEQ_TILE
    in_w = w_in.shape[1]
    pool_w = pool_scale.shape[1]
    lru_w = conv_b.shape[1]
    mem_w = kv.shape[2] // 2
    gdim = w_pool.shape[1]
    bdim = w_ax.shape[1]
    n_heads = 4
    head_dim = mem_w // n_heads
    mix_w = w_out.shape[0]
    dims = (T, pool_w, lru_w, mem_w, n_heads, head_dim, gdim, bdim)
    assert S % T == 0 and T % 8 == 0
    assert gdim == bdim == head_dim == LANES and in_w % PROJ_CHUNK == 0

    const2 = lambda g: (0, 0)
    const3 = lambda g: (0, 0, 0)
    single = pl.Buffered(1)

    def full(a):
        return pl.BlockSpec(a.shape, const2 if a.ndim == 2 else const3, pipeline_mode=single)

    n_s = S // T
    n_tiles = B * n_s

    def tile_index(t):
        return (t // n_s, t % n_s, 0)

    prev_tile = lambda g: jnp.maximum(g - 1, 0)
    in_specs = [
        pl.BlockSpec((1, T, D), lambda g: tile_index(prev_tile(g))),
        pl.BlockSpec((1, T, D), lambda g: tile_index(jnp.minimum(g + 1, n_tiles - 1))),
        pl.BlockSpec((1,) + kv.shape[1:], lambda g: (jnp.minimum(g // n_s, B - 1), 0, 0)),
        full(w_in), full(w_pool), full(pool_scale), full(conv_w), full(conv_b), full(w_ax), full(b_ax),
        full(nsp), full(norm_g), full(w_out), full(ln_g), full(ln_b), full(w_route), full(b_route),
    ]
    tiles = _row_tiles(D)
    out_specs = [
        pl.BlockSpec((1, T, D), lambda g: tile_index(prev_tile(g))),
        pl.BlockSpec((T * tiles, LANES), lambda g: (prev_tile(g), 0)),
        pl.BlockSpec((1, T, LANES), lambda g: tile_index(prev_tile(g))),
        pl.BlockSpec((1, 8, T), lambda g: (prev_tile(g), 0, 0)),
        pl.BlockSpec((8, LANES), const2),
    ]
    out_shape = [
        jax.ShapeDtypeStruct((B, S, D), F32),
        jax.ShapeDtypeStruct((B * S * tiles, LANES), F32),
        jax.ShapeDtypeStruct((B, S, LANES), F32),
        jax.ShapeDtypeStruct((n_tiles, 8, T), F32),
        jax.ShapeDtypeStruct((8, LANES), F32),
    ]
    scratch = [
        pltpu.VMEM((T, in_w), F32),
        pltpu.VMEM((T, in_w), F32),
        pltpu.VMEM((T, mix_w), F32),
        pltpu.VMEM((T, mix_w), BF16),
        pltpu.VMEM((T, mix_w), BF16),
        pltpu.VMEM((T, D), F32),
        pltpu.VMEM((HALO, pool_w + lru_w), F32),
        pltpu.VMEM((1, lru_w), F32),
        pltpu.VMEM((1, LANES), F32),
    ]
    return pl.pallas_call(
        functools.partial(_mixer_kernel, dims, alpha, n_s),
        grid=(n_tiles + 1,),
        in_specs=in_specs,
        out_specs=out_specs,
        out_shape=out_shape,
        scratch_shapes=scratch,
        compiler_params=pltpu.CompilerParams(dimension_semantics=("arbitrary",),
                                             vmem_limit_bytes=VMEM_LIMIT),
        name="mixer_router",
    )(x, x, kv, w_in, w_pool, pool_scale, conv_w, conv_b, w_ax, b_ax, nsp, norm_g, w_out, ln_g, ln_b,
      w_route, b_route)


def _ffn_kernel(tiles, bexp_ref, first_ref, nexte_ref, nused_ref,
                idx_cur_ref, idx_nxt_ref, xp_hbm, wg_hbm, wu_hbm, wd_hbm, ys_hbm,
                xbuf, obuf, stage_g, stage_u, stage_d, wg_ref, wu_ref, wd_ref, gsem, ssem, wsem):
    R = EXPERT_ROWS
    pitch = _vmem_pitch(tiles)
    step = pl.program_id(0)
    n_used = nused_ref[0]

    def weight_copies(e):
        return (pltpu.make_async_copy(wg_hbm.at[e], stage_g, wsem.at[0]),
                pltpu.make_async_copy(wu_hbm.at[e], stage_u, wsem.at[1]),
                pltpu.make_async_copy(wd_hbm.at[e], stage_d, wsem.at[2]))

    def gather_copy(src, i, sl):
        return pltpu.make_async_copy(xp_hbm.at[pl.ds(pl.multiple_of(src, tiles), tiles)],
                                     xbuf.at[sl, pl.ds(i * pitch, tiles)], gsem.at[sl])

    def scatter_copy(dst, i, sl):
        return pltpu.make_async_copy(obuf.at[sl, pl.ds(i * pitch, tiles)],
                                     ys_hbm.at[pl.ds(pl.multiple_of(dst, tiles), tiles)], ssem.at[sl])

    def wait_gathers(sl):
        pltpu.make_async_copy(xp_hbm.at[pl.ds(0, R * tiles)], xbuf.at[sl, pl.ds(0, R * tiles)], gsem.at[sl]).wait()

    def wait_scatters(sl):
        pltpu.make_async_copy(obuf.at[sl, pl.ds(0, R * tiles)], ys_hbm.at[pl.ds(0, R * tiles)], ssem.at[sl]).wait()

    @pl.when(step == 0)
    def _():
        for c in weight_copies(bexp_ref[0]):
            c.start(priority=WEIGHT_DMA_PRIORITY)
        for i in range(R):
            gather_copy(idx_cur_ref[0, 0, i], i, 0).start()
        obuf[1] = jnp.zeros(obuf.shape[1:], F32)
        spare = ys_hbm.shape[0] - 2 * R * tiles
        for k in range(2):
            pltpu.make_async_copy(obuf.at[1, pl.ds(0, R * tiles)],
                                  ys_hbm.at[pl.ds(spare + k * R * tiles, R * tiles)], ssem.at[1]).start()
        for k in range(2):
            wait_scatters(1)

    def run_block(r, cur, dst_off, nxt_ref, nxt_off):
        @pl.when(r + 1 < n_used)
        def _():
            for i in range(R):
                gather_copy(nxt_ref[0, 0, nxt_off + i], i, 1 - cur).start()

        @pl.when(first_ref[r] == 1)
        def _():
            for c in weight_copies(0):
                c.wait()
            n_chunks = 8
            kg = stage_g.shape[0] // n_chunks
            kd = stage_d.shape[0] // n_chunks

            def cast_chunk(j, carry):
                sg = pl.ds(pl.multiple_of(j * kg, kg), kg)
                sd = pl.ds(pl.multiple_of(j * kd, kd), kd)
                wg_ref[sg, :] = stage_g[sg, :].astype(BF16)
                wu_ref[sg, :] = stage_u[sg, :].astype(BF16)
                wd_ref[sd, :] = stage_d[sd, :].astype(BF16)
                return carry
            lax.fori_loop(0, n_chunks, cast_chunk, 0)

            @pl.when(nexte_ref[r] >= 0)
            def _():
                for c in weight_copies(nexte_ref[r]):
                    c.start(priority=WEIGHT_DMA_PRIORITY)

        wait_gathers(cur)

        @pl.when(r >= 2)
        def _():
            wait_scatters(cur)

        xb = _load_rows(xbuf.at[cur], R, tiles, pitch).astype(BF16)
        g = jnp.dot(xb, wg_ref[...], preferred_element_type=F32)
        u = jnp.dot(xb, wu_ref[...], preferred_element_type=F32)
        hdn = (g * _sigmoid(g)) * u
        y = jnp.dot(hdn.astype(BF16), wd_ref[...], preferred_element_type=F32)
        _store_rows(obuf.at[cur], y, pitch)

        for i in range(R):
            scatter_copy(idx_cur_ref[0, 0, dst_off + i], i, cur).start()

        @pl.when(r == n_used - 1)
        def _():
            wait_scatters(cur)

            @pl.when(r >= 1)
            def _():
                wait_scatters(1 - cur)

    for j in range(BLOCKS_PER_STEP):
        r = BLOCKS_PER_STEP * step + j
        nxt_ref, nxt_off = (idx_cur_ref, 2 * (j + 1) * R) if j + 1 < BLOCKS_PER_STEP else (idx_nxt_ref, 0)
        pl.when(r < n_used)(functools.partial(run_block, r, j % 2, (2 * j + 1) * R, nxt_ref, nxt_off))


def _expert_ffn(xp, idx, plan, w_gate, w_up, w_down, n_rows_out):
    R = EXPERT_ROWS
    per_step = 2 * R * BLOCKS_PER_STEP
    assert idx.shape[0] % BLOCKS_PER_STEP == 0
    idx = idx.reshape(idx.shape[0] // BLOCKS_PER_STEP, 1, per_step)
    nb = idx.shape[0]
    D, FF = w_gate.shape[1:]
    tiles = _row_tiles(D)
    pitch = _vmem_pitch(tiles)
    idx_block = (1, 1, per_step)
    grid_spec = pltpu.PrefetchScalarGridSpec(
        num_scalar_prefetch=4,
        grid=(nb,),
        in_specs=[
            pl.BlockSpec(idx_block, lambda r, *_: (r, 0, 0), memory_space=pltpu.SMEM),
            pl.BlockSpec(idx_block, lambda r, *_: (jnp.minimum(r + 1, nb - 1), 0, 0), memory_space=pltpu.SMEM),
            pl.BlockSpec(memory_space=pl.ANY),
            pl.BlockSpec(memory_space=pl.ANY),
            pl.BlockSpec(memory_space=pl.ANY),
            pl.BlockSpec(memory_space=pl.ANY),
        ],
        out_specs=pl.BlockSpec(memory_space=pl.ANY),
        scratch_shapes=[
            pltpu.VMEM((2, R * pitch, LANES), F32),
            pltpu.VMEM((2, R * pitch, LANES), F32),
            pltpu.VMEM((D, FF), F32),
            pltpu.VMEM((D, FF), F32),
            pltpu.VMEM((FF, D), F32),
            pltpu.VMEM((D, FF), BF16),
            pltpu.VMEM((D, FF), BF16),
            pltpu.VMEM((FF, D), BF16),
            pltpu.SemaphoreType.DMA((2,)),
            pltpu.SemaphoreType.DMA((2,)),
            pltpu.SemaphoreType.DMA((3,)),
        ],
    )
    return pl.pallas_call(
        functools.partial(_ffn_kernel, tiles),
        grid_spec=grid_spec,
        out_shape=jax.ShapeDtypeStruct((n_rows_out * tiles, LANES), F32),
        compiler_params=pltpu.CompilerParams(dimension_semantics=("arbitrary",),
                                             vmem_limit_bytes=VMEM_LIMIT),
        name="expert_ffn",
    )(*plan, idx, idx, xp, w_gate, w_up, w_down)


def _combine_kernel(alpha, tiles, x1_ref, y0_ref, y1_ref, route_ref, g_ref, b_ref, out_ref):
    T = x1_ref.shape[0]
    rt = route_ref[...]
    y0 = _load_rows(y0_ref, T, tiles, tiles)
    y1 = _load_rows(y1_ref, T, tiles, tiles)
    z = alpha * x1_ref[...] + rt[:, 4:5] * y0 + rt[:, 5:6] * y1
    mu = jnp.mean(z, axis=-1, keepdims=True)
    zc = z - mu
    var = jnp.mean(zc * zc, axis=-1, keepdims=True)
    out_ref[...] = zc * lax.rsqrt(var + EPS) * g_ref[...] + b_ref[...]


def _combine(x1, ys, route, ln_g, ln_b, alpha):
    N, D = x1.shape
    T = COMBINE_TILE
    assert N % T == 0
    off = N // T
    tiles = _row_tiles(D)
    return pl.pallas_call(
        functools.partial(_combine_kernel, alpha, tiles),
        grid=(N // T,),
        in_specs=[
            pl.BlockSpec((T, D), lambda i: (i, 0)),
            pl.BlockSpec((T * tiles, LANES), lambda i: (i, 0)),
            pl.BlockSpec((T * tiles, LANES), lambda i: (i + off, 0)),
            pl.BlockSpec((T, LANES), lambda i: (i, 0)),
            pl.BlockSpec((1, D), lambda i: (0, 0)),
            pl.BlockSpec((1, D), lambda i: (0, 0)),
        ],
        out_specs=pl.BlockSpec((T, D), lambda i: (i, 0)),
        out_shape=jax.ShapeDtypeStruct((N, D), F32),
        compiler_params=pltpu.CompilerParams(dimension_semantics=("arbitrary",),
                                             vmem_limit_bytes=VMEM_LIMIT),
        name="combine_ln",
    )(x1, ys, ys, route, ln_g, ln_b)


def _route_weights(w_group, w_fine):
    w = jnp.concatenate([w_group, w_fine], axis=1)
    wh = w.astype(BF16)
    wl = (w - wh.astype(F32)).astype(BF16)
    n = w.shape[1]
    out = jnp.zeros((w.shape[0], LANES), BF16)
    out = out.at[:, :n].set(wh)
    return out.at[:, ROUTE_LO:ROUTE_LO + n].set(wl)


def _layer(x, mem, w_in, w_pool, pool_scale, conv_w, conv_b, w_a, b_a, w_x, b_x, lam, w_mem_kv, mix_norm_g,
           w_out, ln1_g, ln1_b, w_group, b_group, w_fine, b_fine, w_gate, w_up, w_down, ln2_g, ln2_b, alpha):
    B, S, D = x.shape
    N = B * S
    R = EXPERT_ROWS
    row = lambda v: v.reshape(1, -1).astype(F32)

    kv = _kv_project(mem, w_mem_kv.astype(BF16))
    w_ax = jnp.concatenate([w_a, w_x], axis=2).astype(BF16)
    b_ax = jnp.concatenate([b_a, b_x]).reshape(1, -1)
    nsp = row(jax.nn.softplus(-lam.astype(F32)))
    b_route = jnp.zeros((1, LANES), F32).at[0, :N_GROUPS + N_EXPERTS].set(
        jnp.concatenate([b_group, b_fine.reshape(-1)]))
    x1, xp, route, route_t, counts = _mixer(
        x, kv, w_in.astype(BF16), w_pool.astype(BF16), row(pool_scale), conv_w, row(conv_b), w_ax, b_ax, nsp,
        row(mix_norm_g), w_out.astype(BF16), row(ln1_g), row(ln1_b), _route_weights(w_group, w_fine), b_route,
        alpha)
    x1 = x1.reshape(N, D)
    route = route.reshape(N, LANES)

    e_idx = route_t[:, 0:2, :].astype(jnp.int32)
    rank = route_t[:, 2:4, :].astype(jnp.int32)
    cnt = counts[0, :N_EXPERTS].astype(jnp.int32)
    padded = (cnt + R - 1) // R * R
    pad_ends = jnp.cumsum(padded)
    pad_starts = pad_ends - padded
    ex = jnp.arange(N_EXPERTS, dtype=jnp.int32)
    start_of = jnp.sum(jnp.where(e_idx[..., None] == ex, pad_starts, 0), axis=-1)
    dest = start_of + rank
    n_blocks = -(-(2 * N + N_EXPERTS * R) // (R * BLOCKS_PER_STEP)) * BLOCKS_PER_STEP
    P = n_blocks * R
    block_start = jnp.arange(n_blocks, dtype=jnp.int32) * R
    block_exp = jnp.minimum(jnp.sum((pad_ends[None, :] <= block_start[:, None]).astype(jnp.int32), axis=1),
                            N_EXPERTS - 1)
    tok_tile = route_t.shape[2]
    slot = (jnp.arange(N, dtype=jnp.int32).reshape(-1, 1, tok_tile)
            + jnp.array([0, N], jnp.int32).reshape(1, 2, 1))
    p = jnp.arange(P, dtype=jnp.int32)
    spare = 2 * N + ((p // R) % 2) * R + (p % R)
    buf_slot = spare.at[dest.reshape(-1)].set(slot.reshape(-1), unique_indices=True)
    buf_tok = jnp.where(buf_slot < 2 * N, buf_slot % N, 0)
    tiles = _row_tiles(D)
    idx = jnp.concatenate([buf_tok.reshape(n_blocks, 1, R), buf_slot.reshape(n_blocks, 1, R)], axis=2) * tiles

    n_used = (pad_ends[-1:] // R).astype(jnp.int32)
    blk = jnp.arange(n_blocks, dtype=jnp.int32)
    first = ((blk < n_used) & ((blk == 0) | (block_exp != jnp.roll(block_exp, 1)))).astype(jnp.int32)
    later_active = (padded[None, :] > 0) & (ex[None, :] > ex[:, None])
    next_active = jnp.min(jnp.where(later_active, ex[None, :], N_EXPERTS), axis=1)
    next_active = jnp.where(next_active == N_EXPERTS, -1, next_active).astype(jnp.int32)
    plan = (block_exp.astype(jnp.int32), first, next_active[block_exp], n_used)

    ys = _expert_ffn(xp, idx, plan, w_gate, w_up, w_down, 2 * N + 2 * R)
    out = _combine(x1, ys, route, row(ln2_g), row(ln2_b), alpha)
    return out.reshape(B, S, D)


def kernel(x, mem, w_in, w_pool, pool_scale, conv_w, conv_b, w_a, b_a, w_x, b_x, lam, w_mem_kv, mix_norm_g, w_out,
           ln1_g, ln1_b, w_group, b_group, w_fine, b_fine, w_gate, w_up, w_down, ln2_g, ln2_b):
    depth = w_in.shape[0]
    alpha = (2 * depth) ** 0.25
    for l in range(depth):
        x = _layer(x, mem, w_in[l], w_pool[l], pool_scale[l], conv_w[l], conv_b[l], w_a[l], b_a[l], w_x[l], b_x[l],
                   lam[l], w_mem_kv[l], mix_norm_g[l], w_out[l], ln1_g[l], ln1_b[l], w_group[l], b_group[l],
                   w_fine[l], b_fine[l], w_gate[l], w_up[l], w_down[l], ln2_g[l], ln2_b[l], alpha)
    return x
```

```python
import functools

import jax
import jax.numpy as jnp
from jax import lax
from jax.experimental import pallas as pl
from jax.experimental.pallas import tpu as pltpu

F32 = jnp.float32
BF16 = jnp.bfloat16

POOL_WINDOWS = (2, 4, 8, 16)
LRU_C = 8.0
CONV_WIDTH = 4
N_GROUPS = 4
EXPERTS_PER_GROUP = 8
N_EXPERTS = N_GROUPS * EXPERTS_PER_GROUP
EPS = 1e-5

LANES = 128
HALO = 16
SEQ_TILE = 256
PROJ_CHUNK = 256
EXPERT_ROWS = 128
BLOCKS_PER_STEP = 4
GATHER_AHEAD = 2
GATHER_SLOTS = GATHER_AHEAD + 1
WEIGHT_DMA_PRIORITY = 1
COMBINE_TILE = 512
ROUTE_LO = 64
NEG = -1e30
VMEM_LIMIT = 56 * 1024 * 1024


def _sigmoid(v):
    return 1.0 / (1.0 + jnp.exp(-v))


def _gelu_tanh(v):
    return 0.5 * v * (1.0 + jnp.tanh(0.7978845608028654 * (v + 0.044715 * (v * v * v))))


def _shift_rows(v, k):
    return pltpu.roll(v, k, 0)


def _linear_scan(a, b, h0):
    n = a.shape[0]
    groups = n // 8
    a3 = a.reshape(groups, 8, LANES)
    b3 = b.reshape(groups, 8, LANES)
    sub = lax.broadcasted_iota(jnp.int32, (groups, 8, LANES), 1)
    for k in (1, 2, 4):
        keep = sub >= k
        b_prev = jnp.where(keep, pltpu.roll(b3, k, 1), 0.0)
        a_prev = jnp.where(keep, pltpu.roll(a3, k, 1), 1.0)
        b3 = b3 + a3 * b_prev
        a3 = a3 * a_prev
    carry = h0
    out = []
    for g in range(groups):
        hg = a3[g] * carry + b3[g]
        out.append(hg)
        carry = hg[7:8]
    return jnp.concatenate(out, axis=0)


def _row_tiles(d):
    tiles, rem = divmod(d, LANES)
    assert rem == 0 and tiles % 8 == 0, "a row must be whole (8, 128) tiles to be one contiguous DMA"
    return tiles


def _vmem_pitch(tiles):
    return tiles if (tiles // 8) % 2 else tiles + 8


def _store_rows(ref, v, pitch):
    n, d = v.shape
    for c in range(d // LANES):
        ref[pl.ds(c, n, stride=pitch), :] = v[:, c * LANES:(c + 1) * LANES]


def _load_rows(ref, n, tiles, pitch):
    return jnp.concatenate([ref[pl.ds(c, n, stride=pitch), :] for c in range(tiles)], axis=1)


def _kv_kernel(mem_ref, w_ref, kv_ref):
    kv_ref[0] = jnp.dot(mem_ref[0].astype(BF16), w_ref[...], preferred_element_type=F32).astype(BF16)


def _kv_project(mem, w_kv):
    B, M, D = mem.shape
    E = w_kv.shape[1]
    return pl.pallas_call(
        _kv_kernel,
        grid=(B,),
        in_specs=[pl.BlockSpec((1, M, D), lambda b: (b, 0, 0)),
                  pl.BlockSpec((D, E), lambda b: (0, 0))],
        out_specs=pl.BlockSpec((1, M, E), lambda b: (b, 0, 0)),
        out_shape=jax.ShapeDtypeStruct((B, M, E), BF16),
        compiler_params=pltpu.CompilerParams(dimension_semantics=("arbitrary",),
                                             vmem_limit_bytes=VMEM_LIMIT),
        name="kv_project",
    )(mem, w_kv)


def _finish_tile(alpha, T, g_step, xr_ref, y_ref, ln_g_ref, ln_b_ref, w_route_ref, b_route_ref, cnt_ref,
                 x1_ref, xp_ref, route_ref, route_t_ref, counts_ref):
    z = alpha * xr_ref[0] + y_ref[...]
    mu = jnp.mean(z, axis=-1, keepdims=True)
    zc = z - mu
    var = jnp.mean(zc * zc, axis=-1, keepdims=True)
    x1 = zc * lax.rsqrt(var + EPS) * ln_g_ref[...] + ln_b_ref[...]
    x1_ref[0] = x1

    xh = x1.astype(BF16)
    xl = (x1 - xh.astype(F32)).astype(BF16)
    _store_rows(xp_ref, x1, x1.shape[1] // LANES)
    lg = jnp.dot(jnp.concatenate([xh, xl], axis=0), w_route_ref[...], preferred_element_type=F32)
    top = lg[:T]
    logits = top + lg[T:] + pltpu.roll(top, LANES - ROUTE_LO, 1) + b_route_ref[...]
    lane_i = lax.broadcasted_iota(jnp.int32, (T, LANES), 1)
    lane = lane_i.astype(F32)
    lane_grp = ((lane_i - N_GROUPS) >> 3).astype(F32)

    is_g = lane_i < N_GROUPS
    gl = jnp.where(is_g, logits, NEG)
    gmax = jnp.max(gl, axis=-1, keepdims=True)
    gsum = jnp.sum(jnp.where(is_g, jnp.exp(gl - gmax), 0.0), axis=-1, keepdims=True)
    g_p = 1.0 / gsum
    g_idx = jnp.min(jnp.where(gl == gmax, lane, float(LANES)), axis=-1, keepdims=True)

    in_grp = (lane_i >= N_GROUPS) & (lane_i < N_GROUPS + N_EXPERTS) & (lane_grp == g_idx)
    fl = jnp.where(in_grp, logits, NEG)
    l1 = jnp.max(fl, axis=-1, keepdims=True)
    i1 = jnp.min(jnp.where(fl == l1, lane, float(LANES)), axis=-1, keepdims=True)
    fl2 = jnp.where(lane == i1, NEG, fl)
    l2 = jnp.max(fl2, axis=-1, keepdims=True)
    i2 = jnp.min(jnp.where(fl2 == l2, lane, float(LANES)), axis=-1, keepdims=True)
    e21 = jnp.exp(l2 - l1)
    w1 = g_p / (1.0 + e21)
    w2 = g_p * e21 / (1.0 + e21)
    e1 = i1 - N_GROUPS
    e2 = i2 - N_GROUPS

    oh1 = (lane == e1).astype(F32)
    oh2 = (lane == e2).astype(F32)
    both = (oh1 + oh2) * (g_step > 0).astype(F32)
    tri = (lax.broadcasted_iota(jnp.int32, (T, T), 0) > lax.broadcasted_iota(jnp.int32, (T, T), 1))
    pfx = jnp.dot(tri.astype(F32).astype(BF16), both.astype(BF16), preferred_element_type=F32)
    base = pfx + cnt_ref[...]
    r1 = jnp.sum(oh1 * base, axis=-1, keepdims=True)
    r2 = jnp.sum(oh2 * base, axis=-1, keepdims=True)
    cnt_ref[...] = cnt_ref[...] + jnp.sum(both, axis=0, keepdims=True)
    counts_ref[...] = jnp.broadcast_to(cnt_ref[...], counts_ref.shape)

    route = jnp.where(lane_i == 0, e1,
            jnp.where(lane_i == 1, e2,
            jnp.where(lane_i == 2, r1,
            jnp.where(lane_i == 3, r2,
            jnp.where(lane_i == 4, w1,
            jnp.where(lane_i == 5, w2, 0.0))))))
    route_ref[0] = route
    route_t_ref[0] = route.T[:8]


def _mixer_kernel(dims, alpha, n_s,
                  xr_ref, xn_ref, kv_ref, w_in_ref, w_pool_ref, pool_scale_ref, conv_w_ref, conv_b_ref,
                  w_ax_ref, b_ax_ref, nsp_ref, norm_g_ref, w_out_ref, ln_g_ref, ln_b_ref,
                  w_route_ref, b_route_ref,
                  x1_ref, xp_ref, route_ref, route_t_ref, counts_ref,
                  h_ref, hn_ref, mix_ref, mixb_ref, mixn_ref, y_ref, halo_ref, state_ref, cnt_ref):
    T, pool_w, lru_w, mem_w, n_heads, head_dim, gdim, bdim = dims
    g_step = pl.program_id(0)
    s = lax.rem(g_step, n_s)
    o_lru = pool_w
    o_gate = o_lru + lru_w
    o_q = o_gate + lru_w

    @pl.when(s == 0)
    def _():
        halo_ref[...] = jnp.zeros_like(halo_ref)
        state_ref[...] = jnp.zeros_like(state_ref)

    @pl.when(g_step == 0)
    def _():
        cnt_ref[...] = jnp.zeros_like(cnt_ref)
        hn_ref[...] = jnp.dot(xr_ref[0].astype(BF16), w_in_ref[...], preferred_element_type=F32)
        mixn_ref[...] = jnp.zeros_like(mixn_ref)

    h_ref[...] = hn_ref[...]
    mixb_ref[...] = mixn_ref[...]
    xnb = xn_ref[0].astype(BF16)
    n_proj = w_in_ref.shape[1] // PROJ_CHUNK
    n_out = w_out_ref.shape[1] // PROJ_CHUNK
    done = {"proj": 0, "out": 0}

    def fill_mxu(n_in, n_o):
        for j in range(done["proj"], min(done["proj"] + n_in, n_proj)):
            cols = slice(j * PROJ_CHUNK, (j + 1) * PROJ_CHUNK)
            hn_ref[:, cols] = jnp.dot(xnb, w_in_ref[:, cols], preferred_element_type=F32)
        for j in range(done["out"], min(done["out"] + n_o, n_out)):
            cols = slice(j * PROJ_CHUNK, (j + 1) * PROJ_CHUNK)
            y_ref[:, cols] = jnp.dot(mixb_ref[...], w_out_ref[:, cols], preferred_element_type=F32)
        done["proj"] = min(done["proj"] + n_in, n_proj)
        done["out"] = min(done["out"] + n_o, n_out)

    row = lax.broadcasted_iota(jnp.int32, (T, 1), 0)
    pos = (s * T + row + 1).astype(F32)

    ext = jnp.concatenate([halo_ref[:, :pool_w], h_ref[:, :pool_w]], axis=0)
    ssq = jnp.zeros((T, LANES), F32)
    win = ext
    span = 1
    for g, w in enumerate(POOL_WINDOWS):
        while span < w:
            win = win + _shift_rows(win, span)
            span *= 2
        sw = win[HALO:, :gdim]
        if g + 1 < len(POOL_WINDOWS):
            win = win[:, gdim:]
        u = h_ref[:, g * gdim:(g + 1) * gdim]
        d = sw / jnp.minimum(pos, float(w)) - u
        y = jnp.dot(d.astype(BF16), w_pool_ref[g], preferred_element_type=F32)
        y = y * pool_scale_ref[:, g * gdim:(g + 1) * gdim]
        ssq = ssq + y * y
        mix_ref[:, g * gdim:(g + 1) * gdim] = y
    inv = lax.rsqrt(jnp.sum(ssq, axis=-1, keepdims=True) / pool_w + EPS)
    mixn_ref[:, :pool_w] = (mix_ref[:, :pool_w] * inv * norm_g_ref[:, :pool_w]).astype(BF16)
    fill_mxu(2, 0)

    ssq = jnp.zeros((T, LANES), F32)
    for hh in range(lru_w // bdim):
        c0 = hh * bdim
        ext = jnp.concatenate([halo_ref[:, o_lru + c0:o_lru + c0 + bdim],
                               h_ref[:, o_lru + c0:o_lru + c0 + bdim]], axis=0)
        cw = conv_w_ref[:, c0:c0 + bdim]
        uc = ext * cw[CONV_WIDTH - 1:CONV_WIDTH]
        for k in range(1, CONV_WIDTH):
            uc = uc + _shift_rows(ext, k) * cw[CONV_WIDTH - 1 - k:CONV_WIDTH - k]
        uc = uc[HALO:] + conv_b_ref[:, c0:c0 + bdim]
        ga = jnp.dot(uc.astype(BF16), w_ax_ref[hh], preferred_element_type=F32)
        r = _sigmoid(ga[:, :bdim] + b_ax_ref[:, c0:c0 + bdim])
        i = _sigmoid(ga[:, bdim:] + b_ax_ref[:, lru_w + c0:lru_w + c0 + bdim])
        log_a = -LRU_C * r * nsp_ref[:, c0:c0 + bdim]
        a = jnp.exp(log_a)
        mult = jnp.sqrt(jnp.maximum(1.0 - a * a, 0.0))
        bt = mult * i * uc
        hs = _linear_scan(a, bt, state_ref[:, c0:c0 + bdim])
        state_ref[:, c0:c0 + bdim] = hs[T - 1:T]
        y = _gelu_tanh(h_ref[:, o_gate + c0:o_gate + c0 + bdim]) * hs
        ssq = ssq + y * y
        mix_ref[:, o_lru + c0:o_lru + c0 + bdim] = y
        fill_mxu(1, 1)
    inv = lax.rsqrt(jnp.sum(ssq, axis=-1, keepdims=True) / lru_w + EPS)
    mixn_ref[:, o_lru:o_gate] = (mix_ref[:, o_lru:o_gate] * inv * norm_g_ref[:, o_lru:o_gate]).astype(BF16)

    halo_ref[...] = h_ref[T - HALO:, :o_gate]

    ssq = jnp.zeros((T, LANES), F32)
    m0 = o_gate
    for hh in range(n_heads):
        c0 = hh * head_dim
        q = h_ref[:, o_q + c0:o_q + c0 + head_dim].astype(BF16)
        kh = kv_ref[0, :, c0:c0 + head_dim]
        vh = kv_ref[0, :, mem_w + c0:mem_w + c0 + head_dim]
        sc = lax.dot_general(q, kh, (((1,), (1,)), ((), ())), preferred_element_type=F32) * (head_dim ** -0.5)
        p = jnp.exp(sc - jnp.max(sc, axis=-1, keepdims=True))
        p = p / jnp.sum(p, axis=-1, keepdims=True)
        y = jnp.dot(p.astype(BF16), vh, preferred_element_type=F32)
        ssq = ssq + y * y
        mix_ref[:, m0 + c0:m0 + c0 + head_dim] = y
        fill_mxu(1, 1)
    inv = lax.rsqrt(jnp.sum(ssq, axis=-1, keepdims=True) / mem_w + EPS)
    mixn_ref[:, m0:] = (mix_ref[:, m0:] * inv * norm_g_ref[:, m0:]).astype(BF16)

    fill_mxu(n_proj, n_out)
    _finish_tile(alpha, T, g_step, xr_ref, y_ref, ln_g_ref, ln_b_ref, w_route_ref, b_route_ref, cnt_ref,
                 x1_ref, xp_ref, route_ref, route_t_ref, counts_ref)


def _mixer(x, kv, w_in, w_pool, pool_scale, conv_w, conv_b, w_ax, b_ax, nsp, norm_g, w_out, ln_g, ln_b,
           w_route, b_route, alpha):
    B, S, D = x.shape
    T = SEQ_TILE
    in_w = w_in.shape[1]
    pool_w = pool_scale.shape[1]
    lru_w = conv_b.shape[1]
    mem_w = kv.shape[2] // 2
    gdim = w_pool.shape[1]
    bdim = w_ax.shape[1]
    n_heads = 4
    head_dim = mem_w // n_heads
    mix_w = w_out.shape[0]
    dims = (T, pool_w, lru_w, mem_w, n_heads, head_dim, gdim, bdim)
    assert S % T == 0 and T % 8 == 0
    assert gdim == bdim == head_dim == LANES and in_w % PROJ_CHUNK == 0

    const2 = lambda g: (0, 0)
    const3 = lambda g: (0, 0, 0)
    single = pl.Buffered(1)

    def full(a):
        return pl.BlockSpec(a.shape, const2 if a.ndim == 2 else const3, pipeline_mode=single)

    n_s = S // T
    n_tiles = B * n_s

    def tile_index(t):
        return (t // n_s, t % n_s, 0)

    prev_tile = lambda g: jnp.maximum(g - 1, 0)
    in_specs = [
        pl.BlockSpec((1, T, D), lambda g: tile_index(prev_tile(g))),
        pl.BlockSpec((1, T, D), lambda g: tile_index(jnp.minimum(g + 1, n_tiles - 1))),
        pl.BlockSpec((1,) + kv.shape[1:], lambda g: (jnp.minimum(g // n_s, B - 1), 0, 0)),
        full(w_in), full(w_pool), full(pool_scale), full(conv_w), full(conv_b), full(w_ax), full(b_ax),
        full(nsp), full(norm_g), full(w_out), full(ln_g), full(ln_b), full(w_route), full(b_route),
    ]
    tiles = _row_tiles(D)
    out_specs = [
        pl.BlockSpec((1, T, D), lambda g: tile_index(prev_tile(g))),
        pl.BlockSpec((T * tiles, LANES), lambda g: (prev_tile(g), 0)),
        pl.BlockSpec((1, T, LANES), lambda g: tile_index(prev_tile(g))),
        pl.BlockSpec((1, 8, T), lambda g: (prev_tile(g), 0, 0)),
        pl.BlockSpec((8, LANES), const2),
    ]
    out_shape = [
        jax.ShapeDtypeStruct((B, S, D), F32),
        jax.ShapeDtypeStruct((B * S * tiles, LANES), F32),
        jax.ShapeDtypeStruct((B, S, LANES), F32),
        jax.ShapeDtypeStruct((n_tiles, 8, T), F32),
        jax.ShapeDtypeStruct((8, LANES), F32),
    ]
    scratch = [
        pltpu.VMEM((T, in_w), F32),
        pltpu.VMEM((T, in_w), F32),
        pltpu.VMEM((T, mix_w), F32),
        pltpu.VMEM((T, mix_w), BF16),
        pltpu.VMEM((T, mix_w), BF16),
        pltpu.VMEM((T, D), F32),
        pltpu.VMEM((HALO, pool_w + lru_w), F32),
        pltpu.VMEM((1, lru_w), F32),
        pltpu.VMEM((1, LANES), F32),
    ]
    return pl.pallas_call(
        functools.partial(_mixer_kernel, dims, alpha, n_s),
        grid=(n_tiles + 1,),
        in_specs=in_specs,
        out_specs=out_specs,
        out_shape=out_shape,
        scratch_shapes=scratch,
        compiler_params=pltpu.CompilerParams(dimension_semantics=("arbitrary",),
                                             vmem_limit_bytes=VMEM_LIMIT),
        name="mixer_router",
    )(x, x, kv, w_in, w_pool, pool_scale, conv_w, conv_b, w_ax, b_ax, nsp, norm_g, w_out, ln_g, ln_b,
      w_route, b_route)


def _ffn_kernel(tiles, bexp_ref, first_ref, nexte_ref, nused_ref,
                idx_cur_ref, idx_nxt_ref, xp_hbm, wg_hbm, wu_hbm, wd_hbm, ys_hbm,
                xbuf, obuf, stage_g, stage_u, stage_d, wg_ref, wu_ref, wd_ref, gsem, ssem, wsem):
    R = EXPERT_ROWS
    pitch = _vmem_pitch(tiles)
    step = pl.program_id(0)
    n_used = nused_ref[0]

    def weight_copies(e):
        return (pltpu.make_async_copy(wg_hbm.at[e], stage_g, wsem.at[0]),
                pltpu.make_async_copy(wu_hbm.at[e], stage_u, wsem.at[1]),
                pltpu.make_async_copy(wd_hbm.at[e], stage_d, wsem.at[2]))

    def gather_copy(src, i, sl):
        return pltpu.make_async_copy(xp_hbm.at[pl.ds(pl.multiple_of(src, tiles), tiles)],
                                     xbuf.at[sl, pl.ds(i * pitch, tiles)], gsem.at[sl])

    def scatter_copy(dst, i, sl):
        return pltpu.make_async_copy(obuf.at[sl, pl.ds(i * pitch, tiles)],
                                     ys_hbm.at[pl.ds(pl.multiple_of(dst, tiles), tiles)], ssem.at[sl])

    def wait_gathers(sl):
        pltpu.make_async_copy(xp_hbm.at[pl.ds(0, R * tiles)], xbuf.at[sl, pl.ds(0, R * tiles)], gsem.at[sl]).wait()

    def wait_scatters(sl):
        pltpu.make_async_copy(obuf.at[sl, pl.ds(0, R * tiles)], ys_hbm.at[pl.ds(0, R * tiles)], ssem.at[sl]).wait()

    @pl.when(step == 0)
    def _():
        for c in weight_copies(bexp_ref[0]):
            c.start(priority=WEIGHT_DMA_PRIORITY)
        for k in range(GATHER_AHEAD):
            @pl.when(k < n_used)
            def _(k=k):
                for i in range(R):
                    gather_copy(idx_cur_ref[0, 0, 2 * k * R + i], i, k).start()
        obuf[1] = jnp.zeros(obuf.shape[1:], F32)
        spare = ys_hbm.shape[0] - 2 * R * tiles
        for k in range(2):
            pltpu.make_async_copy(obuf.at[1, pl.ds(0, R * tiles)],
                                  ys_hbm.at[pl.ds(spare + k * R * tiles, R * tiles)], ssem.at[1]).start()
        for k in range(2):
            wait_scatters(1)

    def run_block(r, cur, dst_off, ahead_ref, ahead_off):
        gslot = lax.rem(r, GATHER_SLOTS)

        @pl.when(r + GATHER_AHEAD < n_used)
        def _():
            aslot = lax.rem(r + GATHER_AHEAD, GATHER_SLOTS)
            for i in range(R):
                gather_copy(ahead_ref[0, 0, ahead_off + i], i, aslot).start()

        @pl.when(first_ref[r] == 1)
        def _():
            for c in weight_copies(0):
                c.wait()
            n_chunks = 8
            kg = stage_g.shape[0] // n_chunks
            kd = stage_d.shape[0] // n_chunks

            def cast_chunk(j, carry):
                sg = pl.ds(pl.multiple_of(j * kg, kg), kg)
                sd = pl.ds(pl.multiple_of(j * kd, kd), kd)
                wg_ref[sg, :] = stage_g[sg, :].astype(BF16)
                wu_ref[sg, :] = stage_u[sg, :].astype(BF16)
                wd_ref[sd, :] = stage_d[sd, :].astype(BF16)
                return carry
            lax.fori_loop(0, n_chunks, cast_chunk, 0)

            @pl.when(nexte_ref[r] >= 0)
            def _():
                for c in weight_copies(nexte_ref[r]):
                    c.start(priority=WEIGHT_DMA_PRIORITY)

        wait_gathers(gslot)

        @pl.when(r >= 2)
        def _():
            wait_scatters(cur)

        xb = _load_rows(xbuf.at[gslot], R, tiles, pitch).astype(BF16)
        g = jnp.dot(xb, wg_ref[...], preferred_element_type=F32)
        u = jnp.dot(xb, wu_ref[...], preferred_element_type=F32)
        hdn = (g * _sigmoid(g)) * u
        y = jnp.dot(hdn.astype(BF16), wd_ref[...], preferred_element_type=F32)
        _store_rows(obuf.at[cur], y, pitch)

        for i in range(R):
            scatter_copy(idx_cur_ref[0, 0, dst_off + i], i, cur).start()

        @pl.when(r == n_used - 1)
        def _():
            wait_scatters(cur)

            @pl.when(r >= 1)
            def _():
                wait_scatters(1 - cur)

    for j in range(BLOCKS_PER_STEP):
        r = BLOCKS_PER_STEP * step + j
        ja = j + GATHER_AHEAD
        ahead = (idx_cur_ref, 2 * ja * R) if ja < BLOCKS_PER_STEP else (idx_nxt_ref, 2 * (ja - BLOCKS_PER_STEP) * R)
        pl.when(r < n_used)(functools.partial(run_block, r, j % 2, (2 * j + 1) * R, *ahead))


def _expert_ffn(xp, idx, plan, w_gate, w_up, w_down, n_rows_out):
    R = EXPERT_ROWS
    per_step = 2 * R * BLOCKS_PER_STEP
    assert idx.shape[0] % BLOCKS_PER_STEP == 0 and BLOCKS_PER_STEP % 2 == 0 and GATHER_AHEAD <= BLOCKS_PER_STEP
    idx = idx.reshape(idx.shape[0] // BLOCKS_PER_STEP, 1, per_step)
    nb = idx.shape[0]
    D, FF = w_gate.shape[1:]
    tiles = _row_tiles(D)
    pitch = _vmem_pitch(tiles)
    idx_block = (1, 1, per_step)
    grid_spec = pltpu.PrefetchScalarGridSpec(
        num_scalar_prefetch=4,
        grid=(nb,),
        in_specs=[
            pl.BlockSpec(idx_block, lambda r, *_: (r, 0, 0), memory_space=pltpu.SMEM),
            pl.BlockSpec(idx_block, lambda r, *_: (jnp.minimum(r + 1, nb - 1), 0, 0), memory_space=pltpu.SMEM),
            pl.BlockSpec(memory_space=pl.ANY),
            pl.BlockSpec(memory_space=pl.ANY),
            pl.BlockSpec(memory_space=pl.ANY),
            pl.BlockSpec(memory_space=pl.ANY),
        ],
        out_specs=pl.BlockSpec(memory_space=pl.ANY),
        scratch_shapes=[
            pltpu.VMEM((GATHER_SLOTS, R * pitch, LANES), F32),
            pltpu.VMEM((2, R * pitch, LANES), F32),
            pltpu.VMEM((D, FF), F32),
            pltpu.VMEM((D, FF), F32),
            pltpu.VMEM((FF, D), F32),
            pltpu.VMEM((D, FF), BF16),
            pltpu.VMEM((D, FF), BF16),
            pltpu.VMEM((FF, D), BF16),
            pltpu.SemaphoreType.DMA((GATHER_SLOTS,)),
            pltpu.SemaphoreType.DMA((2,)),
            pltpu.SemaphoreType.DMA((3,)),
        ],
    )
    return pl.pallas_call(
        functools.partial(_ffn_kernel, tiles),
        grid_spec=grid_spec,
        out_shape=jax.ShapeDtypeStruct((n_rows_out * tiles, LANES), F32),
        compiler_params=pltpu.CompilerParams(dimension_semantics=("arbitrary",),
                                             vmem_limit_bytes=VMEM_LIMIT),
        name="expert_ffn",
    )(*plan, idx, idx, xp, w_gate, w_up, w_down)


def _combine_kernel(alpha, tiles, x1_ref, y0_ref, y1_ref, route_ref, g_ref, b_ref, out_ref):
    T = x1_ref.shape[0]
    rt = route_ref[...]
    y0 = _load_rows(y0_ref, T, tiles, tiles)
    y1 = _load_rows(y1_ref, T, tiles, tiles)
    z = alpha * x1_ref[...] + rt[:, 4:5] * y0 + rt[:, 5:6] * y1
    mu = jnp.mean(z, axis=-1, keepdims=True)
    zc = z - mu
    var = jnp.mean(zc * zc, axis=-1, keepdims=True)
    out_ref[...] = zc * lax.rsqrt(var + EPS) * g_ref[...] + b_ref[...]


def _combine(x1, ys, route, ln_g, ln_b, alpha):
    N, D = x1.shape
    T = COMBINE_TILE
    assert N % T == 0
    off = N // T
    tiles = _row_tiles(D)
    return pl.pallas_call(
        functools.partial(_combine_kernel, alpha, tiles),
        grid=(N // T,),
        in_specs=[
            pl.BlockSpec((T, D), lambda i: (i, 0)),
            pl.BlockSpec((T * tiles, LANES), lambda i: (i, 0)),
            pl.BlockSpec((T * tiles, LANES), lambda i: (i + off, 0)),
            pl.BlockSpec((T, LANES), lambda i: (i, 0)),
            pl.BlockSpec((1, D), lambda i: (0, 0)),
            pl.BlockSpec((1, D), lambda i: (0, 0)),
        ],
        out_specs=pl.BlockSpec((T, D), lambda i: (i, 0)),
        out_shape=jax.ShapeDtypeStruct((N, D), F32),
        compiler_params=pltpu.CompilerParams(dimension_semantics=("arbitrary",),
                                             vmem_limit_bytes=VMEM_LIMIT),
        name="combine_ln",
    )(x1, ys, ys, route, ln_g, ln_b)


def _route_weights(w_group, w_fine):
    w = jnp.concatenate([w_group, w_fine], axis=1)
    wh = w.astype(BF16)
    wl = (w - wh.astype(F32)).astype(BF16)
    n = w.shape[1]
    out = jnp.zeros((w.shape[0], LANES), BF16)
    out = out.at[:, :n].set(wh)
    return out.at[:, ROUTE_LO:ROUTE_LO + n].set(wl)


def _layer(x, mem, w_in, w_pool, pool_scale, conv_w, conv_b, w_a, b_a, w_x, b_x, lam, w_mem_kv, mix_norm_g,
           w_out, ln1_g, ln1_b, w_group, b_group, w_fine, b_fine, w_gate, w_up, w_down, ln2_g, ln2_b, alpha):
    B, S, D = x.shape
    N = B * S
    R = EXPERT_ROWS
    row = lambda v: v.reshape(1, -1).astype(F32)

    kv = _kv_project(mem, w_mem_kv.astype(BF16))
    w_ax = jnp.concatenate([w_a, w_x], axis=2).astype(BF16)
    b_ax = jnp.concatenate([b_a, b_x]).reshape(1, -1)
    nsp = row(jax.nn.softplus(-lam.astype(F32)))
    b_route = jnp.zeros((1, LANES), F32).at[0, :N_GROUPS + N_EXPERTS].set(
        jnp.concatenate([b_group, b_fine.reshape(-1)]))
    x1, xp, route, route_t, counts = _mixer(
        x, kv, w_in.astype(BF16), w_pool.astype(BF16), row(pool_scale), conv_w, row(conv_b), w_ax, b_ax, nsp,
        row(mix_norm_g), w_out.astype(BF16), row(ln1_g), row(ln1_b), _route_weights(w_group, w_fine), b_route,
        alpha)
    x1 = x1.reshape(N, D)
    route = route.reshape(N, LANES)

    e_idx = route_t[:, 0:2, :].astype(jnp.int32)
    rank = route_t[:, 2:4, :].astype(jnp.int32)
    cnt = counts[0, :N_EXPERTS].astype(jnp.int32)
    padded = (cnt + R - 1) // R * R
    pad_ends = jnp.cumsum(padded)
    pad_starts = pad_ends - padded
    ex = jnp.arange(N_EXPERTS, dtype=jnp.int32)
    start_of = jnp.sum(jnp.where(e_idx[..., None] == ex, pad_starts, 0), axis=-1)
    dest = start_of + rank
    n_blocks = -(-(2 * N + N_EXPERTS * R) // (R * BLOCKS_PER_STEP)) * BLOCKS_PER_STEP
    P = n_blocks * R
    block_start = jnp.arange(n_blocks, dtype=jnp.int32) * R
    block_exp = jnp.minimum(jnp.sum((pad_ends[None, :] <= block_start[:, None]).astype(jnp.int32), axis=1),
                            N_EXPERTS - 1)
    tok_tile = route_t.shape[2]
    slot = (jnp.arange(N, dtype=jnp.int32).reshape(-1, 1, tok_tile)
            + jnp.array([0, N], jnp.int32).reshape(1, 2, 1))
    p = jnp.arange(P, dtype=jnp.int32)
    spare = 2 * N + ((p // R) % 2) * R + (p % R)
    buf_slot = spare.at[dest.reshape(-1)].set(slot.reshape(-1), unique_indices=True)
    buf_tok = jnp.where(buf_slot < 2 * N, buf_slot % N, 0)
    tiles = _row_tiles(D)
    idx = jnp.concatenate([buf_tok.reshape(n_blocks, 1, R), buf_slot.reshape(n_blocks, 1, R)], axis=2) * tiles

    n_used = (pad_ends[-1:] // R).astype(jnp.int32)
    blk = jnp.arange(n_blocks, dtype=jnp.int32)
    first = ((blk < n_used) & ((blk == 0) | (block_exp != jnp.roll(block_exp, 1)))).astype(jnp.int32)
    later_active = (padded[None, :] > 0) & (ex[None, :] > ex[:, None])
    next_active = jnp.min(jnp.where(later_active, ex[None, :], N_EXPERTS), axis=1)
    next_active = jnp.where(next_active == N_EXPERTS, -1, next_active).astype(jnp.int32)
    plan = (block_exp.astype(jnp.int32), first, next_active[block_exp], n_used)

    ys = _expert_ffn(xp, idx, plan, w_gate, w_up, w_down, 2 * N + 2 * R)
    out = _combine(x1, ys, route, row(ln2_g), row(ln2_b), alpha)
    return out.reshape(B, S, D)


def kernel(x, mem, w_in, w_pool, pool_scale, conv_w, conv_b, w_a, b_a, w_x, b_x, lam, w_mem_kv, mix_norm_g, w_out,
           ln1_g, ln1_b, w_group, b_group, w_fine, b_fine, w_gate, w_up, w_down, ln2_g, ln2_b):
    depth = w_in.shape[0]
    alpha = (2 * depth) ** 0.25
    for l in range(depth):
        x = _layer(x, mem, w_in[l], w_pool[l], pool_scale[l], conv_w[l], conv_b[l], w_a[l], b_a[l], w_x[l], b_x[l],
                   lam[l], w_mem_kv[l], mix_norm_g[l], w_out[l], ln1_g[l], ln1_b[l], w_group[l], b_group[l],
                   w_fine[l], b_fine[l], w_gate[l], w_up[l], w_down[l], ln2_g[l], ln2_b[l], alpha)
    return x
```

```python
import functools

import jax
import jax.numpy as jnp
from jax import lax
from jax.experimental import pallas as pl
from jax.experimental.pallas import tpu as pltpu

F32 = jnp.float32
BF16 = jnp.bfloat16

POOL_WINDOWS = (2, 4, 8, 16)
LRU_C = 8.0
CONV_WIDTH = 4
N_GROUPS = 4
EXPERTS_PER_GROUP = 8
N_EXPERTS = N_GROUPS * EXPERTS_PER_GROUP
EPS = 1e-5

LANES = 128
HALO = 16
SEQ_TILE = 256
PROJ_CHUNK = 256
EXPERT_ROWS = 128
BLOCKS_PER_STEP = 4
GATHER_AHEAD = 2
GATHER_SLOTS = GATHER_AHEAD + 1
WEIGHT_DMA_PRIORITY = 1
COMBINE_TILE = 512
ROUTE_LO = 64
NEG = -1e30
VMEM_LIMIT = 56 * 1024 * 1024


def _sigmoid(v):
    return 1.0 / (1.0 + jnp.exp(-v))


def _gelu_tanh(v):
    return 0.5 * v * (1.0 + jnp.tanh(0.7978845608028654 * (v + 0.044715 * (v * v * v))))


def _shift_rows(v, k):
    return pltpu.roll(v, k, 0)


def _linear_scan(a, b, h0):
    n = a.shape[0]
    groups = n // 8
    a3 = a.reshape(groups, 8, LANES)
    b3 = b.reshape(groups, 8, LANES)
    sub = lax.broadcasted_iota(jnp.int32, (groups, 8, LANES), 1)
    for k in (1, 2, 4):
        keep = sub >= k
        b_prev = jnp.where(keep, pltpu.roll(b3, k, 1), 0.0)
        a_prev = jnp.where(keep, pltpu.roll(a3, k, 1), 1.0)
        b3 = b3 + a3 * b_prev
        a3 = a3 * a_prev
    carry = h0
    out = []
    for g in range(groups):
        hg = a3[g] * carry + b3[g]
        out.append(hg)
        carry = hg[7:8]
    return jnp.concatenate(out, axis=0)


def _row_tiles(d):
    tiles, rem = divmod(d, LANES)
    assert rem == 0 and tiles % 8 == 0, "a row must be whole (8, 128) tiles to be one contiguous DMA"
    return tiles


def _vmem_pitch(tiles):
    return tiles if (tiles // 8) % 2 else tiles + 8


def _store_rows(ref, v, pitch):
    n, d = v.shape
    for c in range(d // LANES):
        ref[pl.ds(c, n, stride=pitch), :] = v[:, c * LANES:(c + 1) * LANES]


def _load_rows(ref, n, tiles, pitch):
    return jnp.concatenate([ref[pl.ds(c, n, stride=pitch), :] for c in range(tiles)], axis=1)


def _kv_kernel(mem_ref, w_ref, kv_ref):
    kv_ref[0] = jnp.dot(mem_ref[0].astype(BF16), w_ref[...], preferred_element_type=F32).astype(BF16)


def _kv_project(mem, w_kv):
    B, M, D = mem.shape
    E = w_kv.shape[1]
    return pl.pallas_call(
        _kv_kernel,
        grid=(B,),
        in_specs=[pl.BlockSpec((1, M, D), lambda b: (b, 0, 0)),
                  pl.BlockSpec((D, E), lambda b: (0, 0))],
        out_specs=pl.BlockSpec((1, M, E), lambda b: (b, 0, 0)),
        out_shape=jax.ShapeDtypeStruct((B, M, E), BF16),
        compiler_params=pltpu.CompilerParams(dimension_semantics=("arbitrary",),
                                             vmem_limit_bytes=VMEM_LIMIT),
        name="kv_project",
    )(mem, w_kv)


def _finish_tile(alpha, T, g_step, xr_ref, y_ref, ln_g_ref, ln_b_ref, w_route_ref, b_route_ref, cnt_ref,
                 x1_ref, xp_ref, route_ref, route_t_ref, counts_ref):
    z = alpha * xr_ref[0] + y_ref[...]
    mu = jnp.mean(z, axis=-1, keepdims=True)
    zc = z - mu
    var = jnp.mean(zc * zc, axis=-1, keepdims=True)
    x1 = zc * lax.rsqrt(var + EPS) * ln_g_ref[...] + ln_b_ref[...]
    x1_ref[0] = x1

    xh = x1.astype(BF16)
    xl = (x1 - xh.astype(F32)).astype(BF16)
    _store_rows(xp_ref, x1, x1.shape[1] // LANES)
    lg = jnp.dot(jnp.concatenate([xh, xl], axis=0), w_route_ref[...], preferred_element_type=F32)
    top = lg[:T]
    logits = top + lg[T:] + pltpu.roll(top, LANES - ROUTE_LO, 1) + b_route_ref[...]
    lane_i = lax.broadcasted_iota(jnp.int32, (T, LANES), 1)
    lane = lane_i.astype(F32)
    lane_grp = ((lane_i - N_GROUPS) >> 3).astype(F32)

    is_g = lane_i < N_GROUPS
    gl = jnp.where(is_g, logits, NEG)
    gmax = jnp.max(gl, axis=-1, keepdims=True)
    gsum = jnp.sum(jnp.where(is_g, jnp.exp(gl - gmax), 0.0), axis=-1, keepdims=True)
    g_p = 1.0 / gsum
    g_idx = jnp.min(jnp.where(gl == gmax, lane, float(LANES)), axis=-1, keepdims=True)

    in_grp = (lane_i >= N_GROUPS) & (lane_i < N_GROUPS + N_EXPERTS) & (lane_grp == g_idx)
    fl = jnp.where(in_grp, logits, NEG)
    l1 = jnp.max(fl, axis=-1, keepdims=True)
    i1 = jnp.min(jnp.where(fl == l1, lane, float(LANES)), axis=-1, keepdims=True)
    fl2 = jnp.where(lane == i1, NEG, fl)
    l2 = jnp.max(fl2, axis=-1, keepdims=True)
    i2 = jnp.min(jnp.where(fl2 == l2, lane, float(LANES)), axis=-1, keepdims=True)
    e21 = jnp.exp(l2 - l1)
    w1 = g_p / (1.0 + e21)
    w2 = g_p * e21 / (1.0 + e21)
    e1 = i1 - N_GROUPS
    e2 = i2 - N_GROUPS

    oh1 = (lane == e1).astype(F32)
    oh2 = (lane == e2).astype(F32)
    both = (oh1 + oh2) * (g_step > 0).astype(F32)
    tri = (lax.broadcasted_iota(jnp.int32, (T, T), 0) > lax.broadcasted_iota(jnp.int32, (T, T), 1))
    pfx = jnp.dot(tri.astype(F32).astype(BF16), both.astype(BF16), preferred_element_type=F32)
    base = pfx + cnt_ref[...]
    r1 = jnp.sum(oh1 * base, axis=-1, keepdims=True)
    r2 = jnp.sum(oh2 * base, axis=-1, keepdims=True)
    cnt_ref[...] = cnt_ref[...] + jnp.sum(both, axis=0, keepdims=True)
    counts_ref[...] = jnp.broadcast_to(cnt_ref[...], counts_ref.shape)

    route = jnp.where(lane_i == 0, e1,
            jnp.where(lane_i == 1, e2,
            jnp.where(lane_i == 2, r1,
            jnp.where(lane_i == 3, r2,
            jnp.where(lane_i == 4, w1,
            jnp.where(lane_i == 5, w2, 0.0))))))
    route_ref[0] = route
    route_t_ref[0] = route.T[:8]


def _mixer_kernel(dims, alpha, n_s,
                  xr_ref, xn_ref, kv_ref, w_in_ref, w_pool_ref, pool_scale_ref, conv_w_ref, conv_b_ref,
                  w_ax_ref, b_ax_ref, nsp_ref, norm_g_ref, w_out_ref, ln_g_ref, ln_b_ref,
                  w_route_ref, b_route_ref,
                  x1_ref, xp_ref, route_ref, route_t_ref, counts_ref,
                  h_ref, hn_ref, mix_ref, mixb_ref, mixn_ref, y_ref, halo_ref, state_ref, cnt_ref):
    T, pool_w, lru_w, mem_w, n_heads, head_dim, gdim, bdim = dims
    g_step = pl.program_id(0)
    s = lax.rem(g_step, n_s)
    o_lru = pool_w
    o_gate = o_lru + lru_w
    o_q = o_gate + lru_w

    @pl.when(s == 0)
    def _():
        halo_ref[...] = jnp.zeros_like(halo_ref)
        state_ref[...] = jnp.zeros_like(state_ref)

    @pl.when(g_step == 0)
    def _():
        cnt_ref[...] = jnp.zeros_like(cnt_ref)
        hn_ref[...] = jnp.dot(xr_ref[0].astype(BF16), w_in_ref[...], preferred_element_type=F32)
        mixn_ref[...] = jnp.zeros_like(mixn_ref)

    h_ref[...] = hn_ref[...]
    mixb_ref[...] = mixn_ref[...]
    xnb = xn_ref[0].astype(BF16)
    n_proj = w_in_ref.shape[1] // PROJ_CHUNK
    n_out = w_out_ref.shape[1] // PROJ_CHUNK
    done = {"proj": 0, "out": 0}

    def fill_mxu(n_in, n_o):
        for j in range(done["proj"], min(done["proj"] + n_in, n_proj)):
            cols = slice(j * PROJ_CHUNK, (j + 1) * PROJ_CHUNK)
            hn_ref[:, cols] = jnp.dot(xnb, w_in_ref[:, cols], preferred_element_type=F32)
        for j in range(done["out"], min(done["out"] + n_o, n_out)):
            cols = slice(j * PROJ_CHUNK, (j + 1) * PROJ_CHUNK)
            y_ref[:, cols] = jnp.dot(mixb_ref[...], w_out_ref[:, cols], preferred_element_type=F32)
        done["proj"] = min(done["proj"] + n_in, n_proj)
        done["out"] = min(done["out"] + n_o, n_out)

    row = lax.broadcasted_iota(jnp.int32, (T, 1), 0)
    pos = (s * T + row + 1).astype(F32)

    ext = jnp.concatenate([halo_ref[:, :pool_w], h_ref[:, :pool_w]], axis=0)
    ssq = jnp.zeros((T, LANES), F32)
    win = ext
    span = 1
    for g, w in enumerate(POOL_WINDOWS):
        while span < w:
            win = win + _shift_rows(win, span)
            span *= 2
        sw = win[HALO:, :gdim]
        if g + 1 < len(POOL_WINDOWS):
            win = win[:, gdim:]
        u = h_ref[:, g * gdim:(g + 1) * gdim]
        d = sw / jnp.minimum(pos, float(w)) - u
        y = jnp.dot(d.astype(BF16), w_pool_ref[g], preferred_element_type=F32)
        y = y * pool_scale_ref[:, g * gdim:(g + 1) * gdim]
        ssq = ssq + y * y
        mix_ref[:, g * gdim:(g + 1) * gdim] = y
    inv = lax.rsqrt(jnp.sum(ssq, axis=-1, keepdims=True) / pool_w + EPS)
    mixn_ref[:, :pool_w] = (mix_ref[:, :pool_w] * inv * norm_g_ref[:, :pool_w]).astype(BF16)
    fill_mxu(2, 0)

    ssq = jnp.zeros((T, LANES), F32)
    for hh in range(lru_w // bdim):
        c0 = hh * bdim
        ext = jnp.concatenate([halo_ref[:, o_lru + c0:o_lru + c0 + bdim],
                               h_ref[:, o_lru + c0:o_lru + c0 + bdim]], axis=0)
        cw = conv_w_ref[:, c0:c0 + bdim]
        uc = ext * cw[CONV_WIDTH - 1:CONV_WIDTH]
        for k in range(1, CONV_WIDTH):
            uc = uc + _shift_rows(ext, k) * cw[CONV_WIDTH - 1 - k:CONV_WIDTH - k]
        uc = uc[HALO:] + conv_b_ref[:, c0:c0 + bdim]
        ga = jnp.dot(uc.astype(BF16), w_ax_ref[hh], preferred_element_type=F32)
        r = _sigmoid(ga[:, :bdim] + b_ax_ref[:, c0:c0 + bdim])
        i = _sigmoid(ga[:, bdim:] + b_ax_ref[:, lru_w + c0:lru_w + c0 + bdim])
        log_a = -LRU_C * r * nsp_ref[:, c0:c0 + bdim]
        a = jnp.exp(log_a)
        mult = jnp.sqrt(jnp.maximum(1.0 - a * a, 0.0))
        bt = mult * i * uc
        hs = _linear_scan(a, bt, state_ref[:, c0:c0 + bdim])
        state_ref[:, c0:c0 + bdim] = hs[T - 1:T]
        y = _gelu_tanh(h_ref[:, o_gate + c0:o_gate + c0 + bdim]) * hs
        ssq = ssq + y * y
        mix_ref[:, o_lru + c0:o_lru + c0 + bdim] = y
        fill_mxu(1, 1)
    inv = lax.rsqrt(jnp.sum(ssq, axis=-1, keepdims=True) / lru_w + EPS)
    mixn_ref[:, o_lru:o_gate] = (mix_ref[:, o_lru:o_gate] * inv * norm_g_ref[:, o_lru:o_gate]).astype(BF16)

    halo_ref[...] = h_ref[T - HALO:, :o_gate]

    ssq = jnp.zeros((T, LANES), F32)
    m0 = o_gate
    for hh in range(n_heads):
        c0 = hh * head_dim
        q = h_ref[:, o_q + c0:o_q + c0 + head_dim].astype(BF16)
        kh = kv_ref[0, :, c0:c0 + head_dim]
        vh = kv_ref[0, :, mem_w + c0:mem_w + c0 + head_dim]
        sc = lax.dot_general(q, kh, (((1,), (1,)), ((), ())), preferred_element_type=F32) * (head_dim ** -0.5)
        p = jnp.exp(sc - jnp.max(sc, axis=-1, keepdims=True))
        p = p / jnp.sum(p, axis=-1, keepdims=True)
        y = jnp.dot(p.astype(BF16), vh, preferred_element_type=F32)
        ssq = ssq + y * y
        mix_ref[:, m0 + c0:m0 + c0 + head_dim] = y
        fill_mxu(1, 1)
    inv = lax.rsqrt(jnp.sum(ssq, axis=-1, keepdims=True) / mem_w + EPS)
    mixn_ref[:, m0:] = (mix_ref[:, m0:] * inv * norm_g_ref[:, m0:]).astype(BF16)

    fill_mxu(n_proj, n_out)
    _finish_tile(alpha, T, g_step, xr_ref, y_ref, ln_g_ref, ln_b_ref, w_route_ref, b_route_ref, cnt_ref,
                 x1_ref, xp_ref, route_ref, route_t_ref, counts_ref)


def _mixer(x, kv, w_in, w_pool, pool_scale, conv_w, conv_b, w_ax, b_ax, nsp, norm_g, w_out, ln_g, ln_b,
           w_route, b_route, alpha):
    B, S, D = x.shape
    T = SEQ_TILE
    in_w = w_in.shape[1]
    pool_w = pool_scale.shape[1]
    lru_w = conv_b.shape[1]
    mem_w = kv.shape[2] // 2
    gdim = w_pool.shape[1]
    bdim = w_ax.shape[1]
    n_heads = 4
    head_dim = mem_w // n_heads
    mix_w = w_out.shape[0]
    dims = (T, pool_w, lru_w, mem_w, n_heads, head_dim, gdim, bdim)
    assert S % T == 0 and T % 8 == 0
    assert gdim == bdim == head_dim == LANES and in_w % PROJ_CHUNK == 0

    const2 = lambda g: (0, 0)
    const3 = lambda g: (0, 0, 0)
    single = pl.Buffered(1)

    def full(a):
        return pl.BlockSpec(a.shape, const2 if a.ndim == 2 else const3, pipeline_mode=single)

    n_s = S // T
    n_tiles = B * n_s

    def tile_index(t):
        return (t // n_s, t % n_s, 0)

    prev_tile = lambda g: jnp.maximum(g - 1, 0)
    in_specs = [
        pl.BlockSpec((1, T, D), lambda g: tile_index(prev_tile(g))),
        pl.BlockSpec((1, T, D), lambda g: tile_index(jnp.minimum(g + 1, n_tiles - 1))),
        pl.BlockSpec((1,) + kv.shape[1:], lambda g: (jnp.minimum(g // n_s, B - 1), 0, 0)),
        full(w_in), full(w_pool), full(pool_scale), full(conv_w), full(conv_b), full(w_ax), full(b_ax),
        full(nsp), full(norm_g), full(w_out), full(ln_g), full(ln_b), full(w_route), full(b_route),
    ]
    tiles = _row_tiles(D)
    out_specs = [
        pl.BlockSpec((1, T, D), lambda g: tile_index(prev_tile(g))),
        pl.BlockSpec((T * tiles, LANES), lambda g: (prev_tile(g), 0)),
        pl.BlockSpec((1, T, LANES), lambda g: tile_index(prev_tile(g))),
        pl.BlockSpec((1, 8, T), lambda g: (prev_tile(g), 0, 0)),
        pl.BlockSpec((8, LANES), const2),
    ]
    out_shape = [
        jax.ShapeDtypeStruct((B, S, D), F32),
        jax.ShapeDtypeStruct((B * S * tiles, LANES), F32),
        jax.ShapeDtypeStruct((B, S, LANES), F32),
        jax.ShapeDtypeStruct((n_tiles, 8, T), F32),
        jax.ShapeDtypeStruct((8, LANES), F32),
    ]
    scratch = [
        pltpu.VMEM((T, in_w), F32),
        pltpu.VMEM((T, in_w), F32),
        pltpu.VMEM((T, mix_w), F32),
        pltpu.VMEM((T, mix_w), BF16),
        pltpu.VMEM((T, mix_w), BF16),
        pltpu.VMEM((T, D), F32),
        pltpu.VMEM((HALO, pool_w + lru_w), F32),
        pltpu.VMEM((1, lru_w), F32),
        pltpu.VMEM((1, LANES), F32),
    ]
    return pl.pallas_call(
        functools.partial(_mixer_kernel, dims, alpha, n_s),
        grid=(n_tiles + 1,),
        in_specs=in_specs,
        out_specs=out_specs,
        out_shape=out_shape,
        scratch_shapes=scratch,
        compiler_params=pltpu.CompilerParams(dimension_semantics=("arbitrary",),
                                             vmem_limit_bytes=VMEM_LIMIT),
        name="mixer_router",
    )(x, x, kv, w_in, w_pool, pool_scale, conv_w, conv_b, w_ax, b_ax, nsp, norm_g, w_out, ln_g, ln_b,
      w_route, b_route)


def _ffn_kernel(tiles, bexp_ref, first_ref, nexte_ref, nused_ref,
                idx_prv_ref, idx_cur_ref, idx_nxt_ref, xp_hbm, wg_hbm, wu_hbm, wd_hbm, ys_hbm,
                xbuf, obuf, stage_g, stage_u, stage_d, wg_ref, wu_ref, wd_ref, gsem, ssem, wsem):
    R = EXPERT_ROWS
    pitch = _vmem_pitch(tiles)
    step = pl.program_id(0)
    n_used = nused_ref[0]

    def weight_copies(e):
        return (pltpu.make_async_copy(wg_hbm.at[e], stage_g, wsem.at[0]),
                pltpu.make_async_copy(wu_hbm.at[e], stage_u, wsem.at[1]),
                pltpu.make_async_copy(wd_hbm.at[e], stage_d, wsem.at[2]))

    def gather_copy(src, i, sl):
        return pltpu.make_async_copy(xp_hbm.at[pl.ds(pl.multiple_of(src, tiles), tiles)],
                                     xbuf.at[sl, pl.ds(i * pitch, tiles)], gsem.at[sl])

    def scatter_copy(dst, i, sl):
        return pltpu.make_async_copy(obuf.at[sl, pl.ds(i * pitch, tiles)],
                                     ys_hbm.at[pl.ds(pl.multiple_of(dst, tiles), tiles)], ssem.at[sl])

    def wait_gathers(sl):
        pltpu.make_async_copy(xp_hbm.at[pl.ds(0, R * tiles)], xbuf.at[sl, pl.ds(0, R * tiles)], gsem.at[sl]).wait()

    def wait_scatters(sl):
        pltpu.make_async_copy(obuf.at[sl, pl.ds(0, R * tiles)], ys_hbm.at[pl.ds(0, R * tiles)], ssem.at[sl]).wait()

    @pl.when(step == 0)
    def _():
        for c in weight_copies(bexp_ref[0]):
            c.start(priority=WEIGHT_DMA_PRIORITY)
        for k in range(GATHER_AHEAD):
            for i in range(R):
                gather_copy(idx_cur_ref[0, 0, 2 * k * R + i], i, k).start()
        obuf[1] = jnp.zeros(obuf.shape[1:], F32)
        spare = ys_hbm.shape[0] - 2 * R * tiles
        for k in range(2):
            pltpu.make_async_copy(obuf.at[1, pl.ds(0, R * tiles)],
                                  ys_hbm.at[pl.ds(spare + k * R * tiles, R * tiles)], ssem.at[1]).start()
        for k in range(2):
            wait_scatters(1)

    def run_block(r, cur, dst_off, ahead_ref, ahead_off, prev_ref, prev_off):
        gslot = lax.rem(r, GATHER_SLOTS)

        @pl.when(first_ref[r] == 1)
        def _():
            for c in weight_copies(0):
                c.wait()
            n_chunks = 8
            kg = stage_g.shape[0] // n_chunks
            kd = stage_d.shape[0] // n_chunks

            def cast_chunk(j, carry):
                sg = pl.ds(pl.multiple_of(j * kg, kg), kg)
                sd = pl.ds(pl.multiple_of(j * kd, kd), kd)
                wg_ref[sg, :] = stage_g[sg, :].astype(BF16)
                wu_ref[sg, :] = stage_u[sg, :].astype(BF16)
                wd_ref[sd, :] = stage_d[sd, :].astype(BF16)
                return carry
            lax.fori_loop(0, n_chunks, cast_chunk, 0)

            @pl.when(nexte_ref[r] >= 0)
            def _():
                for c in weight_copies(nexte_ref[r]):
                    c.start(priority=WEIGHT_DMA_PRIORITY)

        wait_gathers(gslot)
        xb = _load_rows(xbuf.at[gslot], R, tiles, pitch).astype(BF16)

        aslot = lax.rem(r + GATHER_AHEAD, GATHER_SLOTS)
        for i in range(R):
            gather_copy(ahead_ref[0, 0, ahead_off + i], i, aslot).start()
        for i in range(R):
            scatter_copy(prev_ref[0, 0, prev_off + i], i, 1 - cur).start()

        g = jnp.dot(xb, wg_ref[...], preferred_element_type=F32)
        u = jnp.dot(xb, wu_ref[...], preferred_element_type=F32)

        @pl.when(r >= 1)
        def _():
            wait_scatters(cur)

        hdn = (g * _sigmoid(g)) * u
        y = jnp.dot(hdn.astype(BF16), wd_ref[...], preferred_element_type=F32)
        _store_rows(obuf.at[cur], y, pitch)

        @pl.when(r == n_used - 1)
        def _():
            for i in range(R):
                scatter_copy(idx_cur_ref[0, 0, dst_off + i], i, cur).start()
            wait_scatters(cur)
            wait_scatters(1 - cur)
            for k in range(1, GATHER_AHEAD + 1):
                wait_gathers(lax.rem(r + k, GATHER_SLOTS))

    for j in range(BLOCKS_PER_STEP):
        r = BLOCKS_PER_STEP * step + j
        ja = j + GATHER_AHEAD
        ahead = (idx_cur_ref, 2 * ja * R) if ja < BLOCKS_PER_STEP else (idx_nxt_ref, 2 * (ja - BLOCKS_PER_STEP) * R)
        prev = (idx_cur_ref, (2 * j - 1) * R) if j > 0 else (idx_prv_ref, (2 * BLOCKS_PER_STEP - 1) * R)
        pl.when(r < n_used)(functools.partial(run_block, r, j % 2, (2 * j + 1) * R, *ahead, *prev))


def _expert_ffn(xp, idx, plan, w_gate, w_up, w_down, n_rows_out):
    R = EXPERT_ROWS
    per_step = 2 * R * BLOCKS_PER_STEP
    assert idx.shape[0] % BLOCKS_PER_STEP == 0 and BLOCKS_PER_STEP % 2 == 0 and GATHER_AHEAD <= BLOCKS_PER_STEP
    idx = idx.reshape(idx.shape[0] // BLOCKS_PER_STEP, 1, per_step)
    nb = idx.shape[0]
    D, FF = w_gate.shape[1:]
    tiles = _row_tiles(D)
    pitch = _vmem_pitch(tiles)
    idx_block = (1, 1, per_step)
    grid_spec = pltpu.PrefetchScalarGridSpec(
        num_scalar_prefetch=4,
        grid=(nb,),
        in_specs=[
            pl.BlockSpec(idx_block, lambda r, *_: (jnp.maximum(r - 1, 0), 0, 0), memory_space=pltpu.SMEM),
            pl.BlockSpec(idx_block, lambda r, *_: (r, 0, 0), memory_space=pltpu.SMEM),
            pl.BlockSpec(idx_block, lambda r, *_: (jnp.minimum(r + 1, nb - 1), 0, 0), memory_space=pltpu.SMEM),
            pl.BlockSpec(memory_space=pl.ANY),
            pl.BlockSpec(memory_space=pl.ANY),
            pl.BlockSpec(memory_space=pl.ANY),
            pl.BlockSpec(memory_space=pl.ANY),
        ],
        out_specs=pl.BlockSpec(memory_space=pl.ANY),
        scratch_shapes=[
            pltpu.VMEM((GATHER_SLOTS, R * pitch, LANES), F32),
            pltpu.VMEM((2, R * pitch, LANES), F32),
            pltpu.VMEM((D, FF), F32),
            pltpu.VMEM((D, FF), F32),
            pltpu.VMEM((FF, D), F32),
            pltpu.VMEM((D, FF), BF16),
            pltpu.VMEM((D, FF), BF16),
            pltpu.VMEM((FF, D), BF16),
            pltpu.SemaphoreType.DMA((GATHER_SLOTS,)),
            pltpu.SemaphoreType.DMA((2,)),
            pltpu.SemaphoreType.DMA((3,)),
        ],
    )
    return pl.pallas_call(
        functools.partial(_ffn_kernel, tiles),
        grid_spec=grid_spec,
        out_shape=jax.ShapeDtypeStruct((n_rows_out * tiles, LANES), F32),
        compiler_params=pltpu.CompilerParams(dimension_semantics=("arbitrary",),
                                             vmem_limit_bytes=VMEM_LIMIT),
        name="expert_ffn",
    )(*plan, idx, idx, idx, xp, w_gate, w_up, w_down)


def _combine_kernel(alpha, tiles, x1_ref, y0_ref, y1_ref, route_ref, g_ref, b_ref, out_ref):
    T = x1_ref.shape[0]
    rt = route_ref[...]
    y0 = _load_rows(y0_ref, T, tiles, tiles)
    y1 = _load_rows(y1_ref, T, tiles, tiles)
    z = alpha * x1_ref[...] + rt[:, 4:5] * y0 + rt[:, 5:6] * y1
    mu = jnp.mean(z, axis=-1, keepdims=True)
    zc = z - mu
    var = jnp.mean(zc * zc, axis=-1, keepdims=True)
    out_ref[...] = zc * lax.rsqrt(var + EPS) * g_ref[...] + b_ref[...]


def _combine(x1, ys, route, ln_g, ln_b, alpha):
    N, D = x1.shape
    T = COMBINE_TILE
    assert N % T == 0
    off = N // T
    tiles = _row_tiles(D)
    return pl.pallas_call(
        functools.partial(_combine_kernel, alpha, tiles),
        grid=(N // T,),
        in_specs=[
            pl.BlockSpec((T, D), lambda i: (i, 0)),
            pl.BlockSpec((T * tiles, LANES), lambda i: (i, 0)),
            pl.BlockSpec((T * tiles, LANES), lambda i: (i + off, 0)),
            pl.BlockSpec((T, LANES), lambda i: (i, 0)),
            pl.BlockSpec((1, D), lambda i: (0, 0)),
            pl.BlockSpec((1, D), lambda i: (0, 0)),
        ],
        out_specs=pl.BlockSpec((T, D), lambda i: (i, 0)),
        out_shape=jax.ShapeDtypeStruct((N, D), F32),
        compiler_params=pltpu.CompilerParams(dimension_semantics=("arbitrary",),
                                             vmem_limit_bytes=VMEM_LIMIT),
        name="combine_ln",
    )(x1, ys, ys, route, ln_g, ln_b)


def _route_weights(w_group, w_fine):
    w = jnp.concatenate([w_group, w_fine], axis=1)
    wh = w.astype(BF16)
    wl = (w - wh.astype(F32)).astype(BF16)
    n = w.shape[1]
    out = jnp.zeros((w.shape[0], LANES), BF16)
    out = out.at[:, :n].set(wh)
    return out.at[:, ROUTE_LO:ROUTE_LO + n].set(wl)


def _layer(x, mem, w_in, w_pool, pool_scale, conv_w, conv_b, w_a, b_a, w_x, b_x, lam, w_mem_kv, mix_norm_g,
           w_out, ln1_g, ln1_b, w_group, b_group, w_fine, b_fine, w_gate, w_up, w_down, ln2_g, ln2_b, alpha):
    B, S, D = x.shape
    N = B * S
    R = EXPERT_ROWS
    row = lambda v: v.reshape(1, -1).astype(F32)

    kv = _kv_project(mem, w_mem_kv.astype(BF16))
    w_ax = jnp.concatenate([w_a, w_x], axis=2).astype(BF16)
    b_ax = jnp.concatenate([b_a, b_x]).reshape(1, -1)
    nsp = row(jax.nn.softplus(-lam.astype(F32)))
    b_route = jnp.zeros((1, LANES), F32).at[0, :N_GROUPS + N_EXPERTS].set(
        jnp.concatenate([b_group, b_fine.reshape(-1)]))
    x1, xp, route, route_t, counts = _mixer(
        x, kv, w_in.astype(BF16), w_pool.astype(BF16), row(pool_scale), conv_w, row(conv_b), w_ax, b_ax, nsp,
        row(mix_norm_g), w_out.astype(BF16), row(ln1_g), row(ln1_b), _route_weights(w_group, w_fine), b_route,
        alpha)
    x1 = x1.reshape(N, D)
    route = route.reshape(N, LANES)

    e_idx = route_t[:, 0:2, :].astype(jnp.int32)
    rank = route_t[:, 2:4, :].astype(jnp.int32)
    cnt = counts[0, :N_EXPERTS].astype(jnp.int32)
    padded = (cnt + R - 1) // R * R
    pad_ends = jnp.cumsum(padded)
    pad_starts = pad_ends - padded
    ex = jnp.arange(N_EXPERTS, dtype=jnp.int32)
    start_of = jnp.sum(jnp.where(e_idx[..., None] == ex, pad_starts, 0), axis=-1)
    dest = start_of + rank
    n_blocks = -(-(2 * N + N_EXPERTS * R) // (R * BLOCKS_PER_STEP)) * BLOCKS_PER_STEP
    P = n_blocks * R
    block_start = jnp.arange(n_blocks, dtype=jnp.int32) * R
    block_exp = jnp.minimum(jnp.sum((pad_ends[None, :] <= block_start[:, None]).astype(jnp.int32), axis=1),
                            N_EXPERTS - 1)
    tok_tile = route_t.shape[2]
    slot = (jnp.arange(N, dtype=jnp.int32).reshape(-1, 1, tok_tile)
            + jnp.array([0, N], jnp.int32).reshape(1, 2, 1))
    p = jnp.arange(P, dtype=jnp.int32)
    spare = 2 * N + ((p // R) % 2) * R + (p % R)
    buf_slot = spare.at[dest.reshape(-1)].set(slot.reshape(-1), unique_indices=True)
    buf_tok = jnp.where(buf_slot < 2 * N, buf_slot % N, 0)
    tiles = _row_tiles(D)
    idx = jnp.concatenate([buf_tok.reshape(n_blocks, 1, R), buf_slot.reshape(n_blocks, 1, R)], axis=2) * tiles

    n_used = (pad_ends[-1:] // R).astype(jnp.int32)
    blk = jnp.arange(n_blocks, dtype=jnp.int32)
    first = ((blk < n_used) & ((blk == 0) | (block_exp != jnp.roll(block_exp, 1)))).astype(jnp.int32)
    later_active = (padded[None, :] > 0) & (ex[None, :] > ex[:, None])
    next_active = jnp.min(jnp.where(later_active, ex[None, :], N_EXPERTS), axis=1)
    next_active = jnp.where(next_active == N_EXPERTS, -1, next_active).astype(jnp.int32)
    plan = (block_exp.astype(jnp.int32), first, next_active[block_exp], n_used)

    ys = _expert_ffn(xp, idx, plan, w_gate, w_up, w_down, 2 * N + 2 * R)
    out = _combine(x1, ys, route, row(ln2_g), row(ln2_b), alpha)
    return out.reshape(B, S, D)


def kernel(x, mem, w_in, w_pool, pool_scale, conv_w, conv_b, w_a, b_a, w_x, b_x, lam, w_mem_kv, mix_norm_g, w_out,
           ln1_g, ln1_b, w_group, b_group, w_fine, b_fine, w_gate, w_up, w_down, ln2_g, ln2_b):
    depth = w_in.shape[0]
    alpha = (2 * depth) ** 0.25
    for l in range(depth):
        x = _layer(x, mem, w_in[l], w_pool[l], pool_scale[l], conv_w[l], conv_b[l], w_a[l], b_a[l], w_x[l], b_x[l],
                   lam[l], w_mem_kv[l], mix_norm_g[l], w_out[l], ln1_g[l], ln1_b[l], w_group[l], b_group[l],
                   w_fine[l], b_fine[l], w_gate[l], w_up[l], w_down[l], ln2_g[l], ln2_b[l], alpha)
    return x
```

```python
import functools

import jax
import jax.numpy as jnp
from jax import lax
from jax.experimental import pallas as pl
from jax.experimental.pallas import tpu as pltpu

F32 = jnp.float32
BF16 = jnp.bfloat16

POOL_WINDOWS = (2, 4, 8, 16)
LRU_C = 8.0
CONV_WIDTH = 4
N_GROUPS = 4
EXPERTS_PER_GROUP = 8
N_EXPERTS = N_GROUPS * EXPERTS_PER_GROUP
EPS = 1e-5

LANES = 128
HALO = 16
SEQ_TILE = 256
PROJ_CHUNK = 256
EXPERT_ROWS = 128
BLOCKS_PER_STEP = 4
GATHER_AHEAD = 4
GATHER_SLOTS = GATHER_AHEAD + 1
WEIGHT_DMA_PRIORITY = 1
COMBINE_TILE = 512
ROUTE_LO = 64
NEG = -1e30
VMEM_LIMIT = 56 * 1024 * 1024


def _sigmoid(v):
    return 1.0 / (1.0 + jnp.exp(-v))


def _gelu_tanh(v):
    return 0.5 * v * (1.0 + jnp.tanh(0.7978845608028654 * (v + 0.044715 * (v * v * v))))


def _shift_rows(v, k):
    return pltpu.roll(v, k, 0)


def _linear_scan(a, b, h0):
    n = a.shape[0]
    groups = n // 8
    a3 = a.reshape(groups, 8, LANES)
    b3 = b.reshape(groups, 8, LANES)
    sub = lax.broadcasted_iota(jnp.int32, (groups, 8, LANES), 1)
    for k in (1, 2, 4):
        keep = sub >= k
        b_prev = jnp.where(keep, pltpu.roll(b3, k, 1), 0.0)
        a_prev = jnp.where(keep, pltpu.roll(a3, k, 1), 1.0)
        b3 = b3 + a3 * b_prev
        a3 = a3 * a_prev
    carry = h0
    out = []
    for g in range(groups):
        hg = a3[g] * carry + b3[g]
        out.append(hg)
        carry = hg[7:8]
    return jnp.concatenate(out, axis=0)


def _row_tiles(d):
    tiles, rem = divmod(d, LANES)
    assert rem == 0 and tiles % 8 == 0, "a row must be whole (8, 128) tiles to be one contiguous DMA"
    return tiles


def _vmem_pitch(tiles):
    return tiles if (tiles // 8) % 2 else tiles + 8


def _store_rows(ref, v, pitch):
    n, d = v.shape
    for c in range(d // LANES):
        ref[pl.ds(c, n, stride=pitch), :] = v[:, c * LANES:(c + 1) * LANES]


def _load_rows(ref, n, tiles, pitch):
    return jnp.concatenate([ref[pl.ds(c, n, stride=pitch), :] for c in range(tiles)], axis=1)


def _kv_kernel(mem_ref, w_ref, kv_ref):
    kv_ref[0] = jnp.dot(mem_ref[0].astype(BF16), w_ref[...], preferred_element_type=F32).astype(BF16)


def _kv_project(mem, w_kv):
    B, M, D = mem.shape
    E = w_kv.shape[1]
    return pl.pallas_call(
        _kv_kernel,
        grid=(B,),
        in_specs=[pl.BlockSpec((1, M, D), lambda b: (b, 0, 0)),
                  pl.BlockSpec((D, E), lambda b: (0, 0))],
        out_specs=pl.BlockSpec((1, M, E), lambda b: (b, 0, 0)),
        out_shape=jax.ShapeDtypeStruct((B, M, E), BF16),
        compiler_params=pltpu.CompilerParams(dimension_semantics=("arbitrary",),
                                             vmem_limit_bytes=VMEM_LIMIT),
        name="kv_project",
    )(mem, w_kv)


def _finish_tile(alpha, T, g_step, xr_ref, y_ref, ln_g_ref, ln_b_ref, w_route_ref, b_route_ref, cnt_ref,
                 x1_ref, xp_ref, route_ref, route_t_ref, counts_ref):
    z = alpha * xr_ref[0] + y_ref[...]
    mu = jnp.mean(z, axis=-1, keepdims=True)
    zc = z - mu
    var = jnp.mean(zc * zc, axis=-1, keepdims=True)
    x1 = zc * lax.rsqrt(var + EPS) * ln_g_ref[...] + ln_b_ref[...]
    x1_ref[0] = x1

    xh = x1.astype(BF16)
    xl = (x1 - xh.astype(F32)).astype(BF16)
    _store_rows(xp_ref, x1, x1.shape[1] // LANES)
    lg = jnp.dot(jnp.concatenate([xh, xl], axis=0), w_route_ref[...], preferred_element_type=F32)
    top = lg[:T]
    logits = top + lg[T:] + pltpu.roll(top, LANES - ROUTE_LO, 1) + b_route_ref[...]
    lane_i = lax.broadcasted_iota(jnp.int32, (T, LANES), 1)
    lane = lane_i.astype(F32)
    lane_grp = ((lane_i - N_GROUPS) >> 3).astype(F32)

    is_g = lane_i < N_GROUPS
    gl = jnp.where(is_g, logits, NEG)
    gmax = jnp.max(gl, axis=-1, keepdims=True)
    gsum = jnp.sum(jnp.where(is_g, jnp.exp(gl - gmax), 0.0), axis=-1, keepdims=True)
    g_p = 1.0 / gsum
    g_idx = jnp.min(jnp.where(gl == gmax, lane, float(LANES)), axis=-1, keepdims=True)

    in_grp = (lane_i >= N_GROUPS) & (lane_i < N_GROUPS + N_EXPERTS) & (lane_grp == g_idx)
    fl = jnp.where(in_grp, logits, NEG)
    l1 = jnp.max(fl, axis=-1, keepdims=True)
    i1 = jnp.min(jnp.where(fl == l1, lane, float(LANES)), axis=-1, keepdims=True)
    fl2 = jnp.where(lane == i1, NEG, fl)
    l2 = jnp.max(fl2, axis=-1, keepdims=True)
    i2 = jnp.min(jnp.where(fl2 == l2, lane, float(LANES)), axis=-1, keepdims=True)
    e21 = jnp.exp(l2 - l1)
    w1 = g_p / (1.0 + e21)
    w2 = g_p * e21 / (1.0 + e21)
    e1 = i1 - N_GROUPS
    e2 = i2 - N_GROUPS

    oh1 = (lane == e1).astype(F32)
    oh2 = (lane == e2).astype(F32)
    both = (oh1 + oh2) * (g_step > 0).astype(F32)
    tri = (lax.broadcasted_iota(jnp.int32, (T, T), 0) > lax.broadcasted_iota(jnp.int32, (T, T), 1))
    pfx = jnp.dot(tri.astype(F32).astype(BF16), both.astype(BF16), preferred_element_type=F32)
    base = pfx + cnt_ref[...]
    r1 = jnp.sum(oh1 * base, axis=-1, keepdims=True)
    r2 = jnp.sum(oh2 * base, axis=-1, keepdims=True)
    cnt_ref[...] = cnt_ref[...] + jnp.sum(both, axis=0, keepdims=True)
    counts_ref[...] = jnp.broadcast_to(cnt_ref[...], counts_ref.shape)

    route = jnp.where(lane_i == 0, e1,
            jnp.where(lane_i == 1, e2,
            jnp.where(lane_i == 2, r1,
            jnp.where(lane_i == 3, r2,
            jnp.where(lane_i == 4, w1,
            jnp.where(lane_i == 5, w2, 0.0))))))
    route_ref[0] = route
    route_t_ref[0] = route.T[:8]


def _mixer_kernel(dims, alpha, n_s,
                  xr_ref, xn_ref, kv_ref, w_in_ref, w_pool_ref, pool_scale_ref, conv_w_ref, conv_b_ref,
                  w_ax_ref, b_ax_ref, nsp_ref, norm_g_ref, w_out_ref, ln_g_ref, ln_b_ref,
                  w_route_ref, b_route_ref,
                  x1_ref, xp_ref, route_ref, route_t_ref, counts_ref,
                  h_ref, hn_ref, mix_ref, mixb_ref, mixn_ref, y_ref, halo_ref, state_ref, cnt_ref):
    T, pool_w, lru_w, mem_w, n_heads, head_dim, gdim, bdim = dims
    g_step = pl.program_id(0)
    s = lax.rem(g_step, n_s)
    o_lru = pool_w
    o_gate = o_lru + lru_w
    o_q = o_gate + lru_w

    @pl.when(s == 0)
    def _():
        halo_ref[...] = jnp.zeros_like(halo_ref)
        state_ref[...] = jnp.zeros_like(state_ref)

    @pl.when(g_step == 0)
    def _():
        cnt_ref[...] = jnp.zeros_like(cnt_ref)
        hn_ref[...] = jnp.dot(xr_ref[0].astype(BF16), w_in_ref[...], preferred_element_type=F32)
        mixn_ref[...] = jnp.zeros_like(mixn_ref)

    h_ref[...] = hn_ref[...]
    mixb_ref[...] = mixn_ref[...]
    xnb = xn_ref[0].astype(BF16)
    n_proj = w_in_ref.shape[1] // PROJ_CHUNK
    n_out = w_out_ref.shape[1] // PROJ_CHUNK
    done = {"proj": 0, "out": 0}

    def fill_mxu(n_in, n_o):
        for j in range(done["proj"], min(done["proj"] + n_in, n_proj)):
            cols = slice(j * PROJ_CHUNK, (j + 1) * PROJ_CHUNK)
            hn_ref[:, cols] = jnp.dot(xnb, w_in_ref[:, cols], preferred_element_type=F32)
        for j in range(done["out"], min(done["out"] + n_o, n_out)):
            cols = slice(j * PROJ_CHUNK, (j + 1) * PROJ_CHUNK)
            y_ref[:, cols] = jnp.dot(mixb_ref[...], w_out_ref[:, cols], preferred_element_type=F32)
        done["proj"] = min(done["proj"] + n_in, n_proj)
        done["out"] = min(done["out"] + n_o, n_out)

    row = lax.broadcasted_iota(jnp.int32, (T, 1), 0)
    pos = (s * T + row + 1).astype(F32)

    ext = jnp.concatenate([halo_ref[:, :pool_w], h_ref[:, :pool_w]], axis=0)
    ssq = jnp.zeros((T, LANES), F32)
    win = ext
    span = 1
    for g, w in enumerate(POOL_WINDOWS):
        while span < w:
            win = win + _shift_rows(win, span)
            span *= 2
        sw = win[HALO:, :gdim]
        if g + 1 < len(POOL_WINDOWS):
            win = win[:, gdim:]
        u = h_ref[:, g * gdim:(g + 1) * gdim]
        d = sw / jnp.minimum(pos, float(w)) - u
        y = jnp.dot(d.astype(BF16), w_pool_ref[g], preferred_element_type=F32)
        y = y * pool_scale_ref[:, g * gdim:(g + 1) * gdim]
        ssq = ssq + y * y
        mix_ref[:, g * gdim:(g + 1) * gdim] = y
    inv = lax.rsqrt(jnp.sum(ssq, axis=-1, keepdims=True) / pool_w + EPS)
    mixn_ref[:, :pool_w] = (mix_ref[:, :pool_w] * inv * norm_g_ref[:, :pool_w]).astype(BF16)
    fill_mxu(2, 0)

    ssq = jnp.zeros((T, LANES), F32)
    for hh in range(lru_w // bdim):
        c0 = hh * bdim
        ext = jnp.concatenate([halo_ref[:, o_lru + c0:o_lru + c0 + bdim],
                               h_ref[:, o_lru + c0:o_lru + c0 + bdim]], axis=0)
        cw = conv_w_ref[:, c0:c0 + bdim]
        uc = ext * cw[CONV_WIDTH - 1:CONV_WIDTH]
        for k in range(1, CONV_WIDTH):
            uc = uc + _shift_rows(ext, k) * cw[CONV_WIDTH - 1 - k:CONV_WIDTH - k]
        uc = uc[HALO:] + conv_b_ref[:, c0:c0 + bdim]
        ga = jnp.dot(uc.astype(BF16), w_ax_ref[hh], preferred_element_type=F32)
        r = _sigmoid(ga[:, :bdim] + b_ax_ref[:, c0:c0 + bdim])
        i = _sigmoid(ga[:, bdim:] + b_ax_ref[:, lru_w + c0:lru_w + c0 + bdim])
        log_a = -LRU_C * r * nsp_ref[:, c0:c0 + bdim]
        a = jnp.exp(log_a)
        mult = jnp.sqrt(jnp.maximum(1.0 - a * a, 0.0))
        bt = mult * i * uc
        hs = _linear_scan(a, bt, state_ref[:, c0:c0 + bdim])
        state_ref[:, c0:c0 + bdim] = hs[T - 1:T]
        y = _gelu_tanh(h_ref[:, o_gate + c0:o_gate + c0 + bdim]) * hs
        ssq = ssq + y * y
        mix_ref[:, o_lru + c0:o_lru + c0 + bdim] = y
        fill_mxu(1, 1)
    inv = lax.rsqrt(jnp.sum(ssq, axis=-1, keepdims=True) / lru_w + EPS)
    mixn_ref[:, o_lru:o_gate] = (mix_ref[:, o_lru:o_gate] * inv * norm_g_ref[:, o_lru:o_gate]).astype(BF16)

    halo_ref[...] = h_ref[T - HALO:, :o_gate]

    ssq = jnp.zeros((T, LANES), F32)
    m0 = o_gate
    for hh in range(n_heads):
        c0 = hh * head_dim
        q = h_ref[:, o_q + c0:o_q + c0 + head_dim].astype(BF16)
        kh = kv_ref[0, :, c0:c0 + head_dim]
        vh = kv_ref[0, :, mem_w + c0:mem_w + c0 + head_dim]
        sc = lax.dot_general(q, kh, (((1,), (1,)), ((), ())), preferred_element_type=F32) * (head_dim ** -0.5)
        p = jnp.exp(sc - jnp.max(sc, axis=-1, keepdims=True))
        p = p / jnp.sum(p, axis=-1, keepdims=True)
        y = jnp.dot(p.astype(BF16), vh, preferred_element_type=F32)
        ssq = ssq + y * y
        mix_ref[:, m0 + c0:m0 + c0 + head_dim] = y
        fill_mxu(1, 1)
    inv = lax.rsqrt(jnp.sum(ssq, axis=-1, keepdims=True) / mem_w + EPS)
    mixn_ref[:, m0:] = (mix_ref[:, m0:] * inv * norm_g_ref[:, m0:]).astype(BF16)

    fill_mxu(n_proj, n_out)
    _finish_tile(alpha, T, g_step, xr_ref, y_ref, ln_g_ref, ln_b_ref, w_route_ref, b_route_ref, cnt_ref,
                 x1_ref, xp_ref, route_ref, route_t_ref, counts_ref)


def _mixer(x, kv, w_in, w_pool, pool_scale, conv_w, conv_b, w_ax, b_ax, nsp, norm_g, w_out, ln_g, ln_b,
           w_route, b_route, alpha):
    B, S, D = x.shape
    T = SEQ_TILE
    in_w = w_in.shape[1]
    pool_w = pool_scale.shape[1]
    lru_w = conv_b.shape[1]
    mem_w = kv.shape[2] // 2
    gdim = w_pool.shape[1]
    bdim = w_ax.shape[1]
    n_heads = 4
    head_dim = mem_w // n_heads
    mix_w = w_out.shape[0]
    dims = (T, pool_w, lru_w, mem_w, n_heads, head_dim, gdim, bdim)
    assert S % T == 0 and T % 8 == 0
    assert gdim == bdim == head_dim == LANES and in_w % PROJ_CHUNK == 0

    const2 = lambda g: (0, 0)
    const3 = lambda g: (0, 0, 0)
    single = pl.Buffered(1)

    def full(a):
        return pl.BlockSpec(a.shape, const2 if a.ndim == 2 else const3, pipeline_mode=single)

    n_s = S // T
    n_tiles = B * n_s

    def tile_index(t):
        return (t // n_s, t % n_s, 0)

    prev_tile = lambda g: jnp.maximum(g - 1, 0)
    in_specs = [
        pl.BlockSpec((1, T, D), lambda g: tile_index(prev_tile(g))),
        pl.BlockSpec((1, T, D), lambda g: tile_index(jnp.minimum(g + 1, n_tiles - 1))),
        pl.BlockSpec((1,) + kv.shape[1:], lambda g: (jnp.minimum(g // n_s, B - 1), 0, 0)),
        full(w_in), full(w_pool), full(pool_scale), full(conv_w), full(conv_b), full(w_ax), full(b_ax),
        full(nsp), full(norm_g), full(w_out), full(ln_g), full(ln_b), full(w_route), full(b_route),
    ]
    tiles = _row_tiles(D)
    out_specs = [
        pl.BlockSpec((1, T, D), lambda g: tile_index(prev_tile(g))),
        pl.BlockSpec((T * tiles, LANES), lambda g: (prev_tile(g), 0)),
        pl.BlockSpec((1, T, LANES), lambda g: tile_index(prev_tile(g))),
        pl.BlockSpec((1, 8, T), lambda g: (prev_tile(g), 0, 0)),
        pl.BlockSpec((8, LANES), const2),
    ]
    out_shape = [
        jax.ShapeDtypeStruct((B, S, D), F32),
        jax.ShapeDtypeStruct((B * S * tiles, LANES), F32),
        jax.ShapeDtypeStruct((B, S, LANES), F32),
        jax.ShapeDtypeStruct((n_tiles, 8, T), F32),
        jax.ShapeDtypeStruct((8, LANES), F32),
    ]
    scratch = [
        pltpu.VMEM((T, in_w), F32),
        pltpu.VMEM((T, in_w), F32),
        pltpu.VMEM((T, mix_w), F32),
        pltpu.VMEM((T, mix_w), BF16),
        pltpu.VMEM((T, mix_w), BF16),
        pltpu.VMEM((T, D), F32),
        pltpu.VMEM((HALO, pool_w + lru_w), F32),
        pltpu.VMEM((1, lru_w), F32),
        pltpu.VMEM((1, LANES), F32),
    ]
    return pl.pallas_call(
        functools.partial(_mixer_kernel, dims, alpha, n_s),
        grid=(n_tiles + 1,),
        in_specs=in_specs,
        out_specs=out_specs,
        out_shape=out_shape,
        scratch_shapes=scratch,
        compiler_params=pltpu.CompilerParams(dimension_semantics=("arbitrary",),
                                             vmem_limit_bytes=VMEM_LIMIT),
        name="mixer_router",
    )(x, x, kv, w_in, w_pool, pool_scale, conv_w, conv_b, w_ax, b_ax, nsp, norm_g, w_out, ln_g, ln_b,
      w_route, b_route)


def _ffn_kernel(tiles, bexp_ref, first_ref, nexte_ref, nused_ref,
                idx_prv_ref, idx_cur_ref, idx_nxt_ref, xp_hbm, wg_hbm, wu_hbm, wd_hbm, ys_hbm,
                xbuf, obuf, stage_g, stage_u, stage_d, wg_ref, wu_ref, wd_ref, gsem, ssem, wsem):
    R = EXPERT_ROWS
    pitch = _vmem_pitch(tiles)
    step = pl.program_id(0)
    n_used = nused_ref[0]

    def weight_copies(e):
        return (pltpu.make_async_copy(wg_hbm.at[e], stage_g, wsem.at[0]),
                pltpu.make_async_copy(wu_hbm.at[e], stage_u, wsem.at[1]),
                pltpu.make_async_copy(wd_hbm.at[e], stage_d, wsem.at[2]))

    def gather_copy(src, i, sl):
        return pltpu.make_async_copy(xp_hbm.at[pl.ds(pl.multiple_of(src, tiles), tiles)],
                                     xbuf.at[sl, pl.ds(i * pitch, tiles)], gsem.at[sl])

    def scatter_copy(dst, i, sl):
        return pltpu.make_async_copy(obuf.at[sl, pl.ds(i * pitch, tiles)],
                                     ys_hbm.at[pl.ds(pl.multiple_of(dst, tiles), tiles)], ssem.at[sl])

    def wait_gathers(sl):
        pltpu.make_async_copy(xp_hbm.at[pl.ds(0, R * tiles)], xbuf.at[sl, pl.ds(0, R * tiles)], gsem.at[sl]).wait()

    def wait_scatters(sl):
        pltpu.make_async_copy(obuf.at[sl, pl.ds(0, R * tiles)], ys_hbm.at[pl.ds(0, R * tiles)], ssem.at[sl]).wait()

    @pl.when(step == 0)
    def _():
        for c in weight_copies(bexp_ref[0]):
            c.start(priority=WEIGHT_DMA_PRIORITY)
        for k in range(GATHER_AHEAD):
            for i in range(R):
                gather_copy(idx_cur_ref[0, 0, 2 * k * R + i], i, k).start()
        obuf[1] = jnp.zeros(obuf.shape[1:], F32)
        spare = ys_hbm.shape[0] - 2 * R * tiles
        for k in range(2):
            pltpu.make_async_copy(obuf.at[1, pl.ds(0, R * tiles)],
                                  ys_hbm.at[pl.ds(spare + k * R * tiles, R * tiles)], ssem.at[1]).start()
        for k in range(2):
            wait_scatters(1)

    def run_block(r, cur, dst_off, ahead_ref, ahead_off, prev_ref, prev_off):
        gslot = lax.rem(r, GATHER_SLOTS)

        @pl.when(first_ref[r] == 1)
        def _():
            for c in weight_copies(0):
                c.wait()
            n_chunks = 8
            kg = stage_g.shape[0] // n_chunks
            kd = stage_d.shape[0] // n_chunks

            def cast_chunk(j, carry):
                sg = pl.ds(pl.multiple_of(j * kg, kg), kg)
                sd = pl.ds(pl.multiple_of(j * kd, kd), kd)
                wg_ref[sg, :] = stage_g[sg, :].astype(BF16)
                wu_ref[sg, :] = stage_u[sg, :].astype(BF16)
                wd_ref[sd, :] = stage_d[sd, :].astype(BF16)
                return carry
            lax.fori_loop(0, n_chunks, cast_chunk, 0)

            @pl.when(nexte_ref[r] >= 0)
            def _():
                for c in weight_copies(nexte_ref[r]):
                    c.start(priority=WEIGHT_DMA_PRIORITY)

        wait_gathers(gslot)
        xb = _load_rows(xbuf.at[gslot], R, tiles, pitch).astype(BF16)

        aslot = lax.rem(r + GATHER_AHEAD, GATHER_SLOTS)
        for i in range(R):
            gather_copy(ahead_ref[0, 0, ahead_off + i], i, aslot).start()
        for i in range(R):
            scatter_copy(prev_ref[0, 0, prev_off + i], i, 1 - cur).start(priority=i % 2)

        g = jnp.dot(xb, wg_ref[...], preferred_element_type=F32)
        u = jnp.dot(xb, wu_ref[...], preferred_element_type=F32)

        @pl.when(r >= 1)
        def _():
            wait_scatters(cur)

        hdn = (g * _sigmoid(g)) * u
        y = jnp.dot(hdn.astype(BF16), wd_ref[...], preferred_element_type=F32)
        _store_rows(obuf.at[cur], y, pitch)

        @pl.when(r == n_used - 1)
        def _():
            for i in range(R):
                scatter_copy(idx_cur_ref[0, 0, dst_off + i], i, cur).start()
            wait_scatters(cur)
            wait_scatters(1 - cur)
            for k in range(1, GATHER_AHEAD + 1):
                wait_gathers(lax.rem(r + k, GATHER_SLOTS))

    for j in range(BLOCKS_PER_STEP):
        r = BLOCKS_PER_STEP * step + j
        ja = j + GATHER_AHEAD
        ahead = (idx_cur_ref, 2 * ja * R) if ja < BLOCKS_PER_STEP else (idx_nxt_ref, 2 * (ja - BLOCKS_PER_STEP) * R)
        prev = (idx_cur_ref, (2 * j - 1) * R) if j > 0 else (idx_prv_ref, (2 * BLOCKS_PER_STEP - 1) * R)
        pl.when(r < n_used)(functools.partial(run_block, r, j % 2, (2 * j + 1) * R, *ahead, *prev))


def _expert_ffn(xp, idx, plan, w_gate, w_up, w_down, n_rows_out):
    R = EXPERT_ROWS
    per_step = 2 * R * BLOCKS_PER_STEP
    assert idx.shape[0] % BLOCKS_PER_STEP == 0 and BLOCKS_PER_STEP % 2 == 0 and GATHER_AHEAD <= BLOCKS_PER_STEP
    idx = idx.reshape(idx.shape[0] // BLOCKS_PER_STEP, 1, per_step)
    nb = idx.shape[0]
    D, FF = w_gate.shape[1:]
    tiles = _row_tiles(D)
    pitch = _vmem_pitch(tiles)
    idx_block = (1, 1, per_step)
    grid_spec = pltpu.PrefetchScalarGridSpec(
        num_scalar_prefetch=4,
        grid=(nb,),
        in_specs=[
            pl.BlockSpec(idx_block, lambda r, *_: (jnp.maximum(r - 1, 0), 0, 0), memory_space=pltpu.SMEM),
            pl.BlockSpec(idx_block, lambda r, *_: (r, 0, 0), memory_space=pltpu.SMEM),
            pl.BlockSpec(idx_block, lambda r, *_: (jnp.minimum(r + 1, nb - 1), 0, 0), memory_space=pltpu.SMEM),
            pl.BlockSpec(memory_space=pl.ANY),
            pl.BlockSpec(memory_space=pl.ANY),
            pl.BlockSpec(memory_space=pl.ANY),
            pl.BlockSpec(memory_space=pl.ANY),
        ],
        out_specs=pl.BlockSpec(memory_space=pl.ANY),
        scratch_shapes=[
            pltpu.VMEM((GATHER_SLOTS, R * pitch, LANES), F32),
            pltpu.VMEM((2, R * pitch, LANES), F32),
            pltpu.VMEM((D, FF), F32),
            pltpu.VMEM((D, FF), F32),
            pltpu.VMEM((FF, D), F32),
            pltpu.VMEM((D, FF), BF16),
            pltpu.VMEM((D, FF), BF16),
            pltpu.VMEM((FF, D), BF16),
            pltpu.SemaphoreType.DMA((GATHER_SLOTS,)),
            pltpu.SemaphoreType.DMA((2,)),
            pltpu.SemaphoreType.DMA((3,)),
        ],
    )
    return pl.pallas_call(
        functools.partial(_ffn_kernel, tiles),
        grid_spec=grid_spec,
        out_shape=jax.ShapeDtypeStruct((n_rows_out * tiles, LANES), F32),
        compiler_params=pltpu.CompilerParams(dimension_semantics=("arbitrary",),
                                             vmem_limit_bytes=VMEM_LIMIT),
        name="expert_ffn",
    )(*plan, idx, idx, idx, xp, w_gate, w_up, w_down)


def _combine_kernel(alpha, tiles, x1_ref, y0_ref, y1_ref, route_ref, g_ref, b_ref, out_ref):
    T = x1_ref.shape[0]
    rt = route_ref[...]
    y0 = _load_rows(y0_ref, T, tiles, tiles)
    y1 = _load_rows(y1_ref, T, tiles, tiles)
    z = alpha * x1_ref[...] + rt[:, 4:5] * y0 + rt[:, 5:6] * y1
    mu = jnp.mean(z, axis=-1, keepdims=True)
    zc = z - mu
    var = jnp.mean(zc * zc, axis=-1, keepdims=True)
    out_ref[...] = zc * lax.rsqrt(var + EPS) * g_ref[...] + b_ref[...]


def _combine(x1, ys, route, ln_g, ln_b, alpha):
    N, D = x1.shape
    T = COMBINE_TILE
    assert N % T == 0
    off = N // T
    tiles = _row_tiles(D)
    return pl.pallas_call(
        functools.partial(_combine_kernel, alpha, tiles),
        grid=(N // T,),
        in_specs=[
            pl.BlockSpec((T, D), lambda i: (i, 0)),
            pl.BlockSpec((T * tiles, LANES), lambda i: (i, 0)),
            pl.BlockSpec((T * tiles, LANES), lambda i: (i + off, 0)),
            pl.BlockSpec((T, LANES), lambda i: (i, 0)),
            pl.BlockSpec((1, D), lambda i: (0, 0)),
            pl.BlockSpec((1, D), lambda i: (0, 0)),
        ],
        out_specs=pl.BlockSpec((T, D), lambda i: (i, 0)),
        out_shape=jax.ShapeDtypeStruct((N, D), F32),
        compiler_params=pltpu.CompilerParams(dimension_semantics=("arbitrary",),
                                             vmem_limit_bytes=VMEM_LIMIT),
        name="combine_ln",
    )(x1, ys, ys, route, ln_g, ln_b)


def _route_weights(w_group, w_fine):
    w = jnp.concatenate([w_group, w_fine], axis=1)
    wh = w.astype(BF16)
    wl = (w - wh.astype(F32)).astype(BF16)
    n = w.shape[1]
    out = jnp.zeros((w.shape[0], LANES), BF16)
    out = out.at[:, :n].set(wh)
    return out.at[:, ROUTE_LO:ROUTE_LO + n].set(wl)


def _layer(x, mem, w_in, w_pool, pool_scale, conv_w, conv_b, w_a, b_a, w_x, b_x, lam, w_mem_kv, mix_norm_g,
           w_out, ln1_g, ln1_b, w_group, b_group, w_fine, b_fine, w_gate, w_up, w_down, ln2_g, ln2_b, alpha):
    B, S, D = x.shape
    N = B * S
    R = EXPERT_ROWS
    row = lambda v: v.reshape(1, -1).astype(F32)

    kv = _kv_project(mem, w_mem_kv.astype(BF16))
    w_ax = jnp.concatenate([w_a, w_x], axis=2).astype(BF16)
    b_ax = jnp.concatenate([b_a, b_x]).reshape(1, -1)
    nsp = row(jax.nn.softplus(-lam.astype(F32)))
    b_route = jnp.zeros((1, LANES), F32).at[0, :N_GROUPS + N_EXPERTS].set(
        jnp.concatenate([b_group, b_fine.reshape(-1)]))
    x1, xp, route, route_t, counts = _mixer(
        x, kv, w_in.astype(BF16), w_pool.astype(BF16), row(pool_scale), conv_w, row(conv_b), w_ax, b_ax, nsp,
        row(mix_norm_g), w_out.astype(BF16), row(ln1_g), row(ln1_b), _route_weights(w_group, w_fine), b_route,
        alpha)
    x1 = x1.reshape(N, D)
    route = route.reshape(N, LANES)

    e_idx = route_t[:, 0:2, :].astype(jnp.int32)
    rank = route_t[:, 2:4, :].astype(jnp.int32)
    cnt = counts[0, :N_EXPERTS].astype(jnp.int32)
    padded = (cnt + R - 1) // R * R
    pad_ends = jnp.cumsum(padded)
    pad_starts = pad_ends - padded
    ex = jnp.arange(N_EXPERTS, dtype=jnp.int32)
    start_of = jnp.sum(jnp.where(e_idx[..., None] == ex, pad_starts, 0), axis=-1)
    dest = start_of + rank
    n_blocks = -(-(2 * N + N_EXPERTS * R) // (R * BLOCKS_PER_STEP)) * BLOCKS_PER_STEP
    P = n_blocks * R
    block_start = jnp.arange(n_blocks, dtype=jnp.int32) * R
    block_exp = jnp.minimum(jnp.sum((pad_ends[None, :] <= block_start[:, None]).astype(jnp.int32), axis=1),
                            N_EXPERTS - 1)
    tok_tile = route_t.shape[2]
    slot = (jnp.arange(N, dtype=jnp.int32).reshape(-1, 1, tok_tile)
            + jnp.array([0, N], jnp.int32).reshape(1, 2, 1))
    p = jnp.arange(P, dtype=jnp.int32)
    spare = 2 * N + ((p // R) % 2) * R + (p % R)
    buf_slot = spare.at[dest.reshape(-1)].set(slot.reshape(-1), unique_indices=True)
    buf_tok = jnp.where(buf_slot < 2 * N, buf_slot % N, 0)
    tiles = _row_tiles(D)
    idx = jnp.concatenate([buf_tok.reshape(n_blocks, 1, R), buf_slot.reshape(n_blocks, 1, R)], axis=2) * tiles

    n_used = (pad_ends[-1:] // R).astype(jnp.int32)
    blk = jnp.arange(n_blocks, dtype=jnp.int32)
    first = ((blk < n_used) & ((blk == 0) | (block_exp != jnp.roll(block_exp, 1)))).astype(jnp.int32)
    later_active = (padded[None, :] > 0) & (ex[None, :] > ex[:, None])
    next_active = jnp.min(jnp.where(later_active, ex[None, :], N_EXPERTS), axis=1)
    next_active = jnp.where(next_active == N_EXPERTS, -1, next_active).astype(jnp.int32)
    plan = (block_exp.astype(jnp.int32), first, next_active[block_exp], n_used)

    ys = _expert_ffn(xp, idx, plan, w_gate, w_up, w_down, 2 * N + 2 * R)
    out = _combine(x1, ys, route, row(ln2_g), row(ln2_b), alpha)
    return out.reshape(B, S, D)


def kernel(x, mem, w_in, w_pool, pool_scale, conv_w, conv_b, w_a, b_a, w_x, b_x, lam, w_mem_kv, mix_norm_g, w_out,
           ln1_g, ln1_b, w_group, b_group, w_fine, b_fine, w_gate, w_up, w_down, ln2_g, ln2_b):
    depth = w_in.shape[0]
    alpha = (2 * depth) ** 0.25
    for l in range(depth):
        x = _layer(x, mem, w_in[l], w_pool[l], pool_scale[l], conv_w[l], conv_b[l], w_a[l], b_a[l], w_x[l], b_x[l],
                   lam[l], w_mem_kv[l], mix_norm_g[l], w_out[l], ln1_g[l], ln1_b[l], w_group[l], b_group[l],
                   w_fine[l], b_fine[l], w_gate[l], w_up[l], w_down[l], ln2_g[l], ln2_b[l], alpha)
    return x
```

```python
import functools

import jax
import jax.numpy as jnp
from jax import lax
from jax.experimental import pallas as pl
from jax.experimental.pallas import tpu as pltpu

F32 = jnp.float32
BF16 = jnp.bfloat16

POOL_WINDOWS = (2, 4, 8, 16)
LRU_C = 8.0
CONV_WIDTH = 4
MEM_HEADS = 4
N_GROUPS = 4
EXPERTS_PER_GROUP = 8
N_EXPERTS = N_GROUPS * EXPERTS_PER_GROUP
GROUP_SHIFT = EXPERTS_PER_GROUP.bit_length() - 1
assert 1 << GROUP_SHIFT == EXPERTS_PER_GROUP
EPS = 1e-5

LANES = 128
HALO = 16
SEQ_TILE = 256
PROJ_CHUNK = 256
EXPERT_ROWS = 128
BLOCKS_PER_STEP = 4
GATHER_AHEAD = 4
GATHER_SLOTS = GATHER_AHEAD + 1
WEIGHT_DMA_PRIORITY = 1
WEIGHT_CAST_CHUNKS = 8
COMBINE_TILE = 512
ROUTE_LO = 64
ROUTE_E, ROUTE_RANK, ROUTE_W = 0, 2, 4
NEG = -1e30
VMEM_LIMIT = 56 * 1024 * 1024


def _sigmoid(v):
    return 1.0 / (1.0 + jnp.exp(-v))


def _gelu_tanh(v):
    return 0.5 * v * (1.0 + jnp.tanh(0.7978845608028654 * (v + 0.044715 * (v * v * v))))


def _shift_rows(v, k):
    return pltpu.roll(v, k, 0)


def _linear_scan(a, b, h0):
    n = a.shape[0]
    groups = n // 8
    a3 = a.reshape(groups, 8, LANES)
    b3 = b.reshape(groups, 8, LANES)
    sub = lax.broadcasted_iota(jnp.int32, (groups, 8, LANES), 1)
    for k in (1, 2, 4):
        keep = sub >= k
        b_prev = jnp.where(keep, pltpu.roll(b3, k, 1), 0.0)
        a_prev = jnp.where(keep, pltpu.roll(a3, k, 1), 1.0)
        b3 = b3 + a3 * b_prev
        a3 = a3 * a_prev
    carry = h0
    out = []
    for g in range(groups):
        hg = a3[g] * carry + b3[g]
        out.append(hg)
        carry = hg[7:8]
    return jnp.concatenate(out, axis=0)


def _row_tiles(d):
    tiles, rem = divmod(d, LANES)
    assert rem == 0 and tiles % 8 == 0, "a row must be whole (8, 128) tiles to be one contiguous DMA"
    return tiles


def _vmem_pitch(tiles):
    return tiles if (tiles // 8) % 2 else tiles + 8


def _store_rows(ref, v, pitch):
    n, d = v.shape
    for c in range(d // LANES):
        ref[pl.ds(c, n, stride=pitch), :] = v[:, c * LANES:(c + 1) * LANES]


def _load_rows(ref, n, tiles, pitch):
    return jnp.concatenate([ref[pl.ds(c, n, stride=pitch), :] for c in range(tiles)], axis=1)


def _kv_kernel(mem_ref, w_ref, kv_ref):
    kv_ref[0] = jnp.dot(mem_ref[0].astype(BF16), w_ref[...], preferred_element_type=F32).astype(BF16)


def _kv_project(mem, w_kv):
    B, M, D = mem.shape
    E = w_kv.shape[1]
    return pl.pallas_call(
        _kv_kernel,
        grid=(B,),
        in_specs=[pl.BlockSpec((1, M, D), lambda b: (b, 0, 0)),
                  pl.BlockSpec((D, E), lambda b: (0, 0))],
        out_specs=pl.BlockSpec((1, M, E), lambda b: (b, 0, 0)),
        out_shape=jax.ShapeDtypeStruct((B, M, E), BF16),
        compiler_params=pltpu.CompilerParams(dimension_semantics=("arbitrary",),
                                             vmem_limit_bytes=VMEM_LIMIT),
        name="kv_project",
    )(mem, w_kv)


def _finish_tile(alpha, T, g_step, xr_ref, y_ref, ln_g_ref, ln_b_ref, w_route_ref, b_route_ref, cnt_ref,
                 x1_ref, xp_ref, route_ref, route_t_ref, counts_ref):
    z = alpha * xr_ref[0] + y_ref[...]
    mu = jnp.mean(z, axis=-1, keepdims=True)
    zc = z - mu
    var = jnp.mean(zc * zc, axis=-1, keepdims=True)
    x1 = zc * lax.rsqrt(var + EPS) * ln_g_ref[...] + ln_b_ref[...]
    x1_ref[0] = x1

    xh = x1.astype(BF16)
    xl = (x1 - xh.astype(F32)).astype(BF16)
    _store_rows(xp_ref, x1, x1.shape[1] // LANES)
    lg = jnp.dot(jnp.concatenate([xh, xl], axis=0), w_route_ref[...], preferred_element_type=F32)
    top = lg[:T]
    logits = top + lg[T:] + pltpu.roll(top, LANES - ROUTE_LO, 1) + b_route_ref[...]
    lane_i = lax.broadcasted_iota(jnp.int32, (T, LANES), 1)
    lane = lane_i.astype(F32)
    lane_grp = ((lane_i - N_GROUPS) >> GROUP_SHIFT).astype(F32)

    is_g = lane_i < N_GROUPS
    gl = jnp.where(is_g, logits, NEG)
    gmax = jnp.max(gl, axis=-1, keepdims=True)
    gsum = jnp.sum(jnp.where(is_g, jnp.exp(gl - gmax), 0.0), axis=-1, keepdims=True)
    g_p = 1.0 / gsum
    g_idx = jnp.min(jnp.where(gl == gmax, lane, float(LANES)), axis=-1, keepdims=True)

    in_grp = (lane_i >= N_GROUPS) & (lane_i < N_GROUPS + N_EXPERTS) & (lane_grp == g_idx)
    fl = jnp.where(in_grp, logits, NEG)
    l1 = jnp.max(fl, axis=-1, keepdims=True)
    i1 = jnp.min(jnp.where(fl == l1, lane, float(LANES)), axis=-1, keepdims=True)
    fl2 = jnp.where(lane == i1, NEG, fl)
    l2 = jnp.max(fl2, axis=-1, keepdims=True)
    i2 = jnp.min(jnp.where(fl2 == l2, lane, float(LANES)), axis=-1, keepdims=True)
    e21 = jnp.exp(l2 - l1)
    w1 = g_p / (1.0 + e21)
    w2 = g_p * e21 / (1.0 + e21)
    e1 = i1 - N_GROUPS
    e2 = i2 - N_GROUPS

    oh1 = (lane == e1).astype(F32)
    oh2 = (lane == e2).astype(F32)
    both = (oh1 + oh2) * (g_step > 0).astype(F32)
    tri = (lax.broadcasted_iota(jnp.int32, (T, T), 0) > lax.broadcasted_iota(jnp.int32, (T, T), 1))
    pfx = jnp.dot(tri.astype(F32).astype(BF16), both.astype(BF16), preferred_element_type=F32)
    base = pfx + cnt_ref[...]
    r1 = jnp.sum(oh1 * base, axis=-1, keepdims=True)
    r2 = jnp.sum(oh2 * base, axis=-1, keepdims=True)
    cnt_ref[...] = cnt_ref[...] + jnp.sum(both, axis=0, keepdims=True)
    counts_ref[...] = jnp.broadcast_to(cnt_ref[...], counts_ref.shape)

    route = jnp.where(lane_i == ROUTE_E, e1,
            jnp.where(lane_i == ROUTE_E + 1, e2,
            jnp.where(lane_i == ROUTE_RANK, r1,
            jnp.where(lane_i == ROUTE_RANK + 1, r2,
            jnp.where(lane_i == ROUTE_W, w1,
            jnp.where(lane_i == ROUTE_W + 1, w2, 0.0))))))
    route_ref[0] = route
    route_t_ref[0] = route.T[:8]


def _mixer_kernel(dims, alpha, n_s,
                  xr_ref, xn_ref, kv_ref, w_in_ref, w_pool_ref, pool_scale_ref, conv_w_ref, conv_b_ref,
                  w_ax_ref, b_ax_ref, nsp_ref, norm_g_ref, w_out_ref, ln_g_ref, ln_b_ref,
                  w_route_ref, b_route_ref,
                  x1_ref, xp_ref, route_ref, route_t_ref, counts_ref,
                  h_ref, hn_ref, mix_ref, mixb_ref, mixn_ref, y_ref, halo_ref, state_ref, cnt_ref):
    T, pool_w, lru_w, mem_w, n_heads, head_dim, gdim, bdim = dims
    g_step = pl.program_id(0)
    s = lax.rem(g_step, n_s)
    o_lru = pool_w
    o_gate = o_lru + lru_w
    o_q = o_gate + lru_w

    @pl.when(s == 0)
    def _():
        halo_ref[...] = jnp.zeros_like(halo_ref)
        state_ref[...] = jnp.zeros_like(state_ref)

    @pl.when(g_step == 0)
    def _():
        cnt_ref[...] = jnp.zeros_like(cnt_ref)
        hn_ref[...] = jnp.dot(xr_ref[0].astype(BF16), w_in_ref[...], preferred_element_type=F32)
        mixn_ref[...] = jnp.zeros_like(mixn_ref)

    h_ref[...] = hn_ref[...]
    mixb_ref[...] = mixn_ref[...]
    xnb = xn_ref[0].astype(BF16)
    n_proj = w_in_ref.shape[1] // PROJ_CHUNK
    n_out = w_out_ref.shape[1] // PROJ_CHUNK
    done = {"proj": 0, "out": 0}

    def fill_mxu(n_in, n_o):
        for j in range(done["proj"], min(done["proj"] + n_in, n_proj)):
            cols = slice(j * PROJ_CHUNK, (j + 1) * PROJ_CHUNK)
            hn_ref[:, cols] = jnp.dot(xnb, w_in_ref[:, cols], preferred_element_type=F32)
        for j in range(done["out"], min(done["out"] + n_o, n_out)):
            cols = slice(j * PROJ_CHUNK, (j + 1) * PROJ_CHUNK)
            y_ref[:, cols] = jnp.dot(mixb_ref[...], w_out_ref[:, cols], preferred_element_type=F32)
        done["proj"] = min(done["proj"] + n_in, n_proj)
        done["out"] = min(done["out"] + n_o, n_out)

    row = lax.broadcasted_iota(jnp.int32, (T, 1), 0)
    pos = (s * T + row + 1).astype(F32)

    ext = jnp.concatenate([halo_ref[:, :pool_w], h_ref[:, :pool_w]], axis=0)
    ssq = jnp.zeros((T, LANES), F32)
    win = ext
    span = 1
    for g, w in enumerate(POOL_WINDOWS):
        while span < w:
            win = win + _shift_rows(win, span)
            span *= 2
        sw = win[HALO:, :gdim]
        if g + 1 < len(POOL_WINDOWS):
            win = win[:, gdim:]
        u = h_ref[:, g * gdim:(g + 1) * gdim]
        d = sw / jnp.minimum(pos, float(w)) - u
        y = jnp.dot(d.astype(BF16), w_pool_ref[g], preferred_element_type=F32)
        y = y * pool_scale_ref[:, g * gdim:(g + 1) * gdim]
        ssq = ssq + y * y
        mix_ref[:, g * gdim:(g + 1) * gdim] = y
    inv = lax.rsqrt(jnp.sum(ssq, axis=-1, keepdims=True) / pool_w + EPS)
    mixn_ref[:, :pool_w] = (mix_ref[:, :pool_w] * inv * norm_g_ref[:, :pool_w]).astype(BF16)
    fill_mxu(2, 0)

    ssq = jnp.zeros((T, LANES), F32)
    for hh in range(lru_w // bdim):
        c0 = hh * bdim
        ext = jnp.concatenate([halo_ref[:, o_lru + c0:o_lru + c0 + bdim],
                               h_ref[:, o_lru + c0:o_lru + c0 + bdim]], axis=0)
        cw = conv_w_ref[:, c0:c0 + bdim]
        uc = ext * cw[CONV_WIDTH - 1:CONV_WIDTH]
        for k in range(1, CONV_WIDTH):
            uc = uc + _shift_rows(ext, k) * cw[CONV_WIDTH - 1 - k:CONV_WIDTH - k]
        uc = uc[HALO:] + conv_b_ref[:, c0:c0 + bdim]
        ga = jnp.dot(uc.astype(BF16), w_ax_ref[hh], preferred_element_type=F32)
        r = _sigmoid(ga[:, :bdim] + b_ax_ref[:, c0:c0 + bdim])
        i = _sigmoid(ga[:, bdim:] + b_ax_ref[:, lru_w + c0:lru_w + c0 + bdim])
        log_a = -LRU_C * r * nsp_ref[:, c0:c0 + bdim]
        a = jnp.exp(log_a)
        mult = jnp.sqrt(jnp.maximum(1.0 - a * a, 0.0))
        bt = mult * i * uc
        hs = _linear_scan(a, bt, state_ref[:, c0:c0 + bdim])
        state_ref[:, c0:c0 + bdim] = hs[T - 1:T]
        y = _gelu_tanh(h_ref[:, o_gate + c0:o_gate + c0 + bdim]) * hs
        ssq = ssq + y * y
        mix_ref[:, o_lru + c0:o_lru + c0 + bdim] = y
        fill_mxu(1, 1)
    inv = lax.rsqrt(jnp.sum(ssq, axis=-1, keepdims=True) / lru_w + EPS)
    mixn_ref[:, o_lru:o_gate] = (mix_ref[:, o_lru:o_gate] * inv * norm_g_ref[:, o_lru:o_gate]).astype(BF16)

    halo_ref[...] = h_ref[T - HALO:, :o_gate]

    ssq = jnp.zeros((T, LANES), F32)
    m0 = o_gate
    for hh in range(n_heads):
        c0 = hh * head_dim
        q = h_ref[:, o_q + c0:o_q + c0 + head_dim].astype(BF16)
        kh = kv_ref[0, :, c0:c0 + head_dim]
        vh = kv_ref[0, :, mem_w + c0:mem_w + c0 + head_dim]
        sc = lax.dot_general(q, kh, (((1,), (1,)), ((), ())), preferred_element_type=F32) * (head_dim ** -0.5)
        p = jnp.exp(sc - jnp.max(sc, axis=-1, keepdims=True))
        p = p / jnp.sum(p, axis=-1, keepdims=True)
        y = jnp.dot(p.astype(BF16), vh, preferred_element_type=F32)
        ssq = ssq + y * y
        mix_ref[:, m0 + c0:m0 + c0 + head_dim] = y
        fill_mxu(1, 1)
    inv = lax.rsqrt(jnp.sum(ssq, axis=-1, keepdims=True) / mem_w + EPS)
    mixn_ref[:, m0:] = (mix_ref[:, m0:] * inv * norm_g_ref[:, m0:]).astype(BF16)

    fill_mxu(n_proj, n_out)
    _finish_tile(alpha, T, g_step, xr_ref, y_ref, ln_g_ref, ln_b_ref, w_route_ref, b_route_ref, cnt_ref,
                 x1_ref, xp_ref, route_ref, route_t_ref, counts_ref)


def _mixer(x, kv, w_in, w_pool, pool_scale, conv_w, conv_b, w_ax, b_ax, nsp, norm_g, w_out, ln_g, ln_b,
           w_route, b_route, alpha):
    B, S, D = x.shape
    T = SEQ_TILE
    in_w = w_in.shape[1]
    pool_w = pool_scale.shape[1]
    lru_w = conv_b.shape[1]
    mem_w = kv.shape[2] // 2
    gdim = w_pool.shape[1]
    bdim = w_ax.shape[1]
    n_heads = MEM_HEADS
    head_dim = mem_w // n_heads
    mix_w = w_out.shape[0]
    dims = (T, pool_w, lru_w, mem_w, n_heads, head_dim, gdim, bdim)
    assert S % T == 0 and T % 8 == 0
    assert gdim == bdim == head_dim == LANES and in_w % PROJ_CHUNK == 0

    const2 = lambda g: (0, 0)
    const3 = lambda g: (0, 0, 0)
    single = pl.Buffered(1)

    def full(a):
        return pl.BlockSpec(a.shape, const2 if a.ndim == 2 else const3, pipeline_mode=single)

    n_s = S // T
    n_tiles = B * n_s

    def tile_index(t):
        return (t // n_s, t % n_s, 0)

    prev_tile = lambda g: jnp.maximum(g - 1, 0)
    in_specs = [
        pl.BlockSpec((1, T, D), lambda g: tile_index(prev_tile(g))),
        pl.BlockSpec((1, T, D), lambda g: tile_index(jnp.minimum(g + 1, n_tiles - 1))),
        pl.BlockSpec((1,) + kv.shape[1:], lambda g: (jnp.minimum(g // n_s, B - 1), 0, 0)),
        full(w_in), full(w_pool), full(pool_scale), full(conv_w), full(conv_b), full(w_ax), full(b_ax),
        full(nsp), full(norm_g), full(w_out), full(ln_g), full(ln_b), full(w_route), full(b_route),
    ]
    tiles = _row_tiles(D)
    out_specs = [
        pl.BlockSpec((1, T, D), lambda g: tile_index(prev_tile(g))),
        pl.BlockSpec((T * tiles, LANES), lambda g: (prev_tile(g), 0)),
        pl.BlockSpec((1, T, LANES), lambda g: tile_index(prev_tile(g))),
        pl.BlockSpec((1, 8, T), lambda g: (prev_tile(g), 0, 0)),
        pl.BlockSpec((8, LANES), const2),
    ]
    out_shape = [
        jax.ShapeDtypeStruct((B, S, D), F32),
        jax.ShapeDtypeStruct((B * S * tiles, LANES), F32),
        jax.ShapeDtypeStruct((B, S, LANES), F32),
        jax.ShapeDtypeStruct((n_tiles, 8, T), F32),
        jax.ShapeDtypeStruct((8, LANES), F32),
    ]
    scratch = [
        pltpu.VMEM((T, in_w), F32),
        pltpu.VMEM((T, in_w), F32),
        pltpu.VMEM((T, mix_w), F32),
        pltpu.VMEM((T, mix_w), BF16),
        pltpu.VMEM((T, mix_w), BF16),
        pltpu.VMEM((T, D), F32),
        pltpu.VMEM((HALO, pool_w + lru_w), F32),
        pltpu.VMEM((1, lru_w), F32),
        pltpu.VMEM((1, LANES), F32),
    ]
    return pl.pallas_call(
        functools.partial(_mixer_kernel, dims, alpha, n_s),
        grid=(n_tiles + 1,),
        in_specs=in_specs,
        out_specs=out_specs,
        out_shape=out_shape,
        scratch_shapes=scratch,
        compiler_params=pltpu.CompilerParams(dimension_semantics=("arbitrary",),
                                             vmem_limit_bytes=VMEM_LIMIT),
        name="mixer_router",
    )(x, x, kv, w_in, w_pool, pool_scale, conv_w, conv_b, w_ax, b_ax, nsp, norm_g, w_out, ln_g, ln_b,
      w_route, b_route)


def _ffn_kernel(tiles, bexp_ref, first_ref, nexte_ref, nused_ref,
                idx_prv_ref, idx_cur_ref, idx_nxt_ref, xp_hbm, wg_hbm, wu_hbm, wd_hbm, ys_hbm,
                xbuf, obuf, stage_g, stage_u, stage_d, wg_ref, wu_ref, wd_ref, gsem, ssem, wsem):
    R = EXPERT_ROWS
    pitch = _vmem_pitch(tiles)
    step = pl.program_id(0)
    n_used = nused_ref[0]

    def weight_copies(e):
        return (pltpu.make_async_copy(wg_hbm.at[e], stage_g, wsem.at[0]),
                pltpu.make_async_copy(wu_hbm.at[e], stage_u, wsem.at[1]),
                pltpu.make_async_copy(wd_hbm.at[e], stage_d, wsem.at[2]))

    def gather_copy(src, i, sl):
        return pltpu.make_async_copy(xp_hbm.at[pl.ds(pl.multiple_of(src, tiles), tiles)],
                                     xbuf.at[sl, pl.ds(i * pitch, tiles)], gsem.at[sl])

    def scatter_copy(dst, i, sl):
        return pltpu.make_async_copy(obuf.at[sl, pl.ds(i * pitch, tiles)],
                                     ys_hbm.at[pl.ds(pl.multiple_of(dst, tiles), tiles)], ssem.at[sl])

    def wait_gathers(sl):
        pltpu.make_async_copy(xp_hbm.at[pl.ds(0, R * tiles)], xbuf.at[sl, pl.ds(0, R * tiles)], gsem.at[sl]).wait()

    def wait_scatters(sl):
        pltpu.make_async_copy(obuf.at[sl, pl.ds(0, R * tiles)], ys_hbm.at[pl.ds(0, R * tiles)], ssem.at[sl]).wait()

    @pl.when(step == 0)
    def _():
        for c in weight_copies(bexp_ref[0]):
            c.start(priority=WEIGHT_DMA_PRIORITY)
        for k in range(GATHER_AHEAD):
            for i in range(R):
                gather_copy(idx_cur_ref[0, 0, 2 * k * R + i], i, k).start()
        obuf[1] = jnp.zeros(obuf.shape[1:], F32)
        spare = ys_hbm.shape[0] - 2 * R * tiles
        for k in range(2):
            pltpu.make_async_copy(obuf.at[1, pl.ds(0, R * tiles)],
                                  ys_hbm.at[pl.ds(spare + k * R * tiles, R * tiles)], ssem.at[1]).start()
        for k in range(2):
            wait_scatters(1)

    def run_block(r, cur, dst_off, ahead_ref, ahead_off, prev_ref, prev_off):
        gslot = lax.rem(r, GATHER_SLOTS)

        @pl.when(first_ref[r] == 1)
        def _():
            for c in weight_copies(0):
                c.wait()
            n_chunks = WEIGHT_CAST_CHUNKS
            kg = stage_g.shape[0] // n_chunks
            kd = stage_d.shape[0] // n_chunks

            def cast_chunk(j, carry):
                sg = pl.ds(pl.multiple_of(j * kg, kg), kg)
                sd = pl.ds(pl.multiple_of(j * kd, kd), kd)
                wg_ref[sg, :] = stage_g[sg, :].astype(BF16)
                wu_ref[sg, :] = stage_u[sg, :].astype(BF16)
                wd_ref[sd, :] = stage_d[sd, :].astype(BF16)
                return carry
            lax.fori_loop(0, n_chunks, cast_chunk, 0)

            @pl.when(nexte_ref[r] >= 0)
            def _():
                for c in weight_copies(nexte_ref[r]):
                    c.start(priority=WEIGHT_DMA_PRIORITY)

        wait_gathers(gslot)
        xb = _load_rows(xbuf.at[gslot], R, tiles, pitch).astype(BF16)

        aslot = lax.rem(r + GATHER_AHEAD, GATHER_SLOTS)
        for i in range(R):
            gather_copy(ahead_ref[0, 0, ahead_off + i], i, aslot).start()
        for i in range(R):
            scatter_copy(prev_ref[0, 0, prev_off + i], i, 1 - cur).start(priority=i % 2)

        g = jnp.dot(xb, wg_ref[...], preferred_element_type=F32)
        u = jnp.dot(xb, wu_ref[...], preferred_element_type=F32)

        @pl.when(r >= 1)
        def _():
            wait_scatters(cur)

        hdn = (g * _sigmoid(g)) * u
        y = jnp.dot(hdn.astype(BF16), wd_ref[...], preferred_element_type=F32)
        _store_rows(obuf.at[cur], y, pitch)

        @pl.when(r == n_used - 1)
        def _():
            for i in range(R):
                scatter_copy(idx_cur_ref[0, 0, dst_off + i], i, cur).start()
            wait_scatters(cur)
            wait_scatters(1 - cur)
            for k in range(1, GATHER_AHEAD + 1):
                wait_gathers(lax.rem(r + k, GATHER_SLOTS))

    for j in range(BLOCKS_PER_STEP):
        r = BLOCKS_PER_STEP * step + j
        ja = j + GATHER_AHEAD
        ahead = (idx_cur_ref, 2 * ja * R) if ja < BLOCKS_PER_STEP else (idx_nxt_ref, 2 * (ja - BLOCKS_PER_STEP) * R)
        prev = (idx_cur_ref, (2 * j - 1) * R) if j > 0 else (idx_prv_ref, (2 * BLOCKS_PER_STEP - 1) * R)
        pl.when(r < n_used)(functools.partial(run_block, r, j % 2, (2 * j + 1) * R, *ahead, *prev))


def _expert_ffn(xp, idx, plan, w_gate, w_up, w_down, n_rows_out):
    R = EXPERT_ROWS
    per_step = 2 * R * BLOCKS_PER_STEP
    assert idx.shape[0] % BLOCKS_PER_STEP == 0 and BLOCKS_PER_STEP % 2 == 0 and GATHER_AHEAD <= BLOCKS_PER_STEP
    idx = idx.reshape(idx.shape[0] // BLOCKS_PER_STEP, 1, per_step)
    nb = idx.shape[0]
    D, FF = w_gate.shape[1:]
    tiles = _row_tiles(D)
    pitch = _vmem_pitch(tiles)
    idx_block = (1, 1, per_step)
    grid_spec = pltpu.PrefetchScalarGridSpec(
        num_scalar_prefetch=4,
        grid=(nb,),
        in_specs=[
            pl.BlockSpec(idx_block, lambda r, *_: (jnp.maximum(r - 1, 0), 0, 0), memory_space=pltpu.SMEM),
            pl.BlockSpec(idx_block, lambda r, *_: (r, 0, 0), memory_space=pltpu.SMEM),
            pl.BlockSpec(idx_block, lambda r, *_: (jnp.minimum(r + 1, nb - 1), 0, 0), memory_space=pltpu.SMEM),
            pl.BlockSpec(memory_space=pl.ANY),
            pl.BlockSpec(memory_space=pl.ANY),
            pl.BlockSpec(memory_space=pl.ANY),
            pl.BlockSpec(memory_space=pl.ANY),
        ],
        out_specs=pl.BlockSpec(memory_space=pl.ANY),
        scratch_shapes=[
            pltpu.VMEM((GATHER_SLOTS, R * pitch, LANES), F32),
            pltpu.VMEM((2, R * pitch, LANES), F32),
            pltpu.VMEM((D, FF), F32),
            pltpu.VMEM((D, FF), F32),
            pltpu.VMEM((FF, D), F32),
            pltpu.VMEM((D, FF), BF16),
            pltpu.VMEM((D, FF), BF16),
            pltpu.VMEM((FF, D), BF16),
            pltpu.SemaphoreType.DMA((GATHER_SLOTS,)),
            pltpu.SemaphoreType.DMA((2,)),
            pltpu.SemaphoreType.DMA((3,)),
        ],
    )
    return pl.pallas_call(
        functools.partial(_ffn_kernel, tiles),
        grid_spec=grid_spec,
        out_shape=jax.ShapeDtypeStruct((n_rows_out * tiles, LANES), F32),
        compiler_params=pltpu.CompilerParams(dimension_semantics=("arbitrary",),
                                             vmem_limit_bytes=VMEM_LIMIT),
        name="expert_ffn",
    )(*plan, idx, idx, idx, xp, w_gate, w_up, w_down)


def _combine_kernel(alpha, tiles, x1_ref, y0_ref, y1_ref, route_ref, g_ref, b_ref, out_ref):
    T = x1_ref.shape[0]
    rt = route_ref[...]
    y0 = _load_rows(y0_ref, T, tiles, tiles)
    y1 = _load_rows(y1_ref, T, tiles, tiles)
    z = alpha * x1_ref[...] + rt[:, ROUTE_W:ROUTE_W + 1] * y0 + rt[:, ROUTE_W + 1:ROUTE_W + 2] * y1
    mu = jnp.mean(z, axis=-1, keepdims=True)
    zc = z - mu
    var = jnp.mean(zc * zc, axis=-1, keepdims=True)
    out_ref[...] = zc * lax.rsqrt(var + EPS) * g_ref[...] + b_ref[...]


def _combine(x1, ys, route, ln_g, ln_b, alpha):
    N, D = x1.shape
    T = COMBINE_TILE
    assert N % T == 0
    off = N // T
    tiles = _row_tiles(D)
    return pl.pallas_call(
        functools.partial(_combine_kernel, alpha, tiles),
        grid=(N // T,),
        in_specs=[
            pl.BlockSpec((T, D), lambda i: (i, 0)),
            pl.BlockSpec((T * tiles, LANES), lambda i: (i, 0)),
            pl.BlockSpec((T * tiles, LANES), lambda i: (i + off, 0)),
            pl.BlockSpec((T, LANES), lambda i: (i, 0)),
            pl.BlockSpec((1, D), lambda i: (0, 0)),
            pl.BlockSpec((1, D), lambda i: (0, 0)),
        ],
        out_specs=pl.BlockSpec((T, D), lambda i: (i, 0)),
        out_shape=jax.ShapeDtypeStruct((N, D), F32),
        compiler_params=pltpu.CompilerParams(dimension_semantics=("arbitrary",),
                                             vmem_limit_bytes=VMEM_LIMIT),
        name="combine_ln",
    )(x1, ys, ys, route, ln_g, ln_b)


def _route_weights(w_group, w_fine):
    w = jnp.concatenate([w_group, w_fine], axis=1)
    wh = w.astype(BF16)
    wl = (w - wh.astype(F32)).astype(BF16)
    n = w.shape[1]
    out = jnp.zeros((w.shape[0], LANES), BF16)
    out = out.at[:, :n].set(wh)
    return out.at[:, ROUTE_LO:ROUTE_LO + n].set(wl)


def _layer(x, mem, w_in, w_pool, pool_scale, conv_w, conv_b, w_a, b_a, w_x, b_x, lam, w_mem_kv, mix_norm_g,
           w_out, ln1_g, ln1_b, w_group, b_group, w_fine, b_fine, w_gate, w_up, w_down, ln2_g, ln2_b, alpha):
    B, S, D = x.shape
    N = B * S
    R = EXPERT_ROWS
    row = lambda v: v.reshape(1, -1).astype(F32)

    kv = _kv_project(mem, w_mem_kv.astype(BF16))
    w_ax = jnp.concatenate([w_a, w_x], axis=2).astype(BF16)
    b_ax = jnp.concatenate([b_a, b_x]).reshape(1, -1)
    nsp = row(jax.nn.softplus(-lam.astype(F32)))
    b_route = jnp.zeros((1, LANES), F32).at[0, :N_GROUPS + N_EXPERTS].set(
        jnp.concatenate([b_group, b_fine.reshape(-1)]))
    x1, xp, route, route_t, counts = _mixer(
        x, kv, w_in.astype(BF16), w_pool.astype(BF16), row(pool_scale), conv_w, row(conv_b), w_ax, b_ax, nsp,
        row(mix_norm_g), w_out.astype(BF16), row(ln1_g), row(ln1_b), _route_weights(w_group, w_fine), b_route,
        alpha)
    x1 = x1.reshape(N, D)
    route = route.reshape(N, LANES)

    e_idx = route_t[:, ROUTE_E:ROUTE_E + 2, :].astype(jnp.int32)
    rank = route_t[:, ROUTE_RANK:ROUTE_RANK + 2, :].astype(jnp.int32)
    cnt = counts[0, :N_EXPERTS].astype(jnp.int32)
    padded = (cnt + R - 1) // R * R
    pad_ends = jnp.cumsum(padded)
    pad_starts = pad_ends - padded
    ex = jnp.arange(N_EXPERTS, dtype=jnp.int32)
    start_of = jnp.sum(jnp.where(e_idx[..., None] == ex, pad_starts, 0), axis=-1)
    dest = start_of + rank
    n_blocks = -(-(2 * N + N_EXPERTS * R) // (R * BLOCKS_PER_STEP)) * BLOCKS_PER_STEP
    P = n_blocks * R
    block_start = jnp.arange(n_blocks, dtype=jnp.int32) * R
    block_exp = jnp.minimum(jnp.sum((pad_ends[None, :] <= block_start[:, None]).astype(jnp.int32), axis=1),
                            N_EXPERTS - 1)
    tok_tile = route_t.shape[2]
    slot = (jnp.arange(N, dtype=jnp.int32).reshape(-1, 1, tok_tile)
            + jnp.array([0, N], jnp.int32).reshape(1, 2, 1))
    p = jnp.arange(P, dtype=jnp.int32)
    spare = 2 * N + ((p // R) % 2) * R + (p % R)
    buf_slot = spare.at[dest.reshape(-1)].set(slot.reshape(-1), unique_indices=True)
    buf_tok = jnp.where(buf_slot < 2 * N, buf_slot % N, 0)
    tiles = _row_tiles(D)
    idx = jnp.concatenate([buf_tok.reshape(n_blocks, 1, R), buf_slot.reshape(n_blocks, 1, R)], axis=2) * tiles

    n_used = (pad_ends[-1:] // R).astype(jnp.int32)
    blk = jnp.arange(n_blocks, dtype=jnp.int32)
    first = ((blk < n_used) & ((blk == 0) | (block_exp != jnp.roll(block_exp, 1)))).astype(jnp.int32)
    later_active = (padded[None, :] > 0) & (ex[None, :] > ex[:, None])
    next_active = jnp.min(jnp.where(later_active, ex[None, :], N_EXPERTS), axis=1)
    next_active = jnp.where(next_active == N_EXPERTS, -1, next_active).astype(jnp.int32)
    plan = (block_exp.astype(jnp.int32), first, next_active[block_exp], n_used)

    ys = _expert_ffn(xp, idx, plan, w_gate, w_up, w_down, 2 * N + 2 * R)
    out = _combine(x1, ys, route, row(ln2_g), row(ln2_b), alpha)
    return out.reshape(B, S, D)


def kernel(x, mem, w_in, w_pool, pool_scale, conv_w, conv_b, w_a, b_a, w_x, b_x, lam, w_mem_kv, mix_norm_g, w_out,
           ln1_g, ln1_b, w_group, b_group, w_fine, b_fine, w_gate, w_up, w_down, ln2_g, ln2_b):
    depth = w_in.shape[0]
    alpha = (2 * depth) ** 0.25
    for l in range(depth):
        x = _layer(x, mem, w_in[l], w_pool[l], pool_scale[l], conv_w[l], conv_b[l], w_a[l], b_a[l], w_x[l], b_x[l],
                   lam[l], w_mem_kv[l], mix_norm_g[l], w_out[l], ln1_g[l], ln1_b[l], w_group[l], b_group[l],
                   w_fine[l], b_fine[l], w_gate[l], w_up[l], w_down[l], ln2_g[l], ln2_b[l], alpha)
    return x
```

```python
import functools

import jax
import jax.numpy as jnp
from jax import lax
from jax.experimental import pallas as pl
from jax.experimental.pallas import tpu as pltpu

F32 = jnp.float32
BF16 = jnp.bfloat16

POOL_WINDOWS = (2, 4, 8, 16)
LRU_C = 8.0
CONV_WIDTH = 4
MEM_HEADS = 4
N_GROUPS = 4
EXPERTS_PER_GROUP = 8
N_EXPERTS = N_GROUPS * EXPERTS_PER_GROUP
GROUP_SHIFT = EXPERTS_PER_GROUP.bit_length() - 1
assert 1 << GROUP_SHIFT == EXPERTS_PER_GROUP
EPS = 1e-5

LANES = 128
HALO = 16
SEQ_TILE = 256
PROJ_CHUNK = 256
EXPERT_ROWS = 128
BLOCKS_PER_STEP = 4
GATHER_AHEAD = 4
GATHER_SLOTS = GATHER_AHEAD + 1
WEIGHT_DMA_PRIORITY = 1
WEIGHT_CAST_CHUNKS = 8
COMBINE_TILE = 512
ROUTE_LO = 64
ROUTE_E, ROUTE_RANK, ROUTE_W = 0, 2, 4
NEG = -1e30
VMEM_LIMIT = 56 * 1024 * 1024


def _sigmoid(v):
    return 1.0 / (1.0 + jnp.exp(-v))


def _gelu_tanh(v):
    return 0.5 * v * (1.0 + jnp.tanh(0.7978845608028654 * (v + 0.044715 * (v * v * v))))


def _shift_rows(v, k):
    return pltpu.roll(v, k, 0)


def _linear_scan(a, b, h0):
    n = a.shape[0]
    groups = n // 8
    a3 = a.reshape(groups, 8, LANES)
    b3 = b.reshape(groups, 8, LANES)
    sub = lax.broadcasted_iota(jnp.int32, (groups, 8, LANES), 1)
    for k in (1, 2, 4):
        keep = sub >= k
        b_prev = jnp.where(keep, pltpu.roll(b3, k, 1), 0.0)
        a_prev = jnp.where(keep, pltpu.roll(a3, k, 1), 1.0)
        b3 = b3 + a3 * b_prev
        a3 = a3 * a_prev
    carry = h0
    out = []
    for g in range(groups):
        hg = a3[g] * carry + b3[g]
        out.append(hg)
        carry = hg[7:8]
    return jnp.concatenate(out, axis=0)


def _row_tiles(d):
    tiles, rem = divmod(d, LANES)
    assert rem == 0 and tiles % 8 == 0, "a row must be whole (8, 128) tiles to be one contiguous DMA"
    return tiles


def _vmem_pitch(tiles):
    return tiles if (tiles // 8) % 2 else tiles + 8


def _store_rows(ref, v, pitch):
    n, d = v.shape
    for c in range(d // LANES):
        ref[pl.ds(c, n, stride=pitch), :] = v[:, c * LANES:(c + 1) * LANES]


def _load_rows(ref, n, tiles, pitch):
    return jnp.concatenate([ref[pl.ds(c, n, stride=pitch), :] for c in range(tiles)], axis=1)


def _kv_kernel(mem_ref, w_ref, kv_ref):
    kv_ref[0] = jnp.dot(mem_ref[0].astype(BF16), w_ref[...], preferred_element_type=F32).astype(BF16)


def _kv_project(mem, w_kv):
    B, M, D = mem.shape
    E = w_kv.shape[1]
    return pl.pallas_call(
        _kv_kernel,
        grid=(B,),
        in_specs=[pl.BlockSpec((1, M, D), lambda b: (b, 0, 0)),
                  pl.BlockSpec((D, E), lambda b: (0, 0))],
        out_specs=pl.BlockSpec((1, M, E), lambda b: (b, 0, 0)),
        out_shape=jax.ShapeDtypeStruct((B, M, E), BF16),
        compiler_params=pltpu.CompilerParams(dimension_semantics=("arbitrary",),
                                             vmem_limit_bytes=VMEM_LIMIT),
        name="kv_project",
    )(mem, w_kv)


def _finish_tile(alpha, T, g_step, xr_ref, y_ref, ln_g_ref, ln_b_ref, w_route_ref, b_route_ref, cnt_ref,
                 x1_ref, xp_ref, route_ref, route_t_ref, counts_ref, fill_mxu):
    z = alpha * xr_ref[0] + y_ref[...]
    mu = jnp.mean(z, axis=-1, keepdims=True)
    zc = z - mu
    var = jnp.mean(zc * zc, axis=-1, keepdims=True)
    x1 = zc * lax.rsqrt(var + EPS) * ln_g_ref[...] + ln_b_ref[...]
    x1_ref[0] = x1

    xh = x1.astype(BF16)
    xl = (x1 - xh.astype(F32)).astype(BF16)
    _store_rows(xp_ref, x1, x1.shape[1] // LANES)
    fill_mxu(2, 0)
    lg = jnp.dot(jnp.concatenate([xh, xl], axis=0), w_route_ref[...], preferred_element_type=F32)
    fill_mxu(1, 0)
    top = lg[:T]
    logits = top + lg[T:] + pltpu.roll(top, LANES - ROUTE_LO, 1) + b_route_ref[...]
    lane_i = lax.broadcasted_iota(jnp.int32, (T, LANES), 1)
    lane = lane_i.astype(F32)
    lane_grp = ((lane_i - N_GROUPS) >> GROUP_SHIFT).astype(F32)

    is_g = lane_i < N_GROUPS
    gl = jnp.where(is_g, logits, NEG)
    gmax = jnp.max(gl, axis=-1, keepdims=True)
    gsum = jnp.sum(jnp.where(is_g, jnp.exp(gl - gmax), 0.0), axis=-1, keepdims=True)
    g_p = 1.0 / gsum
    g_idx = jnp.min(jnp.where(gl == gmax, lane, float(LANES)), axis=-1, keepdims=True)

    in_grp = (lane_i >= N_GROUPS) & (lane_i < N_GROUPS + N_EXPERTS) & (lane_grp == g_idx)
    fl = jnp.where(in_grp, logits, NEG)
    l1 = jnp.max(fl, axis=-1, keepdims=True)
    i1 = jnp.min(jnp.where(fl == l1, lane, float(LANES)), axis=-1, keepdims=True)
    fl2 = jnp.where(lane == i1, NEG, fl)
    l2 = jnp.max(fl2, axis=-1, keepdims=True)
    i2 = jnp.min(jnp.where(fl2 == l2, lane, float(LANES)), axis=-1, keepdims=True)
    e21 = jnp.exp(l2 - l1)
    w1 = g_p / (1.0 + e21)
    w2 = g_p * e21 / (1.0 + e21)
    e1 = i1 - N_GROUPS
    e2 = i2 - N_GROUPS

    oh1 = (lane == e1).astype(F32)
    oh2 = (lane == e2).astype(F32)
    both = (oh1 + oh2) * (g_step > 0).astype(F32)
    tri = (lax.broadcasted_iota(jnp.int32, (T, T), 0) > lax.broadcasted_iota(jnp.int32, (T, T), 1))
    pfx = jnp.dot(tri.astype(F32).astype(BF16), both.astype(BF16), preferred_element_type=F32)
    base = pfx + cnt_ref[...]
    r1 = jnp.sum(oh1 * base, axis=-1, keepdims=True)
    r2 = jnp.sum(oh2 * base, axis=-1, keepdims=True)
    cnt_ref[...] = cnt_ref[...] + jnp.sum(both, axis=0, keepdims=True)
    counts_ref[...] = jnp.broadcast_to(cnt_ref[...], counts_ref.shape)

    route = jnp.where(lane_i == ROUTE_E, e1,
            jnp.where(lane_i == ROUTE_E + 1, e2,
            jnp.where(lane_i == ROUTE_RANK, r1,
            jnp.where(lane_i == ROUTE_RANK + 1, r2,
            jnp.where(lane_i == ROUTE_W, w1,
            jnp.where(lane_i == ROUTE_W + 1, w2, 0.0))))))
    route_ref[0] = route
    route_t_ref[0] = route.T[:8]


def _mixer_kernel(dims, alpha, n_s,
                  xr_ref, xn_ref, kv_ref, w_in_ref, w_pool_ref, pool_scale_ref, conv_w_ref, conv_b_ref,
                  w_ax_ref, b_ax_ref, nsp_ref, norm_g_ref, w_out_ref, ln_g_ref, ln_b_ref,
                  w_route_ref, b_route_ref,
                  x1_ref, xp_ref, route_ref, route_t_ref, counts_ref,
                  h_ref, hn_ref, mix_ref, mixb_ref, mixn_ref, y_ref, halo_ref, state_ref, cnt_ref):
    T, pool_w, lru_w, mem_w, n_heads, head_dim, gdim, bdim = dims
    g_step = pl.program_id(0)
    s = lax.rem(g_step, n_s)
    o_lru = pool_w
    o_gate = o_lru + lru_w
    o_q = o_gate + lru_w

    @pl.when(s == 0)
    def _():
        halo_ref[...] = jnp.zeros_like(halo_ref)
        state_ref[...] = jnp.zeros_like(state_ref)

    @pl.when(g_step == 0)
    def _():
        cnt_ref[...] = jnp.zeros_like(cnt_ref)
        hn_ref[...] = jnp.dot(xr_ref[0].astype(BF16), w_in_ref[...], preferred_element_type=F32)
        mixn_ref[...] = jnp.zeros_like(mixn_ref)

    h_ref[...] = hn_ref[...]
    mixb_ref[...] = mixn_ref[...]
    xnb = xn_ref[0].astype(BF16)
    n_proj = w_in_ref.shape[1] // PROJ_CHUNK
    n_out = w_out_ref.shape[1] // PROJ_CHUNK
    done = {"proj": 0, "out": 0}

    def fill_mxu(n_in, n_o):
        for j in range(done["proj"], min(done["proj"] + n_in, n_proj)):
            cols = slice(j * PROJ_CHUNK, (j + 1) * PROJ_CHUNK)
            hn_ref[:, cols] = jnp.dot(xnb, w_in_ref[:, cols], preferred_element_type=F32)
        for j in range(done["out"], min(done["out"] + n_o, n_out)):
            cols = slice(j * PROJ_CHUNK, (j + 1) * PROJ_CHUNK)
            y_ref[:, cols] = jnp.dot(mixb_ref[...], w_out_ref[:, cols], preferred_element_type=F32)
        done["proj"] = min(done["proj"] + n_in, n_proj)
        done["out"] = min(done["out"] + n_o, n_out)

    row = lax.broadcasted_iota(jnp.int32, (T, 1), 0)
    pos = (s * T + row + 1).astype(F32)

    ext = jnp.concatenate([halo_ref[:, :pool_w], h_ref[:, :pool_w]], axis=0)
    ssq = jnp.zeros((T, LANES), F32)
    win = ext
    span = 1
    for g, w in enumerate(POOL_WINDOWS):
        while span < w:
            win = win + _shift_rows(win, span)
            span *= 2
        sw = win[HALO:, :gdim]
        if g + 1 < len(POOL_WINDOWS):
            win = win[:, gdim:]
        u = h_ref[:, g * gdim:(g + 1) * gdim]
        d = sw / jnp.minimum(pos, float(w)) - u
        y = jnp.dot(d.astype(BF16), w_pool_ref[g], preferred_element_type=F32)
        y = y * pool_scale_ref[:, g * gdim:(g + 1) * gdim]
        ssq = ssq + y * y
        mix_ref[:, g * gdim:(g + 1) * gdim] = y
    inv = lax.rsqrt(jnp.sum(ssq, axis=-1, keepdims=True) / pool_w + EPS)
    mixn_ref[:, :pool_w] = (mix_ref[:, :pool_w] * inv * norm_g_ref[:, :pool_w]).astype(BF16)
    fill_mxu(1, 0)

    ssq = jnp.zeros((T, LANES), F32)
    for hh in range(lru_w // bdim):
        c0 = hh * bdim
        ext = jnp.concatenate([halo_ref[:, o_lru + c0:o_lru + c0 + bdim],
                               h_ref[:, o_lru + c0:o_lru + c0 + bdim]], axis=0)
        cw = conv_w_ref[:, c0:c0 + bdim]
        uc = ext * cw[CONV_WIDTH - 1:CONV_WIDTH]
        for k in range(1, CONV_WIDTH):
            uc = uc + _shift_rows(ext, k) * cw[CONV_WIDTH - 1 - k:CONV_WIDTH - k]
        uc = uc[HALO:] + conv_b_ref[:, c0:c0 + bdim]
        ga = jnp.dot(uc.astype(BF16), w_ax_ref[hh], preferred_element_type=F32)
        r = _sigmoid(ga[:, :bdim] + b_ax_ref[:, c0:c0 + bdim])
        i = _sigmoid(ga[:, bdim:] + b_ax_ref[:, lru_w + c0:lru_w + c0 + bdim])
        log_a = -LRU_C * r * nsp_ref[:, c0:c0 + bdim]
        a = jnp.exp(log_a)
        mult = jnp.sqrt(jnp.maximum(1.0 - a * a, 0.0))
        bt = mult * i * uc
        hs = _linear_scan(a, bt, state_ref[:, c0:c0 + bdim])
        state_ref[:, c0:c0 + bdim] = hs[T - 1:T]
        y = _gelu_tanh(h_ref[:, o_gate + c0:o_gate + c0 + bdim]) * hs
        ssq = ssq + y * y
        mix_ref[:, o_lru + c0:o_lru + c0 + bdim] = y
        fill_mxu(1, 1)
    inv = lax.rsqrt(jnp.sum(ssq, axis=-1, keepdims=True) / lru_w + EPS)
    mixn_ref[:, o_lru:o_gate] = (mix_ref[:, o_lru:o_gate] * inv * norm_g_ref[:, o_lru:o_gate]).astype(BF16)

    halo_ref[...] = h_ref[T - HALO:, :o_gate]

    ssq = jnp.zeros((T, LANES), F32)
    m0 = o_gate
    for hh in range(n_heads):
        c0 = hh * head_dim
        q = h_ref[:, o_q + c0:o_q + c0 + head_dim].astype(BF16)
        kh = kv_ref[0, :, c0:c0 + head_dim]
        vh = kv_ref[0, :, mem_w + c0:mem_w + c0 + head_dim]
        sc = lax.dot_general(q, kh, (((1,), (1,)), ((), ())), preferred_element_type=F32) * (head_dim ** -0.5)
        p = jnp.exp(sc - jnp.max(sc, axis=-1, keepdims=True))
        p = p / jnp.sum(p, axis=-1, keepdims=True)
        y = jnp.dot(p.astype(BF16), vh, preferred_element_type=F32)
        ssq = ssq + y * y
        mix_ref[:, m0 + c0:m0 + c0 + head_dim] = y
    inv = lax.rsqrt(jnp.sum(ssq, axis=-1, keepdims=True) / mem_w + EPS)
    mixn_ref[:, m0:] = (mix_ref[:, m0:] * inv * norm_g_ref[:, m0:]).astype(BF16)

    fill_mxu(0, n_out)
    _finish_tile(alpha, T, g_step, xr_ref, y_ref, ln_g_ref, ln_b_ref, w_route_ref, b_route_ref, cnt_ref,
                 x1_ref, xp_ref, route_ref, route_t_ref, counts_ref, fill_mxu)
    fill_mxu(n_proj, 0)


def _mixer(x, kv, w_in, w_pool, pool_scale, conv_w, conv_b, w_ax, b_ax, nsp, norm_g, w_out, ln_g, ln_b,
           w_route, b_route, alpha):
    B, S, D = x.shape
    T = SEQ_TILE
    in_w = w_in.shape[1]
    pool_w = pool_scale.shape[1]
    lru_w = conv_b.shape[1]
    mem_w = kv.shape[2] // 2
    gdim = w_pool.shape[1]
    bdim = w_ax.shape[1]
    n_heads = MEM_HEADS
    head_dim = mem_w // n_heads
    mix_w = w_out.shape[0]
    dims = (T, pool_w, lru_w, mem_w, n_heads, head_dim, gdim, bdim)
    assert S % T == 0 and T % 8 == 0
    assert gdim == bdim == head_dim == LANES and in_w % PROJ_CHUNK == 0

    const2 = lambda g: (0, 0)
    const3 = lambda g: (0, 0, 0)
    single = pl.Buffered(1)

    def full(a):
        return pl.BlockSpec(a.shape, const2 if a.ndim == 2 else const3, pipeline_mode=single)

    n_s = S // T
    n_tiles = B * n_s

    def tile_index(t):
        return (t // n_s, t % n_s, 0)

    prev_tile = lambda g: jnp.maximum(g - 1, 0)
    in_specs = [
        pl.BlockSpec((1, T, D), lambda g: tile_index(prev_tile(g))),
        pl.BlockSpec((1, T, D), lambda g: tile_index(jnp.minimum(g + 1, n_tiles - 1))),
        pl.BlockSpec((1,) + kv.shape[1:], lambda g: (jnp.minimum(g // n_s, B - 1), 0, 0)),
        full(w_in), full(w_pool), full(pool_scale), full(conv_w), full(conv_b), full(w_ax), full(b_ax),
        full(nsp), full(norm_g), full(w_out), full(ln_g), full(ln_b), full(w_route), full(b_route),
    ]
    tiles = _row_tiles(D)
    out_specs = [
        pl.BlockSpec((1, T, D), lambda g: tile_index(prev_tile(g))),
        pl.BlockSpec((T * tiles, LANES), lambda g: (prev_tile(g), 0)),
        pl.BlockSpec((1, T, LANES), lambda g: tile_index(prev_tile(g))),
        pl.BlockSpec((1, 8, T), lambda g: (prev_tile(g), 0, 0)),
        pl.BlockSpec((8, LANES), const2),
    ]
    out_shape = [
        jax.ShapeDtypeStruct((B, S, D), F32),
        jax.ShapeDtypeStruct((B * S * tiles, LANES), F32),
        jax.ShapeDtypeStruct((B, S, LANES), F32),
        jax.ShapeDtypeStruct((n_tiles, 8, T), F32),
        jax.ShapeDtypeStruct((8, LANES), F32),
    ]
    scratch = [
        pltpu.VMEM((T, in_w), F32),
        pltpu.VMEM((T, in_w), F32),
        pltpu.VMEM((T, mix_w), F32),
        pltpu.VMEM((T, mix_w), BF16),
        pltpu.VMEM((T, mix_w), BF16),
        pltpu.VMEM((T, D), F32),
        pltpu.VMEM((HALO, pool_w + lru_w), F32),
        pltpu.VMEM((1, lru_w), F32),
        pltpu.VMEM((1, LANES), F32),
    ]
    return pl.pallas_call(
        functools.partial(_mixer_kernel, dims, alpha, n_s),
        grid=(n_tiles + 1,),
        in_specs=in_specs,
        out_specs=out_specs,
        out_shape=out_shape,
        scratch_shapes=scratch,
        compiler_params=pltpu.CompilerParams(dimension_semantics=("arbitrary",),
                                             vmem_limit_bytes=VMEM_LIMIT),
        name="mixer_router",
    )(x, x, kv, w_in, w_pool, pool_scale, conv_w, conv_b, w_ax, b_ax, nsp, norm_g, w_out, ln_g, ln_b,
      w_route, b_route)


def _ffn_kernel(tiles, bexp_ref, first_ref, nexte_ref, nused_ref,
                idx_prv_ref, idx_cur_ref, idx_nxt_ref, xp_hbm, wg_hbm, wu_hbm, wd_hbm, ys_hbm,
                xbuf, obuf, stage_g, stage_u, stage_d, wg_ref, wu_ref, wd_ref, gsem, ssem, wsem):
    R = EXPERT_ROWS
    pitch = _vmem_pitch(tiles)
    step = pl.program_id(0)
    n_used = nused_ref[0]

    def weight_copies(e):
        return (pltpu.make_async_copy(wg_hbm.at[e], stage_g, wsem.at[0]),
                pltpu.make_async_copy(wu_hbm.at[e], stage_u, wsem.at[1]),
                pltpu.make_async_copy(wd_hbm.at[e], stage_d, wsem.at[2]))

    def gather_copy(src, i, sl):
        return pltpu.make_async_copy(xp_hbm.at[pl.ds(pl.multiple_of(src, tiles), tiles)],
                                     xbuf.at[sl, pl.ds(i * pitch, tiles)], gsem.at[sl])

    def scatter_copy(dst, i, sl):
        return pltpu.make_async_copy(obuf.at[sl, pl.ds(i * pitch, tiles)],
                                     ys_hbm.at[pl.ds(pl.multiple_of(dst, tiles), tiles)], ssem.at[sl])

    def wait_gathers(sl):
        pltpu.make_async_copy(xp_hbm.at[pl.ds(0, R * tiles)], xbuf.at[sl, pl.ds(0, R * tiles)], gsem.at[sl]).wait()

    def wait_scatters(sl):
        pltpu.make_async_copy(obuf.at[sl, pl.ds(0, R * tiles)], ys_hbm.at[pl.ds(0, R * tiles)], ssem.at[sl]).wait()

    @pl.when(step == 0)
    def _():
        for c in weight_copies(bexp_ref[0]):
            c.start(priority=WEIGHT_DMA_PRIORITY)
        for k in range(GATHER_AHEAD):
            for i in range(R):
                gather_copy(idx_cur_ref[0, 0, 2 * k * R + i], i, k).start()
        obuf[1] = jnp.zeros(obuf.shape[1:], F32)
        spare = ys_hbm.shape[0] - 2 * R * tiles
        for k in range(2):
            pltpu.make_async_copy(obuf.at[1, pl.ds(0, R * tiles)],
                                  ys_hbm.at[pl.ds(spare + k * R * tiles, R * tiles)], ssem.at[1]).start()
        for k in range(2):
            wait_scatters(1)

    def run_block(r, cur, dst_off, ahead_ref, ahead_off, prev_ref, prev_off):
        gslot = lax.rem(r, GATHER_SLOTS)

        @pl.when(first_ref[r] == 1)
        def _():
            for c in weight_copies(0):
                c.wait()
            n_chunks = WEIGHT_CAST_CHUNKS
            kg = stage_g.shape[0] // n_chunks
            kd = stage_d.shape[0] // n_chunks

            def cast_chunk(j, carry):
                sg = pl.ds(pl.multiple_of(j * kg, kg), kg)
                sd = pl.ds(pl.multiple_of(j * kd, kd), kd)
                wg_ref[sg, :] = stage_g[sg, :].astype(BF16)
                wu_ref[sg, :] = stage_u[sg, :].astype(BF16)
                wd_ref[sd, :] = stage_d[sd, :].astype(BF16)
                return carry
            lax.fori_loop(0, n_chunks, cast_chunk, 0)

            @pl.when(nexte_ref[r] >= 0)
            def _():
                for c in weight_copies(nexte_ref[r]):
                    c.start(priority=WEIGHT_DMA_PRIORITY)

        wait_gathers(gslot)
        xb = _load_rows(xbuf.at[gslot], R, tiles, pitch).astype(BF16)

        aslot = lax.rem(r + GATHER_AHEAD, GATHER_SLOTS)
        for i in range(R):
            gather_copy(ahead_ref[0, 0, ahead_off + i], i, aslot).start()
        for i in range(R):
            scatter_copy(prev_ref[0, 0, prev_off + i], i, 1 - cur).start(priority=i % 2)

        g = jnp.dot(xb, wg_ref[...], preferred_element_type=F32)
        u = jnp.dot(xb, wu_ref[...], preferred_element_type=F32)

        @pl.when(r >= 1)
        def _():
            wait_scatters(cur)

        hdn = (g * _sigmoid(g)) * u
        y = jnp.dot(hdn.astype(BF16), wd_ref[...], preferred_element_type=F32)
        _store_rows(obuf.at[cur], y, pitch)

        @pl.when(r == n_used - 1)
        def _():
            for i in range(R):
                scatter_copy(idx_cur_ref[0, 0, dst_off + i], i, cur).start()
            wait_scatters(cur)
            wait_scatters(1 - cur)
            for k in range(1, GATHER_AHEAD + 1):
                wait_gathers(lax.rem(r + k, GATHER_SLOTS))

    for j in range(BLOCKS_PER_STEP):
        r = BLOCKS_PER_STEP * step + j
        ja = j + GATHER_AHEAD
        ahead = (idx_cur_ref, 2 * ja * R) if ja < BLOCKS_PER_STEP else (idx_nxt_ref, 2 * (ja - BLOCKS_PER_STEP) * R)
        prev = (idx_cur_ref, (2 * j - 1) * R) if j > 0 else (idx_prv_ref, (2 * BLOCKS_PER_STEP - 1) * R)
        pl.when(r < n_used)(functools.partial(run_block, r, j % 2, (2 * j + 1) * R, *ahead, *prev))


def _expert_ffn(xp, idx, plan, w_gate, w_up, w_down, n_rows_out):
    R = EXPERT_ROWS
    per_step = 2 * R * BLOCKS_PER_STEP
    assert idx.shape[0] % BLOCKS_PER_STEP == 0 and BLOCKS_PER_STEP % 2 == 0 and GATHER_AHEAD <= BLOCKS_PER_STEP
    idx = idx.reshape(idx.shape[0] // BLOCKS_PER_STEP, 1, per_step)
    nb = idx.shape[0]
    D, FF = w_gate.shape[1:]
    tiles = _row_tiles(D)
    pitch = _vmem_pitch(tiles)
    idx_block = (1, 1, per_step)
    grid_spec = pltpu.PrefetchScalarGridSpec(
        num_scalar_prefetch=4,
        grid=(nb,),
        in_specs=[
            pl.BlockSpec(idx_block, lambda r, *_: (jnp.maximum(r - 1, 0), 0, 0), memory_space=pltpu.SMEM),
            pl.BlockSpec(idx_block, lambda r, *_: (r, 0, 0), memory_space=pltpu.SMEM),
            pl.BlockSpec(idx_block, lambda r, *_: (jnp.minimum(r + 1, nb - 1), 0, 0), memory_space=pltpu.SMEM),
            pl.BlockSpec(memory_space=pl.ANY),
            pl.BlockSpec(memory_space=pl.ANY),
            pl.BlockSpec(memory_space=pl.ANY),
            pl.BlockSpec(memory_space=pl.ANY),
        ],
        out_specs=pl.BlockSpec(memory_space=pl.ANY),
        scratch_shapes=[
            pltpu.VMEM((GATHER_SLOTS, R * pitch, LANES), F32),
            pltpu.VMEM((2, R * pitch, LANES), F32),
            pltpu.VMEM((D, FF), F32),
            pltpu.VMEM((D, FF), F32),
            pltpu.VMEM((FF, D), F32),
            pltpu.VMEM((D, FF), BF16),
            pltpu.VMEM((D, FF), BF16),
            pltpu.VMEM((FF, D), BF16),
            pltpu.SemaphoreType.DMA((GATHER_SLOTS,)),
            pltpu.SemaphoreType.DMA((2,)),
            pltpu.SemaphoreType.DMA((3,)),
        ],
    )
    return pl.pallas_call(
        functools.partial(_ffn_kernel, tiles),
        grid_spec=grid_spec,
        out_shape=jax.ShapeDtypeStruct((n_rows_out * tiles, LANES), F32),
        compiler_params=pltpu.CompilerParams(dimension_semantics=("arbitrary",),
                                             vmem_limit_bytes=VMEM_LIMIT),
        name="expert_ffn",
    )(*plan, idx, idx, idx, xp, w_gate, w_up, w_down)


def _combine_kernel(alpha, tiles, x1_ref, y0_ref, y1_ref, route_ref, g_ref, b_ref, out_ref):
    T = x1_ref.shape[0]
    rt = route_ref[...]
    y0 = _load_rows(y0_ref, T, tiles, tiles)
    y1 = _load_rows(y1_ref, T, tiles, tiles)
    z = alpha * x1_ref[...] + rt[:, ROUTE_W:ROUTE_W + 1] * y0 + rt[:, ROUTE_W + 1:ROUTE_W + 2] * y1
    mu = jnp.mean(z, axis=-1, keepdims=True)
    zc = z - mu
    var = jnp.mean(zc * zc, axis=-1, keepdims=True)
    out_ref[...] = zc * lax.rsqrt(var + EPS) * g_ref[...] + b_ref[...]


def _combine(x1, ys, route, ln_g, ln_b, alpha):
    N, D = x1.shape
    T = COMBINE_TILE
    assert N % T == 0
    off = N // T
    tiles = _row_tiles(D)
    return pl.pallas_call(
        functools.partial(_combine_kernel, alpha, tiles),
        grid=(N // T,),
        in_specs=[
            pl.BlockSpec((T, D), lambda i: (i, 0)),
            pl.BlockSpec((T * tiles, LANES), lambda i: (i, 0)),
            pl.BlockSpec((T * tiles, LANES), lambda i: (i + off, 0)),
            pl.BlockSpec((T, LANES), lambda i: (i, 0)),
            pl.BlockSpec((1, D), lambda i: (0, 0)),
            pl.BlockSpec((1, D), lambda i: (0, 0)),
        ],
        out_specs=pl.BlockSpec((T, D), lambda i: (i, 0)),
        out_shape=jax.ShapeDtypeStruct((N, D), F32),
        compiler_params=pltpu.CompilerParams(dimension_semantics=("arbitrary",),
                                             vmem_limit_bytes=VMEM_LIMIT),
        name="combine_ln",
    )(x1, ys, ys, route, ln_g, ln_b)


def _route_weights(w_group, w_fine):
    w = jnp.concatenate([w_group, w_fine], axis=1)
    wh = w.astype(BF16)
    wl = (w - wh.astype(F32)).astype(BF16)
    n = w.shape[1]
    out = jnp.zeros((w.shape[0], LANES), BF16)
    out = out.at[:, :n].set(wh)
    return out.at[:, ROUTE_LO:ROUTE_LO + n].set(wl)


def _layer(x, mem, w_in, w_pool, pool_scale, conv_w, conv_b, w_a, b_a, w_x, b_x, lam, w_mem_kv, mix_norm_g,
           w_out, ln1_g, ln1_b, w_group, b_group, w_fine, b_fine, w_gate, w_up, w_down, ln2_g, ln2_b, alpha):
    B, S, D = x.shape
    N = B * S
    R = EXPERT_ROWS
    row = lambda v: v.reshape(1, -1).astype(F32)

    kv = _kv_project(mem, w_mem_kv.astype(BF16))
    w_ax = jnp.concatenate([w_a, w_x], axis=2).astype(BF16)
    b_ax = jnp.concatenate([b_a, b_x]).reshape(1, -1)
    nsp = row(jax.nn.softplus(-lam.astype(F32)))
    b_route = jnp.zeros((1, LANES), F32).at[0, :N_GROUPS + N_EXPERTS].set(
        jnp.concatenate([b_group, b_fine.reshape(-1)]))
    x1, xp, route, route_t, counts = _mixer(
        x, kv, w_in.astype(BF16), w_pool.astype(BF16), row(pool_scale), conv_w, row(conv_b), w_ax, b_ax, nsp,
        row(mix_norm_g), w_out.astype(BF16), row(ln1_g), row(ln1_b), _route_weights(w_group, w_fine), b_route,
        alpha)
    x1 = x1.reshape(N, D)
    route = route.reshape(N, LANES)

    e_idx = route_t[:, ROUTE_E:ROUTE_E + 2, :].astype(jnp.int32)
    rank = route_t[:, ROUTE_RANK:ROUTE_RANK + 2, :].astype(jnp.int32)
    cnt = counts[0, :N_EXPERTS].astype(jnp.int32)
    padded = (cnt + R - 1) // R * R
    pad_ends = jnp.cumsum(padded)
    pad_starts = pad_ends - padded
    ex = jnp.arange(N_EXPERTS, dtype=jnp.int32)
    start_of = jnp.sum(jnp.where(e_idx[..., None] == ex, pad_starts, 0), axis=-1)
    dest = start_of + rank
    n_blocks = -(-(2 * N + N_EXPERTS * R) // (R * BLOCKS_PER_STEP)) * BLOCKS_PER_STEP
    P = n_blocks * R
    block_start = jnp.arange(n_blocks, dtype=jnp.int32) * R
    block_exp = jnp.minimum(jnp.sum((pad_ends[None, :] <= block_start[:, None]).astype(jnp.int32), axis=1),
                            N_EXPERTS - 1)
    tok_tile = route_t.shape[2]
    slot = (jnp.arange(N, dtype=jnp.int32).reshape(-1, 1, tok_tile)
            + jnp.array([0, N], jnp.int32).reshape(1, 2, 1))
    p = jnp.arange(P, dtype=jnp.int32)
    spare = 2 * N + ((p // R) % 2) * R + (p % R)
    buf_slot = spare.at[dest.reshape(-1)].set(slot.reshape(-1), unique_indices=True)
    buf_tok = jnp.where(buf_slot < 2 * N, buf_slot % N, 0)
    tiles = _row_tiles(D)
    idx = jnp.concatenate([buf_tok.reshape(n_blocks, 1, R), buf_slot.reshape(n_blocks, 1, R)], axis=2) * tiles

    n_used = (pad_ends[-1:] // R).astype(jnp.int32)
    blk = jnp.arange(n_blocks, dtype=jnp.int32)
    first = ((blk < n_used) & ((blk == 0) | (block_exp != jnp.roll(block_exp, 1)))).astype(jnp.int32)
    later_active = (padded[None, :] > 0) & (ex[None, :] > ex[:, None])
    next_active = jnp.min(jnp.where(later_active, ex[None, :], N_EXPERTS), axis=1)
    next_active = jnp.where(next_active == N_EXPERTS, -1, next_active).astype(jnp.int32)
    plan = (block_exp.astype(jnp.int32), first, next_active[block_exp], n_used)

    ys = _expert_ffn(xp, idx, plan, w_gate, w_up, w_down, 2 * N + 2 * R)
    out = _combine(x1, ys, route, row(ln2_g), row(ln2_b), alpha)
    return out.reshape(B, S, D)


def kernel(x, mem, w_in, w_pool, pool_scale, conv_w, conv_b, w_a, b_a, w_x, b_x, lam, w_mem_kv, mix_norm_g, w_out,
           ln1_g, ln1_b, w_group, b_group, w_fine, b_fine, w_gate, w_up, w_down, ln2_g, ln2_b):
    depth = w_in.shape[0]
    alpha = (2 * depth) ** 0.25
    for l in range(depth):
        x = _layer(x, mem, w_in[l], w_pool[l], pool_scale[l], conv_w[l], conv_b[l], w_a[l], b_a[l], w_x[l], b_x[l],
                   lam[l], w_mem_kv[l], mix_norm_g[l], w_out[l], ln1_g[l], ln1_b[l], w_group[l], b_group[l],
                   w_fine[l], b_fine[l], w_gate[l], w_up[l], w_down[l], ln2_g[l], ln2_b[l], alpha)
    return x
```

```python
import functools

import jax
import jax.numpy as jnp
from jax import lax
from jax.experimental import pallas as pl
from jax.experimental.pallas import tpu as pltpu

F32 = jnp.float32
BF16 = jnp.bfloat16

POOL_WINDOWS = (2, 4, 8, 16)
LRU_C = 8.0
CONV_WIDTH = 4
MEM_HEADS = 4
N_GROUPS = 4
EXPERTS_PER_GROUP = 8
N_EXPERTS = N_GROUPS * EXPERTS_PER_GROUP
GROUP_SHIFT = EXPERTS_PER_GROUP.bit_length() - 1
assert 1 << GROUP_SHIFT == EXPERTS_PER_GROUP
EPS = 1e-5

LANES = 128
HALO = 16
SEQ_TILE = 256
PROJ_CHUNK = 256
EXPERT_ROWS = 128
BLOCKS_PER_STEP = 4
GATHER_AHEAD = 4
GATHER_SLOTS = GATHER_AHEAD + 1
WEIGHT_DMA_PRIORITY = 1
WEIGHT_CAST_CHUNKS = 8
COMBINE_TILE = 256
COMBINE_BUFFERS = 3
ROUTE_LO = 64
ROUTE_E, ROUTE_RANK, ROUTE_W = 0, 2, 4
NEG = -1e30
VMEM_LIMIT = 56 * 1024 * 1024


def _sigmoid(v):
    return 1.0 / (1.0 + jnp.exp(-v))


def _gelu_tanh(v):
    return 0.5 * v * (1.0 + jnp.tanh(0.7978845608028654 * (v + 0.044715 * (v * v * v))))


def _shift_rows(v, k):
    return pltpu.roll(v, k, 0)


def _linear_scan(a, b, h0):
    n = a.shape[0]
    groups = n // 8
    a3 = a.reshape(groups, 8, LANES)
    b3 = b.reshape(groups, 8, LANES)
    sub = lax.broadcasted_iota(jnp.int32, (groups, 8, LANES), 1)
    for k in (1, 2, 4):
        keep = sub >= k
        b_prev = jnp.where(keep, pltpu.roll(b3, k, 1), 0.0)
        a_prev = jnp.where(keep, pltpu.roll(a3, k, 1), 1.0)
        b3 = b3 + a3 * b_prev
        a3 = a3 * a_prev
    carry = h0
    out = []
    for g in range(groups):
        hg = a3[g] * carry + b3[g]
        out.append(hg)
        carry = hg[7:8]
    return jnp.concatenate(out, axis=0)


def _row_tiles(d):
    tiles, rem = divmod(d, LANES)
    assert rem == 0 and tiles % 8 == 0, "a row must be whole (8, 128) tiles to be one contiguous DMA"
    return tiles


def _vmem_pitch(tiles):
    return tiles if (tiles // 8) % 2 else tiles + 8


def _store_rows(ref, v, pitch):
    n, d = v.shape
    for c in range(d // LANES):
        ref[pl.ds(c, n, stride=pitch), :] = v[:, c * LANES:(c + 1) * LANES]


def _load_rows(ref, n, tiles, pitch):
    return jnp.concatenate([ref[pl.ds(c, n, stride=pitch), :] for c in range(tiles)], axis=1)


def _kv_kernel(mem_ref, w_ref, kv_ref):
    kv_ref[0] = jnp.dot(mem_ref[0].astype(BF16), w_ref[...], preferred_element_type=F32).astype(BF16)


def _kv_project(mem, w_kv):
    B, M, D = mem.shape
    E = w_kv.shape[1]
    return pl.pallas_call(
        _kv_kernel,
        grid=(B,),
        in_specs=[pl.BlockSpec((1, M, D), lambda b: (b, 0, 0)),
                  pl.BlockSpec((D, E), lambda b: (0, 0))],
        out_specs=pl.BlockSpec((1, M, E), lambda b: (b, 0, 0)),
        out_shape=jax.ShapeDtypeStruct((B, M, E), BF16),
        compiler_params=pltpu.CompilerParams(dimension_semantics=("arbitrary",),
                                             vmem_limit_bytes=VMEM_LIMIT),
        name="kv_project",
    )(mem, w_kv)


def _finish_tile(alpha, T, g_step, xr_ref, y_ref, ln_g_ref, ln_b_ref, w_route_ref, b_route_ref, cnt_ref,
                 x1_ref, xp_ref, route_ref, route_t_ref, counts_ref):
    z = alpha * xr_ref[0] + y_ref[...]
    mu = jnp.mean(z, axis=-1, keepdims=True)
    zc = z - mu
    var = jnp.mean(zc * zc, axis=-1, keepdims=True)
    x1 = zc * lax.rsqrt(var + EPS) * ln_g_ref[...] + ln_b_ref[...]
    x1_ref[0] = x1

    xh = x1.astype(BF16)
    xl = (x1 - xh.astype(F32)).astype(BF16)
    _store_rows(xp_ref, x1, x1.shape[1] // LANES)
    lg = jnp.dot(jnp.concatenate([xh, xl], axis=0), w_route_ref[...], preferred_element_type=F32)
    top = lg[:T]
    logits = top + lg[T:] + pltpu.roll(top, LANES - ROUTE_LO, 1) + b_route_ref[...]
    lane_i = lax.broadcasted_iota(jnp.int32, (T, LANES), 1)
    lane = lane_i.astype(F32)
    lane_grp = ((lane_i - N_GROUPS) >> GROUP_SHIFT).astype(F32)

    is_g = lane_i < N_GROUPS
    gl = jnp.where(is_g, logits, NEG)
    gmax = jnp.max(gl, axis=-1, keepdims=True)
    gsum = jnp.sum(jnp.where(is_g, jnp.exp(gl - gmax), 0.0), axis=-1, keepdims=True)
    g_p = 1.0 / gsum
    g_idx = jnp.min(jnp.where(gl == gmax, lane, float(LANES)), axis=-1, keepdims=True)

    in_grp = (lane_i >= N_GROUPS) & (lane_i < N_GROUPS + N_EXPERTS) & (lane_grp == g_idx)
    fl = jnp.where(in_grp, logits, NEG)
    l1 = jnp.max(fl, axis=-1, keepdims=True)
    i1 = jnp.min(jnp.where(fl == l1, lane, float(LANES)), axis=-1, keepdims=True)
    fl2 = jnp.where(lane == i1, NEG, fl)
    l2 = jnp.max(fl2, axis=-1, keepdims=True)
    i2 = jnp.min(jnp.where(fl2 == l2, lane, float(LANES)), axis=-1, keepdims=True)
    e21 = jnp.exp(l2 - l1)
    w1 = g_p / (1.0 + e21)
    w2 = g_p * e21 / (1.0 + e21)
    e1 = i1 - N_GROUPS
    e2 = i2 - N_GROUPS

    oh1 = (lane == e1).astype(F32)
    oh2 = (lane == e2).astype(F32)
    both = (oh1 + oh2) * (g_step > 0).astype(F32)
    tri = (lax.broadcasted_iota(jnp.int32, (T, T), 0) > lax.broadcasted_iota(jnp.int32, (T, T), 1))
    pfx = jnp.dot(tri.astype(F32).astype(BF16), both.astype(BF16), preferred_element_type=F32)
    base = pfx + cnt_ref[...]
    r1 = jnp.sum(oh1 * base, axis=-1, keepdims=True)
    r2 = jnp.sum(oh2 * base, axis=-1, keepdims=True)
    cnt_ref[...] = cnt_ref[...] + jnp.sum(both, axis=0, keepdims=True)
    counts_ref[...] = jnp.broadcast_to(cnt_ref[...], counts_ref.shape)

    route = jnp.where(lane_i == ROUTE_E, e1,
            jnp.where(lane_i == ROUTE_E + 1, e2,
            jnp.where(lane_i == ROUTE_RANK, r1,
            jnp.where(lane_i == ROUTE_RANK + 1, r2,
            jnp.where(lane_i == ROUTE_W, w1,
            jnp.where(lane_i == ROUTE_W + 1, w2, 0.0))))))
    route_ref[0] = route
    route_t_ref[0] = route.T[:8]


def _mixer_kernel(dims, alpha, n_s,
                  xr_ref, xn_ref, kv_ref, w_in_ref, w_pool_ref, pool_scale_ref, conv_w_ref, conv_b_ref,
                  w_ax_ref, b_ax_ref, nsp_ref, norm_g_ref, w_out_ref, ln_g_ref, ln_b_ref,
                  w_route_ref, b_route_ref,
                  x1_ref, xp_ref, route_ref, route_t_ref, counts_ref,
                  h_ref, hn_ref, mix_ref, mixb_ref, mixn_ref, y_ref, halo_ref, state_ref, cnt_ref):
    T, pool_w, lru_w, mem_w, n_heads, head_dim, gdim, bdim = dims
    g_step = pl.program_id(0)
    s = lax.rem(g_step, n_s)
    o_lru = pool_w
    o_gate = o_lru + lru_w
    o_q = o_gate + lru_w

    @pl.when(s == 0)
    def _():
        halo_ref[...] = jnp.zeros_like(halo_ref)
        state_ref[...] = jnp.zeros_like(state_ref)

    @pl.when(g_step == 0)
    def _():
        cnt_ref[...] = jnp.zeros_like(cnt_ref)
        hn_ref[...] = jnp.dot(xr_ref[0].astype(BF16), w_in_ref[...], preferred_element_type=F32)
        mixn_ref[...] = jnp.zeros_like(mixn_ref)

    h_ref[...] = hn_ref[...]
    mixb_ref[...] = mixn_ref[...]
    xnb = xn_ref[0].astype(BF16)
    n_proj = w_in_ref.shape[1] // PROJ_CHUNK
    n_out = w_out_ref.shape[1] // PROJ_CHUNK
    done = {"proj": 0, "out": 0}

    def fill_mxu(n_in, n_o):
        for j in range(done["proj"], min(done["proj"] + n_in, n_proj)):
            cols = slice(j * PROJ_CHUNK, (j + 1) * PROJ_CHUNK)
            hn_ref[:, cols] = jnp.dot(xnb, w_in_ref[:, cols], preferred_element_type=F32)
        for j in range(done["out"], min(done["out"] + n_o, n_out)):
            cols = slice(j * PROJ_CHUNK, (j + 1) * PROJ_CHUNK)
            y_ref[:, cols] = jnp.dot(mixb_ref[...], w_out_ref[:, cols], preferred_element_type=F32)
        done["proj"] = min(done["proj"] + n_in, n_proj)
        done["out"] = min(done["out"] + n_o, n_out)

    row = lax.broadcasted_iota(jnp.int32, (T, 1), 0)
    pos = (s * T + row + 1).astype(F32)

    ext = jnp.concatenate([halo_ref[:, :pool_w], h_ref[:, :pool_w]], axis=0)
    ssq = jnp.zeros((T, LANES), F32)
    win = ext
    span = 1
    for g, w in enumerate(POOL_WINDOWS):
        while span < w:
            win = win + _shift_rows(win, span)
            span *= 2
        sw = win[HALO:, :gdim]
        if g + 1 < len(POOL_WINDOWS):
            win = win[:, gdim:]
        u = h_ref[:, g * gdim:(g + 1) * gdim]
        d = sw / jnp.minimum(pos, float(w)) - u
        y = jnp.dot(d.astype(BF16), w_pool_ref[g], preferred_element_type=F32)
        y = y * pool_scale_ref[:, g * gdim:(g + 1) * gdim]
        ssq = ssq + y * y
        mix_ref[:, g * gdim:(g + 1) * gdim] = y
    inv = lax.rsqrt(jnp.sum(ssq, axis=-1, keepdims=True) / pool_w + EPS)
    mixn_ref[:, :pool_w] = (mix_ref[:, :pool_w] * inv * norm_g_ref[:, :pool_w]).astype(BF16)
    fill_mxu(2, 0)

    ssq = jnp.zeros((T, LANES), F32)
    for hh in range(lru_w // bdim):
        c0 = hh * bdim
        ext = jnp.concatenate([halo_ref[:, o_lru + c0:o_lru + c0 + bdim],
                               h_ref[:, o_lru + c0:o_lru + c0 + bdim]], axis=0)
        cw = conv_w_ref[:, c0:c0 + bdim]
        uc = ext * cw[CONV_WIDTH - 1:CONV_WIDTH]
        for k in range(1, CONV_WIDTH):
            uc = uc + _shift_rows(ext, k) * cw[CONV_WIDTH - 1 - k:CONV_WIDTH - k]
        uc = uc[HALO:] + conv_b_ref[:, c0:c0 + bdim]
        ga = jnp.dot(uc.astype(BF16), w_ax_ref[hh], preferred_element_type=F32)
        r = _sigmoid(ga[:, :bdim] + b_ax_ref[:, c0:c0 + bdim])
        i = _sigmoid(ga[:, bdim:] + b_ax_ref[:, lru_w + c0:lru_w + c0 + bdim])
        log_a = -LRU_C * r * nsp_ref[:, c0:c0 + bdim]
        a = jnp.exp(log_a)
        mult = jnp.sqrt(jnp.maximum(1.0 - a * a, 0.0))
        bt = mult * i * uc
        hs = _linear_scan(a, bt, state_ref[:, c0:c0 + bdim])
        state_ref[:, c0:c0 + bdim] = hs[T - 1:T]
        y = _gelu_tanh(h_ref[:, o_gate + c0:o_gate + c0 + bdim]) * hs
        ssq = ssq + y * y
        mix_ref[:, o_lru + c0:o_lru + c0 + bdim] = y
        fill_mxu(1, 1)
    inv = lax.rsqrt(jnp.sum(ssq, axis=-1, keepdims=True) / lru_w + EPS)
    mixn_ref[:, o_lru:o_gate] = (mix_ref[:, o_lru:o_gate] * inv * norm_g_ref[:, o_lru:o_gate]).astype(BF16)

    halo_ref[...] = h_ref[T - HALO:, :o_gate]

    ssq = jnp.zeros((T, LANES), F32)
    m0 = o_gate
    for hh in range(n_heads):
        c0 = hh * head_dim
        q = h_ref[:, o_q + c0:o_q + c0 + head_dim].astype(BF16)
        kh = kv_ref[0, :, c0:c0 + head_dim]
        vh = kv_ref[0, :, mem_w + c0:mem_w + c0 + head_dim]
        sc = lax.dot_general(q, kh, (((1,), (1,)), ((), ())), preferred_element_type=F32) * (head_dim ** -0.5)
        p = jnp.exp(sc - jnp.max(sc, axis=-1, keepdims=True))
        p = p / jnp.sum(p, axis=-1, keepdims=True)
        y = jnp.dot(p.astype(BF16), vh, preferred_element_type=F32)
        ssq = ssq + y * y
        mix_ref[:, m0 + c0:m0 + c0 + head_dim] = y
        fill_mxu(1, 1)
    inv = lax.rsqrt(jnp.sum(ssq, axis=-1, keepdims=True) / mem_w + EPS)
    mixn_ref[:, m0:] = (mix_ref[:, m0:] * inv * norm_g_ref[:, m0:]).astype(BF16)

    fill_mxu(n_proj, n_out)
    _finish_tile(alpha, T, g_step, xr_ref, y_ref, ln_g_ref, ln_b_ref, w_route_ref, b_route_ref, cnt_ref,
                 x1_ref, xp_ref, route_ref, route_t_ref, counts_ref)


def _mixer(x, kv, w_in, w_pool, pool_scale, conv_w, conv_b, w_ax, b_ax, nsp, norm_g, w_out, ln_g, ln_b,
           w_route, b_route, alpha):
    B, S, D = x.shape
    T = SEQ_TILE
    in_w = w_in.shape[1]
    pool_w = pool_scale.shape[1]
    lru_w = conv_b.shape[1]
    mem_w = kv.shape[2] // 2
    gdim = w_pool.shape[1]
    bdim = w_ax.shape[1]
    n_heads = MEM_HEADS
    head_dim = mem_w // n_heads
    mix_w = w_out.shape[0]
    dims = (T, pool_w, lru_w, mem_w, n_heads, head_dim, gdim, bdim)
    assert S % T == 0 and T % 8 == 0
    assert gdim == bdim == head_dim == LANES and in_w % PROJ_CHUNK == 0

    const2 = lambda g: (0, 0)
    const3 = lambda g: (0, 0, 0)
    single = pl.Buffered(1)

    def full(a):
        return pl.BlockSpec(a.shape, const2 if a.ndim == 2 else const3, pipeline_mode=single)

    n_s = S // T
    n_tiles = B * n_s

    def tile_index(t):
        return (t // n_s, t % n_s, 0)

    prev_tile = lambda g: jnp.maximum(g - 1, 0)
    in_specs = [
        pl.BlockSpec((1, T, D), lambda g: tile_index(prev_tile(g))),
        pl.BlockSpec((1, T, D), lambda g: tile_index(jnp.minimum(g + 1, n_tiles - 1))),
        pl.BlockSpec((1,) + kv.shape[1:], lambda g: (jnp.minimum(g // n_s, B - 1), 0, 0)),
        full(w_in), full(w_pool), full(pool_scale), full(conv_w), full(conv_b), full(w_ax), full(b_ax),
        full(nsp), full(norm_g), full(w_out), full(ln_g), full(ln_b), full(w_route), full(b_route),
    ]
    tiles = _row_tiles(D)
    out_specs = [
        pl.BlockSpec((1, T, D), lambda g: tile_index(prev_tile(g))),
        pl.BlockSpec((T * tiles, LANES), lambda g: (prev_tile(g), 0)),
        pl.BlockSpec((1, T, LANES), lambda g: tile_index(prev_tile(g))),
        pl.BlockSpec((1, 8, T), lambda g: (prev_tile(g), 0, 0)),
        pl.BlockSpec((8, LANES), const2),
    ]
    out_shape = [
        jax.ShapeDtypeStruct((B, S, D), F32),
        jax.ShapeDtypeStruct((B * S * tiles, LANES), F32),
        jax.ShapeDtypeStruct((B, S, LANES), F32),
        jax.ShapeDtypeStruct((n_tiles, 8, T), F32),
        jax.ShapeDtypeStruct((8, LANES), F32),
    ]
    scratch = [
        pltpu.VMEM((T, in_w), F32),
        pltpu.VMEM((T, in_w), F32),
        pltpu.VMEM((T, mix_w), F32),
        pltpu.VMEM((T, mix_w), BF16),
        pltpu.VMEM((T, mix_w), BF16),
        pltpu.VMEM((T, D), F32),
        pltpu.VMEM((HALO, pool_w + lru_w), F32),
        pltpu.VMEM((1, lru_w), F32),
        pltpu.VMEM((1, LANES), F32),
    ]
    return pl.pallas_call(
        functools.partial(_mixer_kernel, dims, alpha, n_s),
        grid=(n_tiles + 1,),
        in_specs=in_specs,
        out_specs=out_specs,
        out_shape=out_shape,
        scratch_shapes=scratch,
        compiler_params=pltpu.CompilerParams(dimension_semantics=("arbitrary",),
                                             vmem_limit_bytes=VMEM_LIMIT),
        name="mixer_router",
    )(x, x, kv, w_in, w_pool, pool_scale, conv_w, conv_b, w_ax, b_ax, nsp, norm_g, w_out, ln_g, ln_b,
      w_route, b_route)


def _ffn_kernel(tiles, bexp_ref, first_ref, nexte_ref, nused_ref,
                idx_prv_ref, idx_cur_ref, idx_nxt_ref, xp_hbm, wg_hbm, wu_hbm, wd_hbm, ys_hbm,
                xbuf, obuf, stage_g, stage_u, stage_d, wg_ref, wu_ref, wd_ref, gsem, ssem, wsem):
    R = EXPERT_ROWS
    pitch = _vmem_pitch(tiles)
    step = pl.program_id(0)
    n_used = nused_ref[0]

    def weight_copies(e):
        return (pltpu.make_async_copy(wg_hbm.at[e], stage_g, wsem.at[0]),
                pltpu.make_async_copy(wu_hbm.at[e], stage_u, wsem.at[1]),
                pltpu.make_async_copy(wd_hbm.at[e], stage_d, wsem.at[2]))

    def gather_copy(src, i, sl):
        return pltpu.make_async_copy(xp_hbm.at[pl.ds(pl.multiple_of(src, tiles), tiles)],
                                     xbuf.at[sl, pl.ds(i * pitch, tiles)], gsem.at[sl])

    def scatter_copy(dst, i, sl):
        return pltpu.make_async_copy(obuf.at[sl, pl.ds(i * pitch, tiles)],
                                     ys_hbm.at[pl.ds(pl.multiple_of(dst, tiles), tiles)], ssem.at[sl])

    def wait_gathers(sl):
        pltpu.make_async_copy(xp_hbm.at[pl.ds(0, R * tiles)], xbuf.at[sl, pl.ds(0, R * tiles)], gsem.at[sl]).wait()

    def wait_scatters(sl):
        pltpu.make_async_copy(obuf.at[sl, pl.ds(0, R * tiles)], ys_hbm.at[pl.ds(0, R * tiles)], ssem.at[sl]).wait()

    @pl.when(step == 0)
    def _():
        for c in weight_copies(bexp_ref[0]):
            c.start(priority=WEIGHT_DMA_PRIORITY)
        for k in range(GATHER_AHEAD):
            for i in range(R):
                gather_copy(idx_cur_ref[0, 0, 2 * k * R + i], i, k).start()
        obuf[1] = jnp.zeros(obuf.shape[1:], F32)
        spare = ys_hbm.shape[0] - 2 * R * tiles
        for k in range(2):
            pltpu.make_async_copy(obuf.at[1, pl.ds(0, R * tiles)],
                                  ys_hbm.at[pl.ds(spare + k * R * tiles, R * tiles)], ssem.at[1]).start()
        for k in range(2):
            wait_scatters(1)

    def run_block(r, cur, dst_off, ahead_ref, ahead_off, prev_ref, prev_off):
        gslot = lax.rem(r, GATHER_SLOTS)

        @pl.when(first_ref[r] == 1)
        def _():
            for c in weight_copies(0):
                c.wait()
            n_chunks = WEIGHT_CAST_CHUNKS
            kg = stage_g.shape[0] // n_chunks
            kd = stage_d.shape[0] // n_chunks

            def cast_chunk(j, carry):
                sg = pl.ds(pl.multiple_of(j * kg, kg), kg)
                sd = pl.ds(pl.multiple_of(j * kd, kd), kd)
                wg_ref[sg, :] = stage_g[sg, :].astype(BF16)
                wu_ref[sg, :] = stage_u[sg, :].astype(BF16)
                wd_ref[sd, :] = stage_d[sd, :].astype(BF16)
                return carry
            lax.fori_loop(0, n_chunks, cast_chunk, 0)

            @pl.when(nexte_ref[r] >= 0)
            def _():
                for c in weight_copies(nexte_ref[r]):
                    c.start(priority=WEIGHT_DMA_PRIORITY)

        wait_gathers(gslot)
        xb = _load_rows(xbuf.at[gslot], R, tiles, pitch).astype(BF16)

        aslot = lax.rem(r + GATHER_AHEAD, GATHER_SLOTS)
        for i in range(R):
            gather_copy(ahead_ref[0, 0, ahead_off + i], i, aslot).start()
        for i in range(R):
            scatter_copy(prev_ref[0, 0, prev_off + i], i, 1 - cur).start(priority=i % 2)

        g = jnp.dot(xb, wg_ref[...], preferred_element_type=F32)
        u = jnp.dot(xb, wu_ref[...], preferred_element_type=F32)

        @pl.when(r >= 1)
        def _():
            wait_scatters(cur)

        hdn = (g * _sigmoid(g)) * u
        y = jnp.dot(hdn.astype(BF16), wd_ref[...], preferred_element_type=F32)
        _store_rows(obuf.at[cur], y, pitch)

        @pl.when(r == n_used - 1)
        def _():
            for i in range(R):
                scatter_copy(idx_cur_ref[0, 0, dst_off + i], i, cur).start()
            wait_scatters(cur)
            wait_scatters(1 - cur)
            for k in range(1, GATHER_AHEAD + 1):
                wait_gathers(lax.rem(r + k, GATHER_SLOTS))

    for j in range(BLOCKS_PER_STEP):
        r = BLOCKS_PER_STEP * step + j
        ja = j + GATHER_AHEAD
        ahead = (idx_cur_ref, 2 * ja * R) if ja < BLOCKS_PER_STEP else (idx_nxt_ref, 2 * (ja - BLOCKS_PER_STEP) * R)
        prev = (idx_cur_ref, (2 * j - 1) * R) if j > 0 else (idx_prv_ref, (2 * BLOCKS_PER_STEP - 1) * R)
        pl.when(r < n_used)(functools.partial(run_block, r, j % 2, (2 * j + 1) * R, *ahead, *prev))


def _expert_ffn(xp, idx, plan, w_gate, w_up, w_down, n_rows_out):
    R = EXPERT_ROWS
    per_step = 2 * R * BLOCKS_PER_STEP
    assert idx.shape[0] % BLOCKS_PER_STEP == 0 and BLOCKS_PER_STEP % 2 == 0 and GATHER_AHEAD <= BLOCKS_PER_STEP
    idx = idx.reshape(idx.shape[0] // BLOCKS_PER_STEP, 1, per_step)
    nb = idx.shape[0]
    D, FF = w_gate.shape[1:]
    tiles = _row_tiles(D)
    pitch = _vmem_pitch(tiles)
    idx_block = (1, 1, per_step)
    grid_spec = pltpu.PrefetchScalarGridSpec(
        num_scalar_prefetch=4,
        grid=(nb,),
        in_specs=[
            pl.BlockSpec(idx_block, lambda r, *_: (jnp.maximum(r - 1, 0), 0, 0), memory_space=pltpu.SMEM),
            pl.BlockSpec(idx_block, lambda r, *_: (r, 0, 0), memory_space=pltpu.SMEM),
            pl.BlockSpec(idx_block, lambda r, *_: (jnp.minimum(r + 1, nb - 1), 0, 0), memory_space=pltpu.SMEM),
            pl.BlockSpec(memory_space=pl.ANY),
            pl.BlockSpec(memory_space=pl.ANY),
            pl.BlockSpec(memory_space=pl.ANY),
            pl.BlockSpec(memory_space=pl.ANY),
        ],
        out_specs=pl.BlockSpec(memory_space=pl.ANY),
        scratch_shapes=[
            pltpu.VMEM((GATHER_SLOTS, R * pitch, LANES), F32),
            pltpu.VMEM((2, R * pitch, LANES), F32),
            pltpu.VMEM((D, FF), F32),
            pltpu.VMEM((D, FF), F32),
            pltpu.VMEM((FF, D), F32),
            pltpu.VMEM((D, FF), BF16),
            pltpu.VMEM((D, FF), BF16),
            pltpu.VMEM((FF, D), BF16),
            pltpu.SemaphoreType.DMA((GATHER_SLOTS,)),
            pltpu.SemaphoreType.DMA((2,)),
            pltpu.SemaphoreType.DMA((3,)),
        ],
    )
    return pl.pallas_call(
        functools.partial(_ffn_kernel, tiles),
        grid_spec=grid_spec,
        out_shape=jax.ShapeDtypeStruct((n_rows_out * tiles, LANES), F32),
        compiler_params=pltpu.CompilerParams(dimension_semantics=("arbitrary",),
                                             vmem_limit_bytes=VMEM_LIMIT),
        name="expert_ffn",
    )(*plan, idx, idx, idx, xp, w_gate, w_up, w_down)


def _combine_kernel(alpha, tiles, n_tok, x1_hbm, ys_hbm, route_ref, g_ref, b_ref, out_ref, xbuf, y0buf, y1buf, sem):
    T = out_ref.shape[0]
    i = pl.program_id(0)
    n = pl.num_programs(0)
    ahead = COMBINE_BUFFERS - 1

    def copies(j, slot):
        return (pltpu.make_async_copy(x1_hbm.at[pl.ds(j * T, T)], xbuf.at[slot], sem.at[0, slot]),
                pltpu.make_async_copy(ys_hbm.at[pl.ds(j * T * tiles, T * tiles)], y0buf.at[slot], sem.at[1, slot]),
                pltpu.make_async_copy(ys_hbm.at[pl.ds((n_tok + j * T) * tiles, T * tiles)], y1buf.at[slot],
                                      sem.at[2, slot]))

    @pl.when(i == 0)
    def _():
        for k in range(ahead):
            for c in copies(k, k):
                c.start()

    @pl.when(i + ahead < n)
    def _():
        for c in copies(i + ahead, lax.rem(i + ahead, COMBINE_BUFFERS)):
            c.start()

    slot = lax.rem(i, COMBINE_BUFFERS)
    for c in copies(i, slot):
        c.wait()
    rt = route_ref[...]
    y0 = _load_rows(y0buf.at[slot], T, tiles, tiles)
    y1 = _load_rows(y1buf.at[slot], T, tiles, tiles)
    z = alpha * xbuf[slot] + rt[:, ROUTE_W:ROUTE_W + 1] * y0 + rt[:, ROUTE_W + 1:ROUTE_W + 2] * y1
    mu = jnp.mean(z, axis=-1, keepdims=True)
    zc = z - mu
    var = jnp.mean(zc * zc, axis=-1, keepdims=True)
    out_ref[...] = zc * lax.rsqrt(var + EPS) * g_ref[...] + b_ref[...]


def _combine(x1, ys, route, ln_g, ln_b, alpha):
    N, D = x1.shape
    T = COMBINE_TILE
    assert N % T == 0
    assert N // T >= COMBINE_BUFFERS
    tiles = _row_tiles(D)
    return pl.pallas_call(
        functools.partial(_combine_kernel, alpha, tiles, N),
        grid=(N // T,),
        in_specs=[
            pl.BlockSpec(memory_space=pl.ANY),
            pl.BlockSpec(memory_space=pl.ANY),
            pl.BlockSpec((T, LANES), lambda i: (i, 0)),
            pl.BlockSpec((1, D), lambda i: (0, 0)),
            pl.BlockSpec((1, D), lambda i: (0, 0)),
        ],
        out_specs=pl.BlockSpec((T, D), lambda i: (i, 0)),
        out_shape=jax.ShapeDtypeStruct((N, D), F32),
        scratch_shapes=[
            pltpu.VMEM((COMBINE_BUFFERS, T, D), F32),
            pltpu.VMEM((COMBINE_BUFFERS, T * tiles, LANES), F32),
            pltpu.VMEM((COMBINE_BUFFERS, T * tiles, LANES), F32),
            pltpu.SemaphoreType.DMA((3, COMBINE_BUFFERS)),
        ],
        compiler_params=pltpu.CompilerParams(dimension_semantics=("arbitrary",),
                                             vmem_limit_bytes=VMEM_LIMIT),
        name="combine_ln",
    )(x1, ys, route, ln_g, ln_b)


def _route_weights(w_group, w_fine):
    w = jnp.concatenate([w_group, w_fine], axis=1)
    wh = w.astype(BF16)
    wl = (w - wh.astype(F32)).astype(BF16)
    n = w.shape[1]
    out = jnp.zeros((w.shape[0], LANES), BF16)
    out = out.at[:, :n].set(wh)
    return out.at[:, ROUTE_LO:ROUTE_LO + n].set(wl)


def _layer(x, mem, w_in, w_pool, pool_scale, conv_w, conv_b, w_a, b_a, w_x, b_x, lam, w_mem_kv, mix_norm_g,
           w_out, ln1_g, ln1_b, w_group, b_group, w_fine, b_fine, w_gate, w_up, w_down, ln2_g, ln2_b, alpha):
    B, S, D = x.shape
    N = B * S
    R = EXPERT_ROWS
    row = lambda v: v.reshape(1, -1).astype(F32)

    kv = _kv_project(mem, w_mem_kv.astype(BF16))
    w_ax = jnp.concatenate([w_a, w_x], axis=2).astype(BF16)
    b_ax = jnp.concatenate([b_a, b_x]).reshape(1, -1)
    nsp = row(jax.nn.softplus(-lam.astype(F32)))
    b_route = jnp.zeros((1, LANES), F32).at[0, :N_GROUPS + N_EXPERTS].set(
        jnp.concatenate([b_group, b_fine.reshape(-1)]))
    x1, xp, route, route_t, counts = _mixer(
        x, kv, w_in.astype(BF16), w_pool.astype(BF16), row(pool_scale), conv_w, row(conv_b), w_ax, b_ax, nsp,
        row(mix_norm_g), w_out.astype(BF16), row(ln1_g), row(ln1_b), _route_weights(w_group, w_fine), b_route,
        alpha)
    x1 = x1.reshape(N, D)
    route = route.reshape(N, LANES)

    e_idx = route_t[:, ROUTE_E:ROUTE_E + 2, :].astype(jnp.int32)
    rank = route_t[:, ROUTE_RANK:ROUTE_RANK + 2, :].astype(jnp.int32)
    cnt = counts[0, :N_EXPERTS].astype(jnp.int32)
    padded = (cnt + R - 1) // R * R
    pad_ends = jnp.cumsum(padded)
    pad_starts = pad_ends - padded
    ex = jnp.arange(N_EXPERTS, dtype=jnp.int32)
    start_of = jnp.sum(jnp.where(e_idx[..., None] == ex, pad_starts, 0), axis=-1)
    dest = start_of + rank
    n_blocks = -(-(2 * N + N_EXPERTS * R) // (R * BLOCKS_PER_STEP)) * BLOCKS_PER_STEP
    P = n_blocks * R
    block_start = jnp.arange(n_blocks, dtype=jnp.int32) * R
    block_exp = jnp.minimum(jnp.sum((pad_ends[None, :] <= block_start[:, None]).astype(jnp.int32), axis=1),
                            N_EXPERTS - 1)
    tok_tile = route_t.shape[2]
    slot = (jnp.arange(N, dtype=jnp.int32).reshape(-1, 1, tok_tile)
            + jnp.array([0, N], jnp.int32).reshape(1, 2, 1))
    p = jnp.arange(P, dtype=jnp.int32)
    spare = 2 * N + ((p // R) % 2) * R + (p % R)
    buf_slot = spare.at[dest.reshape(-1)].set(slot.reshape(-1), unique_indices=True)
    buf_tok = jnp.where(buf_slot < 2 * N, buf_slot % N, 0)
    tiles = _row_tiles(D)
    idx = jnp.concatenate([buf_tok.reshape(n_blocks, 1, R), buf_slot.reshape(n_blocks, 1, R)], axis=2) * tiles

    n_used = (pad_ends[-1:] // R).astype(jnp.int32)
    blk = jnp.arange(n_blocks, dtype=jnp.int32)
    first = ((blk < n_used) & ((blk == 0) | (block_exp != jnp.roll(block_exp, 1)))).astype(jnp.int32)
    later_active = (padded[None, :] > 0) & (ex[None, :] > ex[:, None])
    next_active = jnp.min(jnp.where(later_active, ex[None, :], N_EXPERTS), axis=1)
    next_active = jnp.where(next_active == N_EXPERTS, -1, next_active).astype(jnp.int32)
    plan = (block_exp.astype(jnp.int32), first, next_active[block_exp], n_used)

    ys = _expert_ffn(xp, idx, plan, w_gate, w_up, w_down, 2 * N + 2 * R)
    out = _combine(x1, ys, route, row(ln2_g), row(ln2_b), alpha)
    return out.reshape(B, S, D)


def kernel(x, mem, w_in, w_pool, pool_scale, conv_w, conv_b, w_a, b_a, w_x, b_x, lam, w_mem_kv, mix_norm_g, w_out,
           ln1_g, ln1_b, w_group, b_group, w_fine, b_fine, w_gate, w_up, w_down, ln2_g, ln2_b):
    depth = w_in.shape[0]
    alpha = (2 * depth) ** 0.25
    for l in range(depth):
        x = _layer(x, mem, w_in[l], w_pool[l], pool_scale[l], conv_w[l], conv_b[l], w_a[l], b_a[l], w_x[l], b_x[l],
                   lam[l], w_mem_kv[l], mix_norm_g[l], w_out[l], ln1_g[l], ln1_b[l], w_group[l], b_group[l],
                   w_fine[l], b_fine[l], w_gate[l], w_up[l], w_down[l], ln2_g[l], ln2_b[l], alpha)
    return x
```
